```python
import math
import jax
import jax.numpy as jnp
from jax import lax
import numpy as np

D_MODEL = 1024
BATCH = 16
SEQ = 4096
DEPTH = 4

CTX_LEN = 256
GRID_W = 64
NORM_EPS = 1e-6

N_HEADS = 8
N_KV_HEADS = 2
HEAD_DIM = 128
GROUP = N_HEADS // N_KV_HEADS
ROPE_HALF = HEAD_DIM // 2
ROPE_PAIRS_AXIS = HEAD_DIM // 4
ROPE_THETA = 10000.0
ATTN_SCALE = HEAD_DIM ** -0.5
Q_BLOCK = 128
Q_WIDTH = N_HEADS * HEAD_DIM
KV_WIDTH = N_KV_HEADS * HEAD_DIM

HY_WIDTH = D_MODEL // 2
HY_SHORT = 3
HY_BANDS = 16
HY_EMB = 1 + 2 * HY_BANDS
HY_FILTER_HIDDEN = 64
HY_FAST_DECAY_PCT = 0.3
HY_SLOW_DECAY_PCT = 1.5
HY_DECAY_TARGET = 1e-2

LRU_WIDTH = D_MODEL // 2
LRU_BLOCKS = 8
LRU_BLOCK = LRU_WIDTH // LRU_BLOCKS
LRU_CONV = 4
LRU_PAD_LEFT = 2
LRU_PAD_RIGHT = 1
LRU_C = 8.0

N_BRANCH = 3
N_EXPERTS = 16
EXPERT_FF = D_MODEL
EC_CAPACITY = 2

IN_NAMES = ('q', 'k', 'v', 'hy', 'rx', 'rg', 'gate')
IN_WIDTHS = (Q_WIDTH, KV_WIDTH, KV_WIDTH, 3 * HY_WIDTH, LRU_WIDTH, LRU_WIDTH, N_BRANCH * D_MODEL)
IN_TOTAL = Q_WIDTH + 2 * KV_WIDTH + 3 * HY_WIDTH + 2 * LRU_WIDTH + N_BRANCH * D_MODEL

kernel_name = 'hybrid_gqa_hyena_rglru_ecmoe_dit'


def in_columns():
    cols, start = {}, 0
    for name, width in zip(IN_NAMES, IN_WIDTHS):
        cols[name] = (start, start + width)
        start += width
    return cols


def rms_norm(x, g):
    xf = x.astype(jnp.float32)
    y = xf * lax.rsqrt(jnp.mean(xf * xf, axis=-1, keepdims=True) + NORM_EPS)
    return (y * g.astype(jnp.float32)).astype(x.dtype)


def modulate(x, g, shift, scale):
    return rms_norm(x, g) * (1 + scale) + shift


def short_conv(u, w, b, left, right):
    n = u.shape[1]
    up = jnp.pad(u, ((0, 0), (left, right), (0, 0)))
    out = b
    for j in range(w.shape[0]):
        out = out + up[:, j:j + n] * w[j]
    return out


def axial_rope(rows):
    row = jnp.repeat(jnp.arange(rows, dtype=jnp.float32), GRID_W)
    col = jnp.tile(jnp.arange(GRID_W, dtype=jnp.float32), rows)
    inv = jnp.power(ROPE_THETA, -jnp.arange(ROPE_PAIRS_AXIS, dtype=jnp.float32) / ROPE_PAIRS_AXIS)
    ang = jnp.concatenate([row[:, None] * inv, col[:, None] * inv], axis=-1)
    return jnp.cos(ang), jnp.sin(ang)


def apply_rope(x, cos, sin):
    xf = x.astype(jnp.float32)
    x1, x2 = xf[..., :ROPE_HALF], xf[..., ROPE_HALF:]
    cs = cos[None, :, None, :]
    sn = sin[None, :, None, :]
    return jnp.concatenate([x1 * cs - x2 * sn, x2 * cs + x1 * sn], axis=-1).astype(x.dtype)


def attend(qg, keys, vals):
    s = jnp.einsum('bqhgd,bkhd->bhgqk', qg, keys, preferred_element_type=jnp.float32) * ATTN_SCALE
    p = jax.nn.softmax(s, axis=-1).astype(vals.dtype)
    return jnp.einsum('bhgqk,bkhd->bqhgd', p, vals)


def latent_attention(q, k, v, k_ctx, v_ctx):
    bsz, s = q.shape[0], q.shape[1]
    keys = jnp.concatenate([k, k_ctx], axis=1)
    vals = jnp.concatenate([v, v_ctx], axis=1)
    n_blk = s // Q_BLOCK
    qb = q.reshape(bsz, n_blk, Q_BLOCK, N_KV_HEADS, GROUP, HEAD_DIM).swapaxes(0, 1)
    out = lax.map(lambda qblk: attend(qblk, keys, vals), qb)
    return out.swapaxes(0, 1).reshape(bsz, s, Q_WIDTH)


def hyena_filters(n, p):
    f32 = jnp.float32
    t = jnp.linspace(0.0, 1.0, n, dtype=f32)[:, None]
    w = (2.0 * math.pi / n) * jnp.arange(n, dtype=f32)[:, None]
    f = jnp.linspace(1e-4, HY_BANDS - 1, HY_BANDS, dtype=f32)[None, :]
    feats = jnp.concatenate([t, jnp.cos(f * w), -jnp.sin(f * w)], axis=-1)
    freq = p['hy_freq'].astype(f32)
    hid = jnp.sin(freq[0] * (feats @ p['hy_fw1'].astype(f32) + p['hy_fb1'].astype(f32)))
    hid = jnp.sin(freq[1] * (hid @ p['hy_fw2'].astype(f32) + p['hy_fb2'].astype(f32)))
    filt = hid @ p['hy_fw3'].astype(f32)
    max_decay = math.log(HY_DECAY_TARGET) / HY_FAST_DECAY_PCT
    min_decay = math.log(HY_DECAY_TARGET) / HY_SLOW_DECAY_PCT
    deltas = jnp.linspace(min_decay, max_decay, HY_WIDTH, dtype=f32)
    decay = jnp.exp(-t * jnp.abs(deltas)[None, :])
    h_fwd = filt[:, :HY_WIDTH] * decay
    h_bwd = filt[:, HY_WIDTH:] * decay
    return jnp.concatenate([h_fwd[:1] + h_bwd[:1], h_fwd[1:], jnp.zeros((1, HY_WIDTH), f32), h_bwd[:0:-1]], axis=0)


def bidir_long_conv(u, k2, bias):
    n = u.shape[1]
    uf = jnp.fft.rfft(u.astype(jnp.float32), n=2 * n, axis=1)
    kf = jnp.fft.rfft(k2, n=2 * n, axis=0)
    y = jnp.fft.irfft(uf * kf[None], n=2 * n, axis=1)[:, :n]
    return (y + u.astype(jnp.float32) * bias.astype(jnp.float32)).astype(u.dtype)


def hyena_branch(hy, p):
    n = hy.shape[1]
    u = short_conv(hy, p['hy_conv_w'], p['hy_conv_b'], 1, 1)
    x0, x1, v = jnp.split(u, 3, axis=-1)
    k2 = hyena_filters(n, p)
    return x0 * bidir_long_conv(v * x1, k2, p['hy_bias'])


def block_diag(x, w):
    bsz, n, _ = x.shape
    xb = x.reshape(bsz, n, LRU_BLOCKS, LRU_BLOCK)
    return jnp.einsum('blnk,nkj->blnj', xb, w).reshape(bsz, n, LRU_WIDTH)


def rglru_coeffs(xc, wa, ba, wx, bx, lam):
    f32 = jnp.float32
    r = jax.nn.sigmoid((block_diag(xc, wa) + ba).astype(f32))
    i = jax.nn.sigmoid((block_diag(xc, wx) + bx).astype(f32))
    log_a = -LRU_C * r * jax.nn.softplus(-lam.astype(f32))
    a = jnp.exp(log_a)
    b = jnp.sqrt(-jnp.expm1(2.0 * log_a)) * i * xc.astype(f32)
    return a, b


def linear_scan(a, b, reverse):
    def combine(lhs, rhs):
        return (lhs[0] * rhs[0], rhs[0] * lhs[1] + rhs[1])
    _, h = lax.associative_scan(combine, (a, b), reverse=reverse, axis=1)
    return h


def lru_scan_dir(xc, p, d, h0):
    a, b = rglru_coeffs(xc, p['lru_wa'][d], p['lru_ba'][d], p['lru_wx'][d], p['lru_bx'][d], p['lru_lambda'][d])
    if h0 is not None:
        t0 = 0 if d == 0 else -1
        b = b.at[:, t0].add(a[:, t0] * h0)
    return linear_scan(a, b, reverse=(d == 1))


def merge_branches(gate_logits, attn_o, hy_o, lru_o, p):
    g = jax.nn.sigmoid(gate_logits.astype(jnp.float32)).astype(attn_o.dtype)
    ga, gh, gl = jnp.split(g, N_BRANCH, axis=-1)
    y = ga * (attn_o @ p['w_attn_out']) + gh * (hy_o @ p['w_hy_out']) + gl * (lru_o @ p['w_lru_out'])
    return y @ p['w_out']


def ec_moe(u, p):
    bsz, n, d = u.shape
    cap = max(1, EC_CAPACITY * n // N_EXPERTS)
    aff = jax.nn.softmax((u @ p['router_w']).astype(jnp.float32), axis=-1)
    g, idx = lax.top_k(jnp.swapaxes(aff, 1, 2), cap)
    xs = jax.vmap(lambda ub, ib: ub[ib])(u, idx)
    hid = jax.nn.silu(jnp.einsum('becd,edf->becf', xs, p['exp_w1'])) * jnp.einsum('becd,edf->becf', xs, p['exp_w3'])
    ys = jnp.einsum('becf,efd->becd', hid, p['exp_w2']) * g[..., None].astype(u.dtype)
    return jax.vmap(lambda yb, ib: jnp.zeros((n, d), yb.dtype).at[ib.reshape(-1)].add(yb.reshape(-1, d)))(ys, idx)


def setup_inputs(seed: int = 0) -> dict:
    key = jax.random.key(seed)
    ks = iter(jax.random.split(key, 40))
    f32 = jnp.float32

    def nrm(shape, scale):
        return jax.random.normal(next(ks), shape, f32) * scale

    def gain(shape):
        return 1.0 + 0.02 * jax.random.normal(next(ks), shape, f32)

    x = nrm((BATCH, SEQ, D_MODEL), 1.0)
    c = nrm((BATCH, D_MODEL), 1.0)
    ctx = nrm((BATCH, CTX_LEN, D_MODEL), 1.0)
    c_ctx = nrm((D_MODEL,), 1.0)
    mod_w = nrm((DEPTH, D_MODEL, 6 * D_MODEL), 0.5 * D_MODEL ** -0.5)
    mod_b = nrm((DEPTH, 6 * D_MODEL), 0.02)
    norm1_g = gain((DEPTH, D_MODEL))
    norm2_g = gain((DEPTH, D_MODEL))
    w_in = nrm((DEPTH, D_MODEL, IN_TOTAL), D_MODEL ** -0.5)
    q_norm_g = gain((DEPTH, HEAD_DIM))
    k_norm_g = gain((DEPTH, HEAD_DIM))
    hy_conv_w = nrm((DEPTH, HY_SHORT, 3 * HY_WIDTH), HY_SHORT ** -0.5)
    hy_conv_b = nrm((DEPTH, 3 * HY_WIDTH), 0.02)
    hy_fw1 = nrm((DEPTH, HY_EMB, HY_FILTER_HIDDEN), HY_EMB ** -0.5)
    hy_fb1 = nrm((DEPTH, HY_FILTER_HIDDEN), 0.02)
    hy_fw2 = nrm((DEPTH, HY_FILTER_HIDDEN, HY_FILTER_HIDDEN), HY_FILTER_HIDDEN ** -0.5)
    hy_fb2 = nrm((DEPTH, HY_FILTER_HIDDEN), 0.02)
    hy_fw3 = nrm((DEPTH, HY_FILTER_HIDDEN, 2 * HY_WIDTH), 0.02)
    hy_freq = gain((DEPTH, 2, HY_FILTER_HIDDEN))
    hy_bias = nrm((DEPTH, HY_WIDTH), 0.1)
    lru_conv_w = nrm((DEPTH, LRU_CONV, LRU_WIDTH), LRU_CONV ** -0.5)
    lru_conv_b = nrm((DEPTH, LRU_WIDTH), 0.02)
    lru_wa = nrm((DEPTH, 2, LRU_BLOCKS, LRU_BLOCK, LRU_BLOCK), LRU_BLOCK ** -0.5)
    lru_ba = nrm((DEPTH, 2, LRU_WIDTH), 0.02)
    lru_wx = nrm((DEPTH, 2, LRU_BLOCKS, LRU_BLOCK, LRU_BLOCK), LRU_BLOCK ** -0.5)
    lru_bx = nrm((DEPTH, 2, LRU_WIDTH), 0.02)
    a_c = jax.random.uniform(next(ks), (DEPTH, 2, LRU_WIDTH), f32, 0.9, 0.999)
    a_base = a_c ** (1.0 / LRU_C)
    lru_lambda = jnp.log(a_base) - jnp.log1p(-a_base)
    w_attn_out = nrm((DEPTH, Q_WIDTH, D_MODEL), Q_WIDTH ** -0.5)
    w_hy_out = nrm((DEPTH, HY_WIDTH, D_MODEL), HY_WIDTH ** -0.5)
    w_lru_out = nrm((DEPTH, LRU_WIDTH, D_MODEL), LRU_WIDTH ** -0.5)
    w_out = nrm((DEPTH, D_MODEL, D_MODEL), D_MODEL ** -0.5)
    router_w = nrm((DEPTH, D_MODEL, N_EXPERTS), D_MODEL ** -0.5)
    exp_w1 = nrm((DEPTH, N_EXPERTS, D_MODEL, EXPERT_FF), D_MODEL ** -0.5)
    exp_w3 = nrm((DEPTH, N_EXPERTS, D_MODEL, EXPERT_FF), D_MODEL ** -0.5)
    exp_w2 = nrm((DEPTH, N_EXPERTS, EXPERT_FF, D_MODEL), EXPERT_FF ** -0.5)
    final_norm_g = gain((D_MODEL,))
    return {'x': x, 'c': c, 'ctx': ctx, 'c_ctx': c_ctx, 'mod_w': mod_w, 'mod_b': mod_b,
            'norm1_g': norm1_g, 'norm2_g': norm2_g, 'w_in': w_in, 'q_norm_g': q_norm_g, 'k_norm_g': k_norm_g,
            'hy_conv_w': hy_conv_w, 'hy_conv_b': hy_conv_b, 'hy_fw1': hy_fw1, 'hy_fb1': hy_fb1,
            'hy_fw2': hy_fw2, 'hy_fb2': hy_fb2, 'hy_fw3': hy_fw3, 'hy_freq': hy_freq, 'hy_bias': hy_bias,
            'lru_conv_w': lru_conv_w, 'lru_conv_b': lru_conv_b, 'lru_wa': lru_wa, 'lru_ba': lru_ba,
            'lru_wx': lru_wx, 'lru_bx': lru_bx, 'lru_lambda': lru_lambda,
            'w_attn_out': w_attn_out, 'w_hy_out': w_hy_out, 'w_lru_out': w_lru_out, 'w_out': w_out,
            'router_w': router_w, 'exp_w1': exp_w1, 'exp_w3': exp_w3, 'exp_w2': exp_w2,
            'final_norm_g': final_norm_g}


def reference(x, c, ctx, c_ctx, mod_w, mod_b, norm1_g, norm2_g, w_in, q_norm_g, k_norm_g,
              hy_conv_w, hy_conv_b, hy_fw1, hy_fb1, hy_fw2, hy_fb2, hy_fw3, hy_freq, hy_bias,
              lru_conv_w, lru_conv_b, lru_wa, lru_ba, lru_wx, lru_bx, lru_lambda,
              w_attn_out, w_hy_out, w_lru_out, w_out, router_w, exp_w1, exp_w3, exp_w2, final_norm_g):
    bsz, n_lat, _ = x.shape
    n_ctx = ctx.shape[1]
    ROWS = n_lat // GRID_W
    cos, sin = axial_rope(ROWS)
    cols = in_columns()
    silu_c = jax.nn.silu(c)
    silu_cc = jax.nn.silu(c_ctx)
    h, hc = x, ctx
    for l in range(DEPTH):
        last = l == DEPTH - 1
        p = {'hy_conv_w': hy_conv_w[l], 'hy_conv_b': hy_conv_b[l], 'hy_fw1': hy_fw1[l], 'hy_fb1': hy_fb1[l],
             'hy_fw2': hy_fw2[l], 'hy_fb2': hy_fb2[l], 'hy_fw3': hy_fw3[l], 'hy_freq': hy_freq[l],
             'hy_bias': hy_bias[l], 'lru_wa': lru_wa[l], 'lru_ba': lru_ba[l], 'lru_wx': lru_wx[l],
             'lru_bx': lru_bx[l], 'lru_lambda': lru_lambda[l], 'w_attn_out': w_attn_out[l],
             'w_hy_out': w_hy_out[l], 'w_lru_out': w_lru_out[l], 'w_out': w_out[l], 'router_w': router_w[l],
             'exp_w1': exp_w1[l], 'exp_w3': exp_w3[l], 'exp_w2': exp_w2[l]}
        mod = (silu_c @ mod_w[l] + mod_b[l])[:, None, :]
        sh1, sc1, g1, sh2, sc2, g2 = jnp.split(mod, 6, axis=-1)
        mod_c = silu_cc @ mod_w[l] + mod_b[l]
        csh1, csc1, cg1, csh2, csc2, cg2 = jnp.split(mod_c, 6, axis=-1)

        u = modulate(h, norm1_g[l], sh1, sc1)
        uc = modulate(hc, norm1_g[l], csh1, csc1)
        z = u @ w_in[l]
        zs = {name: z[..., a:b] for name, (a, b) in cols.items()}
        if last:
            w_l = w_in[l]
            zcs = {name: uc @ w_l[:, a:b] for name, (a, b) in cols.items() if name in ('k', 'v', 'rx')}
        else:
            zc = uc @ w_in[l]
            zcs = {name: zc[..., a:b] for name, (a, b) in cols.items()}

        k_c = rms_norm(zcs['k'].reshape(bsz, n_ctx, N_KV_HEADS, HEAD_DIM), k_norm_g[l])
        v_c = zcs['v'].reshape(bsz, n_ctx, N_KV_HEADS, HEAD_DIM)
        xc_c = short_conv(zcs['rx'], lru_conv_w[l], lru_conv_b[l], LRU_PAD_LEFT, LRU_PAD_RIGHT)
        hc_f = lru_scan_dir(xc_c, p, 0, None)
        hc_b = lru_scan_dir(xc_c, p, 1, None)

        q = apply_rope(rms_norm(zs['q'].reshape(bsz, n_lat, N_HEADS, HEAD_DIM), q_norm_g[l]), cos, sin)
        k = apply_rope(rms_norm(zs['k'].reshape(bsz, n_lat, N_KV_HEADS, HEAD_DIM), k_norm_g[l]), cos, sin)
        v = zs['v'].reshape(bsz, n_lat, N_KV_HEADS, HEAD_DIM)
        attn = latent_attention(q, k, v, k_c, v_c)
        hy = hyena_branch(zs['hy'], p)
        xc = short_conv(zs['rx'], lru_conv_w[l], lru_conv_b[l], LRU_PAD_LEFT, LRU_PAD_RIGHT)
        h_f = lru_scan_dir(xc, p, 0, hc_f[:, -1])
        h_b = lru_scan_dir(xc, p, 1, hc_b[:, 0])
        lru = (h_f + h_b).astype(h.dtype) * jax.nn.gelu(zs['rg'])
        h = h + g1 * merge_branches(zs['gate'], attn, hy, lru, p)

        if not last:
            qc = rms_norm(zcs['q'].reshape(bsz, n_ctx, N_HEADS, HEAD_DIM), q_norm_g[l])
            attn_c = attend(qc.reshape(bsz, n_ctx, N_KV_HEADS, GROUP, HEAD_DIM), k_c, v_c).reshape(bsz, n_ctx, Q_WIDTH)
            hy_c = hyena_branch(zcs['hy'], p)
            lru_c = (hc_f + hc_b).astype(hc.dtype) * jax.nn.gelu(zcs['rg'])
            hc = hc + cg1 * merge_branches(zcs['gate'], attn_c, hy_c, lru_c, p)

        u2 = modulate(h, norm2_g[l], sh2, sc2)
        h = h + g2 * ec_moe(u2, p)
        if not last:
            uc2 = modulate(hc, norm2_g[l], csh2, csc2)
            hc = hc + cg2 * ec_moe(uc2, p)
    return rms_norm(h, final_norm_g)
```

```python
import functools
import math

import jax
import jax.numpy as jnp
from jax import lax
from jax.experimental import pallas as pl
from jax.experimental.pallas import tpu as pltpu

F32 = jnp.float32
BF16 = jnp.bfloat16
I32 = jnp.int32

D_MODEL = 1024
GRID_W = 64
NORM_EPS = 1e-6
N_HEADS = 8
N_KV_HEADS = 2
HEAD_DIM = 128
GROUP = N_HEADS // N_KV_HEADS
ROPE_PAIRS_AXIS = HEAD_DIM // 4
ROPE_THETA = 10000.0
ATTN_SCALE = HEAD_DIM ** -0.5
Q_WIDTH = N_HEADS * HEAD_DIM
KV_WIDTH = N_KV_HEADS * HEAD_DIM
HY_WIDTH = D_MODEL // 2
HY_BANDS = 16
HY_EMB = 1 + 2 * HY_BANDS
HY_FILTER_HIDDEN = 64
HY_FAST_DECAY_PCT = 0.3
HY_SLOW_DECAY_PCT = 1.5
HY_DECAY_TARGET = 1e-2
LRU_WIDTH = D_MODEL // 2
LRU_BLOCKS = 8
LRU_BLOCK = LRU_WIDTH // LRU_BLOCKS
LRU_C = 8.0
N_BRANCH = 3
N_EXPERTS = 16
EC_CAPACITY = 2
GATE_WIDTH = N_BRANCH * D_MODEL
OFF_Q = 0
OFF_K = OFF_Q + Q_WIDTH
OFF_V = OFF_K + KV_WIDTH
OFF_HY = OFF_V + KV_WIDTH
OFF_RX = OFF_HY + 3 * HY_WIDTH
OFF_RG = OFF_RX + LRU_WIDTH
OFF_GATE = OFF_RG + LRU_WIDTH
IN_TOTAL = OFF_GATE + GATE_WIDTH

LANES = 128
SUBLANES = 8
V7X_VMEM_LIMIT_BYTES = 56 * 1024 * 1024
MOD_ROWS = 24


def _params(n_axes, vmem=V7X_VMEM_LIMIT_BYTES):
    return pltpu.CompilerParams(dimension_semantics=("arbitrary",) * n_axes, vmem_limit_bytes=vmem)


def _dot(a, b):
    return jnp.dot(a, b, preferred_element_type=F32)


def _dot_nt(a, b):
    return lax.dot_general(a, b, (((1,), (1,)), ((), ())), preferred_element_type=F32)


def _split2(x):
    hi = x.astype(BF16)
    lo = (x - hi.astype(F32)).astype(BF16)
    return hi, lo


def _dot3(a, b):
    ah, al = _split2(a)
    bh, bl = _split2(b)
    return _dot(ah, bh) + (_dot(ah, bl) + _dot(al, bh))


def _rms(x):
    return x * lax.rsqrt(jnp.mean(x * x, axis=-1, keepdims=True) + NORM_EPS)


def _sigmoid(x):
    return 1.0 / (1.0 + jnp.exp(-x))


def _silu(x):
    return x * _sigmoid(x)


def _gelu_tanh(x):
    return 0.5 * x * (1.0 + jnp.tanh(math.sqrt(2.0 / math.pi) * (x + 0.044715 * (x * x * x))))


def _shift_down(x, k):
    row = lax.broadcasted_iota(I32, x.shape, 0)
    return jnp.where(row >= k, pltpu.roll(x, k, 0), 0.0)


def _shift_up(x, k):
    n = x.shape[0]
    row = lax.broadcasted_iota(I32, x.shape, 0)
    return jnp.where(row < n - k, pltpu.roll(x, n - k, 0), 0.0)


def _mod_kernel(c_ref, w_ref, b_ref, o_ref):
    o_ref[0] = _dot3(_silu(c_ref[...]), w_ref[0]) + b_ref[0]


def _mod_call(cvec, mod_w, mod_b):
    depth, d, six_d = mod_w.shape
    tn = 1536
    return pl.pallas_call(
        _mod_kernel,
        out_shape=jax.ShapeDtypeStruct((depth, MOD_ROWS, six_d), F32),
        grid=(depth, six_d // tn),
        in_specs=[
            pl.BlockSpec((MOD_ROWS, d), lambda l, j: (0, 0)),
            pl.BlockSpec((1, d, tn), lambda l, j: (l, 0, j)),
            pl.BlockSpec((1, 1, tn), lambda l, j: (l, 0, j)),
        ],
        out_specs=pl.BlockSpec((1, MOD_ROWS, tn), lambda l, j: (l, 0, j)),
        compiler_params=_params(2),
        name="mod",
    )(cvec, mod_w, mod_b.reshape(depth, 1, six_d))


def _inproj_kernel(h_ref, sh_ref, sc_ref, g_ref, w_ref, cs_ref, sn_ref, qg_ref, kg_ref,
                   q_ref, k_ref, v_ref, hy_ref, rx_ref, rg_ref, gate_ref):
    u = (_rms(h_ref[0]) * g_ref[...] * (1.0 + sc_ref[0]) + sh_ref[0]).astype(BF16)
    cs = cs_ref[...]
    sn = sn_ref[...]

    def normed_rope(z, g):
        r = _rms(z) * g
        return r * cs + pltpu.roll(r, HEAD_DIM // 2, 1) * sn

    zq = _dot(u, w_ref[:, OFF_Q:OFF_Q + Q_WIDTH])
    for hd in range(N_HEADS):
        sl = slice(hd * HEAD_DIM, (hd + 1) * HEAD_DIM)
        q_ref[0, :, sl] = (normed_rope(zq[:, sl], qg_ref[...]) * ATTN_SCALE).astype(BF16)
    zkv = _dot(u, w_ref[:, OFF_K:OFF_K + 2 * KV_WIDTH])
    for hd in range(N_KV_HEADS):
        sl = slice(hd * HEAD_DIM, (hd + 1) * HEAD_DIM)
        k_ref[0, :, sl] = normed_rope(zkv[:, sl], kg_ref[...]).astype(BF16)
    v_ref[0] = zkv[:, KV_WIDTH:].astype(BF16)
    for j in range(3):
        sl = slice(j * HY_WIDTH, (j + 1) * HY_WIDTH)
        hy_ref[0, :, sl] = _dot(u, w_ref[:, OFF_HY + j * HY_WIDTH:OFF_HY + (j + 1) * HY_WIDTH])
    rr = _dot(u, w_ref[:, OFF_RX:OFF_RX + 2 * LRU_WIDTH])
    rx_ref[0] = rr[:, :LRU_WIDTH]
    rg_ref[0] = rr[:, LRU_WIDTH:].astype(BF16)
    for j in range(N_BRANCH):
        sl = slice(j * D_MODEL, (j + 1) * D_MODEL)
        gate_ref[0, :, sl] = _dot(u, w_ref[:, OFF_GATE + j * D_MODEL:OFF_GATE + (j + 1) * D_MODEL]).astype(BF16)


def _inproj_call(h, sh, sc, g, w_bf, cs, sn, qg, kg, ctx_rows):
    bsz, t, d = h.shape
    tm = min(512, t)
    row = (lambda b: MOD_ROWS - 8) if ctx_rows else (lambda b: b)
    tok = lambda w: pl.BlockSpec((1, tm, w), lambda b, i: (b, i, 0))
    modspec = pl.BlockSpec((1, 1, d), lambda b, i: (row(b), 0, 0))
    const = lambda shape: pl.BlockSpec(shape, lambda b, i: (0,) * len(shape))
    outs = [(Q_WIDTH, BF16), (KV_WIDTH, BF16), (KV_WIDTH, BF16), (3 * HY_WIDTH, F32), (LRU_WIDTH, F32),
            (LRU_WIDTH, BF16), (GATE_WIDTH, BF16)]
    return pl.pallas_call(
        _inproj_kernel,
        out_shape=[jax.ShapeDtypeStruct((bsz, t, w), dt) for w, dt in outs],
        grid=(bsz, t // tm),
        in_specs=[
            tok(d), modspec, modspec, const((1, d)),
            pl.BlockSpec((d, IN_TOTAL), lambda b, i: (0, 0), pipeline_mode=pl.Buffered(1)),
            pl.BlockSpec((tm, HEAD_DIM), lambda b, i: (i, 0)),
            pl.BlockSpec((tm, HEAD_DIM), lambda b, i: (i, 0)),
            const((1, HEAD_DIM)), const((1, HEAD_DIM)),
        ],
        out_specs=[tok(w) for w, _ in outs],
        compiler_params=_params(2),
        name="inproj",
    )(h, sh, sc, g, w_bf, cs, sn, qg, kg)


def _attn_kernel(*refs, n_src):
    q_ref = refs[0]
    kv = refs[1:1 + 2 * n_src]
    o_ref = refs[1 + 2 * n_src]
    ks = [kv[2 * i][0] for i in range(n_src)]
    vs = [kv[2 * i + 1][0] for i in range(n_src)]
    for g in range(GROUP):
        sl = slice(g * HEAD_DIM, (g + 1) * HEAD_DIM)
        q = q_ref[0, :, sl]
        ss = [_dot_nt(q, k) for k in ks]
        m = ss[0].max(axis=-1, keepdims=True)
        for s in ss[1:]:
            m = jnp.maximum(m, s.max(axis=-1, keepdims=True))
        ps = [jnp.exp(s - m) for s in ss]
        den = ps[0].sum(axis=-1, keepdims=True)
        for p in ps[1:]:
            den = den + p.sum(axis=-1, keepdims=True)
        o = _dot(ps[0].astype(BF16), vs[0])
        for p, v in zip(ps[1:], vs[1:]):
            o = o + _dot(p.astype(BF16), v)
        o_ref[0, :, sl] = (o / den).astype(BF16)


def _attn_call(q, kvs):
    bsz, tq_all, _ = q.shape
    tq = min(256, tq_all)
    gw = GROUP * HEAD_DIM
    in_specs = [pl.BlockSpec((1, tq, gw), lambda b, h, i: (b, i, h))]
    args = [q]
    for k, v in kvs:
        tk = k.shape[1]
        spec = pl.BlockSpec((1, tk, HEAD_DIM), lambda b, h, i: (b, 0, h))
        in_specs += [spec, spec]
        args += [k, v]
    return pl.pallas_call(
        functools.partial(_attn_kernel, n_src=len(kvs)),
        out_shape=jax.ShapeDtypeStruct((bsz, tq_all, Q_WIDTH), BF16),
        grid=(bsz, N_KV_HEADS, tq_all // tq),
        in_specs=in_specs,
        out_specs=pl.BlockSpec((1, tq, gw), lambda b, h, i: (b, i, h)),
        compiler_params=_params(3),
        name="attn",
    )(*args)


def _hy_pre_kernel(h0_ref, h1_ref, h2_ref, w0_ref, w1_ref, w2_ref, b0_ref, b1_ref, b2_ref,
                   x0_ref, w_ref, wb_ref):
    def conv(x_ref, cw_ref, cb_ref):
        x = x_ref[0]
        cw = cw_ref[...]
        return cb_ref[...] + _shift_down(x, 1) * cw[0:1] + x * cw[1:2] + _shift_up(x, 1) * cw[2:3]

    x0_ref[0] = conv(h0_ref, w0_ref, b0_ref)
    w = conv(h2_ref, w2_ref, b2_ref) * conv(h1_ref, w1_ref, b1_ref)
    w_ref[0] = w
    wb_ref[0] = w.astype(BF16)


def _hy_pre_call(hy, conv_w, conv_b):
    bsz, t, _ = hy.shape
    nch = HY_WIDTH // LANES
    xs = lambda part: pl.BlockSpec((1, t, LANES), lambda b, j: (b, 0, part * nch + j))
    ws = lambda part: pl.BlockSpec((3, LANES), lambda b, j: (0, part * nch + j))
    bs = lambda part: pl.BlockSpec((1, LANES), lambda b, j: (0, part * nch + j))
    out = pl.BlockSpec((1, t, LANES), lambda b, j: (b, 0, j))
    return pl.pallas_call(
        _hy_pre_kernel,
        out_shape=[jax.ShapeDtypeStruct((bsz, t, HY_WIDTH), F32), jax.ShapeDtypeStruct((bsz, t, HY_WIDTH), F32),
                   jax.ShapeDtypeStruct((bsz, t, HY_WIDTH), BF16)],
        grid=(bsz, nch),
        in_specs=[xs(0), xs(1), xs(2), ws(0), ws(1), ws(2), bs(0), bs(1), bs(2)],
        out_specs=[out, out, out],
        compiler_params=_params(2),
        name="hy_pre",
    )(hy, hy, hy, conv_w, conv_w, conv_w, conv_b, conv_b, conv_b)


def _hy_filter_kernel(feat_ref, w1_ref, b1_ref, w2_ref, b2_ref, w3_ref, fr_ref, dl_ref, o_ref):
    feats = feat_ref[...]
    hid = jnp.sin(fr_ref[0:1] * (_dot3(feats, w1_ref[...]) + b1_ref[...]))
    hid = jnp.sin(fr_ref[1:2] * (_dot3(hid, w2_ref[...]) + b2_ref[...]))
    filt = _dot3(hid, w3_ref[...])
    decay = jnp.exp(-feats[:, 0:1] * dl_ref[...])
    h_fwd = filt[:, :HY_WIDTH] * decay
    h_bwd = filt[:, HY_WIDTH:] * decay
    o_ref[:, :HY_WIDTH] = h_fwd + h_bwd
    o_ref[:, HY_WIDTH:] = h_bwd - h_fwd


def _hy_filter_call(feats, fw1, fb1, fw2, fb2, fw3, freq, deltas_abs):
    n = feats.shape[0]
    tn = min(512, n)
    const = lambda shape: pl.BlockSpec(shape, lambda i: (0,) * len(shape))
    return pl.pallas_call(
        _hy_filter_kernel,
        out_shape=jax.ShapeDtypeStruct((n, 2 * HY_WIDTH), F32),
        grid=(n // tn,),
        in_specs=[pl.BlockSpec((tn, LANES), lambda i: (i, 0)), const((LANES, LANES)), const((1, LANES)),
                  const((LANES, LANES)), const((1, LANES)), const((LANES, 2 * HY_WIDTH)), const((2, LANES)),
                  const((1, HY_WIDTH))],
        out_specs=pl.BlockSpec((tn, 2 * HY_WIDTH), lambda i: (i, 0)),
        compiler_params=_params(1),
        name="hy_filter",
    )(feats, fw1, fb1, fw2, fb2, fw3, freq, deltas_abs)


def _hy_kspec_kernel(c_ref, s_ref, h_ref, kre_ref, kim_ref, *, scale):
    hs_hi, hs_lo = _split2(h_ref[:, :HY_WIDTH])
    hd_hi, hd_lo = _split2(h_ref[:, HY_WIDTH:])
    c = c_ref[...]
    s = s_ref[...]
    kre_ref[...] = (_dot(c, hs_hi) + _dot(c, hs_lo)) * scale
    kim_ref[...] = (_dot(s, hd_hi) + _dot(s, hd_lo)) * scale


def _hy_kspec_call(ctab, stab, hsd):
    n = ctab.shape[0]
    tf = min(512, n)
    out = pl.BlockSpec((tf, HY_WIDTH), lambda i: (i, 0))
    return pl.pallas_call(
        functools.partial(_hy_kspec_kernel, scale=1.0 / n),
        out_shape=[jax.ShapeDtypeStruct((n, HY_WIDTH), F32)] * 2,
        grid=(n // tf,),
        in_specs=[pl.BlockSpec((tf, n), lambda i: (i, 0)), pl.BlockSpec((tf, n), lambda i: (i, 0)),
                  pl.BlockSpec((n, 2 * HY_WIDTH), lambda i: (0, 0))],
        out_specs=[out, out],
        compiler_params=_params(1),
        name="hy_kspec",
    )(ctab, stab, hsd)


def _hy_fwd_kernel(c_ref, s_ref, w_ref, kre_ref, kim_ref, yre_ref, yim_ref):
    w = w_ref[0]
    a = _dot(c_ref[...], w)
    b = _dot(s_ref[...], w)
    kre = kre_ref[...]
    kim = kim_ref[...]
    yre_ref[0] = (a * kre + b * kim).astype(BF16)
    yim_ref[0] = (a * kim - b * kre).astype(BF16)


def _hy_fwd_call(ctab, stab, wb, kre, kim):
    bsz, n, _ = wb.shape
    tf = min(512, n)
    tab = pl.BlockSpec((tf, n), lambda i, b: (i, 0))
    kk = pl.BlockSpec((tf, HY_WIDTH), lambda i, b: (i, 0))
    out = pl.BlockSpec((1, tf, HY_WIDTH), lambda i, b: (b, i, 0))
    return pl.pallas_call(
        _hy_fwd_kernel,
        out_shape=[jax.ShapeDtypeStruct((bsz, n, HY_WIDTH), BF16)] * 2,
        grid=(n // tf, bsz),
        in_specs=[tab, tab, pl.BlockSpec((1, n, HY_WIDTH), lambda i, b: (b, 0, 0)), kk, kk],
        out_specs=[out, out],
        compiler_params=_params(2),
        name="hy_fwd",
    )(ctab, stab, wb, kre, kim)


def _hy_inv_kernel(ct_ref, st_ref, yre_ref, yim_ref, x0_ref, w_ref, bias_ref, o_ref):
    y = _dot(ct_ref[...], yre_ref[0]) + _dot(st_ref[...], yim_ref[0])
    o_ref[0] = (x0_ref[0] * (y + w_ref[0] * bias_ref[...])).astype(BF16)


def _hy_inv_call(cttab, stntab, yre, yim, x0, w, bias):
    bsz, n, _ = yre.shape
    tt = min(512, n)
    tab = pl.BlockSpec((tt, n), lambda i, b: (i, 0))
    full = pl.BlockSpec((1, n, HY_WIDTH), lambda i, b: (b, 0, 0))
    tile = pl.BlockSpec((1, tt, HY_WIDTH), lambda i, b: (b, i, 0))
    return pl.pallas_call(
        _hy_inv_kernel,
        out_shape=jax.ShapeDtypeStruct((bsz, n, HY_WIDTH), BF16),
        grid=(n // tt, bsz),
        in_specs=[tab, tab, full, full, tile, tile, pl.BlockSpec((1, HY_WIDTH), lambda i, b: (0, 0))],
        out_specs=tile,
        compiler_params=_params(2),
        name="hy_inv",
    )(cttab, stntab, yre, yim, x0, w, bias)


def _lru_kernel(rxl_ref, rxc_ref, rgl_ref, rgc_ref, cw_ref, cb_ref, wa_ref, wx_ref, ba_ref, bx_ref, lam_ref,
                ol_ref, oc_ref, a_scr, b_scr):
    cw = cw_ref[...]
    cb = cb_ref[...]
    row8 = lax.broadcasted_iota(I32, (SUBLANES, LANES), 0)

    def coeffs(x_ref, n):
        x = x_ref[0]
        xc = (cb + _shift_down(x, 2) * cw[0:1] + _shift_down(x, 1) * cw[1:2] + x * cw[2:3]
              + _shift_up(x, 1) * cw[3:4])
        xb = xc.astype(BF16)
        for d in range(2):
            r = _sigmoid(_dot(xb, wa_ref[d, 0]) + ba_ref[d])
            i = _sigmoid(_dot(xb, wx_ref[d, 0]) + bx_ref[d])
            nl = -lam_ref[d]
            softplus = jnp.maximum(nl, 0.0) + jnp.log(1.0 + jnp.exp(-jnp.abs(nl)))
            log_a = (-LRU_C) * r * softplus
            a_scr[d, 0:n, :] = jnp.exp(log_a)
            b_scr[d, 0:n, :] = jnp.sqrt(1.0 - jnp.exp(2.0 * log_a)) * i * xc

    def scan(n, carry_f, carry_b):
        ng = n // SUBLANES

        def body(g, carry):
            cf, cbk = carry
            of = pl.multiple_of(g * SUBLANES, SUBLANES)
            ob = pl.multiple_of((ng - 1 - g) * SUBLANES, SUBLANES)
            a = a_scr[0, pl.ds(of, SUBLANES), :]
            b = b_scr[0, pl.ds(of, SUBLANES), :]
            a2 = a_scr[1, pl.ds(ob, SUBLANES), :]
            b2 = b_scr[1, pl.ds(ob, SUBLANES), :]
            for k in (1, 2, 4):
                keep = row8 >= k
                b = a * jnp.where(keep, pltpu.roll(b, k, 0), 0.0) + b
                a = a * jnp.where(keep, pltpu.roll(a, k, 0), 1.0)
                keep2 = row8 < SUBLANES - k
                b2 = a2 * jnp.where(keep2, pltpu.roll(b2, SUBLANES - k, 0), 0.0) + b2
                a2 = a2 * jnp.where(keep2, pltpu.roll(a2, SUBLANES - k, 0), 1.0)
            hf = a * cf + b
            hb = a2 * cbk + b2
            b_scr[0, pl.ds(of, SUBLANES), :] = hf
            b_scr[1, pl.ds(ob, SUBLANES), :] = hb
            return hf[SUBLANES - 1:SUBLANES, :], hb[0:1, :]

        return lax.fori_loop(0, ng, body, (carry_f, carry_b))

    n_ctx = rxc_ref.shape[1]
    n_lat = rxl_ref.shape[1]
    zero = jnp.zeros((1, LANES), F32)
    coeffs(rxc_ref, n_ctx)
    cf, cbk = scan(n_ctx, zero, zero)
    oc_ref[0] = ((b_scr[0, 0:n_ctx, :] + b_scr[1, 0:n_ctx, :]) * _gelu_tanh(rgc_ref[0].astype(F32))).astype(BF16)
    coeffs(rxl_ref, n_lat)
    scan(n_lat, cf, cbk)
    ol_ref[0] = ((b_scr[0] + b_scr[1]) * _gelu_tanh(rgl_ref[0].astype(F32))).astype(BF16)


def _lru_call(rx, rx_c, rg, rg_c, conv_w, conv_b, wa_bd, wx_bd, ba, bx, lam):
    bsz, t, _ = rx.shape
    tc = rx_c.shape[1]
    nch = LRU_WIDTH // LANES
    seq = lambda n: pl.BlockSpec((1, n, LANES), lambda b, j: (b, 0, j))
    vec = lambda r: pl.BlockSpec((r, 1, LANES), lambda b, j: (0, 0, j))
    wsp = pl.BlockSpec((2, 1, LANES, LANES), lambda b, j: (0, j, 0, 0))
    return pl.pallas_call(
        _lru_kernel,
        out_shape=[jax.ShapeDtypeStruct((bsz, t, LRU_WIDTH), BF16), jax.ShapeDtypeStruct((bsz, tc, LRU_WIDTH), BF16)],
        grid=(bsz, nch),
        in_specs=[seq(t), seq(tc), seq(t), seq(tc),
                  pl.BlockSpec((4, LANES), lambda b, j: (0, j)), pl.BlockSpec((1, LANES), lambda b, j: (0, j)),
                  wsp, wsp, vec(2), vec(2), vec(2)],
        out_specs=[seq(t), seq(tc)],
        scratch_shapes=[pltpu.VMEM((2, t, LANES), F32), pltpu.VMEM((2, t, LANES), F32)],
        compiler_params=_params(2),
        name="lru",
    )(rx, rx_c, rg, rg_c, conv_w, conv_b, wa_bd, wx_bd, ba, bx, lam)


def _merge_kernel(attn_ref, hy_ref, lru_ref, gate_ref, h_ref, g1_ref, wa_ref, wh_ref, wl_ref, wo_ref,
                  n2_ref, sh2_ref, sc2_ref, rw_ref, hn_ref, u2_ref, lt_ref):
    def gate(j):
        return _sigmoid(gate_ref[0, :, j * D_MODEL:(j + 1) * D_MODEL].astype(F32))

    y = gate(0) * _dot(attn_ref[0], wa_ref[...])
    y = y + gate(1) * _dot(hy_ref[0], wh_ref[...])
    y = y + gate(2) * _dot(lru_ref[0], wl_ref[...])
    hn = h_ref[0] + g1_ref[0] * _dot(y.astype(BF16), wo_ref[...])
    hn_ref[0] = hn
    u2 = (_rms(hn) * n2_ref[...] * (1.0 + sc2_ref[0]) + sh2_ref[0]).astype(BF16)
    u2_ref[0] = u2
    lt_ref[0] = _dot_nt(rw_ref[...], u2)


def _merge_call(attn, hyo, lruo, gate, h, g1, wa, wh, wl, wo, n2g, sh2, sc2, rw_t, ctx_rows):
    bsz, t, d = h.shape
    tm = min(512, t)
    row = (lambda b: MOD_ROWS - 8) if ctx_rows else (lambda b: b)
    tok = lambda w: pl.BlockSpec((1, tm, w), lambda b, i: (b, i, 0))
    modspec = pl.BlockSpec((1, 1, d), lambda b, i: (row(b), 0, 0))
    const = lambda shape: pl.BlockSpec(shape, lambda b, i: (0,) * len(shape))
    return pl.pallas_call(
        _merge_kernel,
        out_shape=[jax.ShapeDtypeStruct((bsz, t, d), F32), jax.ShapeDtypeStruct((bsz, t, d), BF16),
                   jax.ShapeDtypeStruct((bsz, N_EXPERTS, t), F32)],
        grid=(bsz, t // tm),
        in_specs=[tok(Q_WIDTH), tok(HY_WIDTH), tok(LRU_WIDTH), tok(GATE_WIDTH), tok(d), modspec,
                  const((Q_WIDTH, d)), const((HY_WIDTH, d)), const((LRU_WIDTH, d)), const((d, d)),
                  const((1, d)), modspec, modspec, const((N_EXPERTS, d))],
        out_specs=[tok(d), tok(d), pl.BlockSpec((1, N_EXPERTS, tm), lambda b, i: (b, 0, i))],
        compiler_params=_params(2),
        name="merge",
    )(attn, hyo, lruo, gate, h, g1, wa, wh, wl, wo, n2g, sh2, sc2, rw_t)


def _router_kernel(lt_ref, slot_ref, slotc_ref, gc_ref, *, cap):
    lg = lt_ref[0]
    n_e, t = lg.shape
    ex = jnp.exp(lg - lg.max(axis=0, keepdims=True))
    aff = ex / ex.sum(axis=0, keepdims=True)
    key = pltpu.bitcast(aff, I32)
    capf = float(cap)

    def count(mask):
        return jnp.where(mask, 1.0, 0.0).sum(axis=1, keepdims=True)

    def vbody(i, thr):
        cand = thr | lax.shift_left(jnp.int32(1), 29 - i)
        return jnp.where(count(key >= cand) >= capf, cand, thr)

    thr = lax.fori_loop(0, 30, vbody, jnp.zeros((n_e, 1), I32))
    gt = key > thr
    eq = key == thr
    need = capf - count(gt)
    idx = lax.broadcasted_iota(I32, (n_e, t), 1)
    nbits = t.bit_length() - 1

    def ibody(i, lo):
        cand = lo | lax.shift_left(jnp.int32(1), nbits - 1 - i)
        return jnp.where(count(eq & (idx < cand)) < need, cand, lo)

    last = lax.fori_loop(0, nbits, ibody, jnp.zeros((n_e, 1), I32))
    sel = gt | (eq & (idx <= last))
    self32 = jnp.where(sel, 1.0, 0.0)
    gsel = jnp.where(sel, aff, 0.0)

    r_i = lax.broadcasted_iota(I32, (LANES, LANES), 0)
    c_i = lax.broadcasted_iota(I32, (LANES, LANES), 1)
    tri = jnp.where(r_i <= c_i, 1.0, 0.0).astype(BF16)
    eye = jnp.where(r_i == c_i, 1.0, 0.0).astype(BF16)
    off = jnp.zeros((n_e, 1), F32)
    for c in range(t // LANES):
        sl = slice(c * LANES, (c + 1) * LANES)
        xs = self32[:, sl]
        inc = _dot(xs.astype(BF16), tri)
        slot1 = jnp.where(sel[:, sl], inc - xs + off + 1.0, 0.0)
        off = off + inc[:, LANES - 1:LANES]
        slot_ref[0, :, sl] = slot1.astype(I32) - 1
        hi = jnp.floor(slot1 * (1.0 / 16.0))
        lo = slot1 - 16.0 * hi
        col = 16.0 * _dot_nt(eye, hi.astype(BF16)) + _dot_nt(eye, lo.astype(BF16))
        slotc_ref[0, sl, :] = col.astype(I32) - 1
        g = gsel[:, sl]
        g1 = g.astype(BF16)
        r1 = g - g1.astype(F32)
        g2 = r1.astype(BF16)
        g3 = (r1 - g2.astype(F32)).astype(BF16)
        gc_ref[0, sl, :] = (_dot_nt(eye, g1) + _dot_nt(eye, g2)) + _dot_nt(eye, g3)


def _router_call(logits_t, cap):
    bsz, n_e, t = logits_t.shape
    return pl.pallas_call(
        functools.partial(_router_kernel, cap=cap),
        out_shape=[jax.ShapeDtypeStruct((bsz, n_e, t), I32), jax.ShapeDtypeStruct((bsz, t, n_e), I32),
                   jax.ShapeDtypeStruct((bsz, t, n_e), F32)],
        grid=(bsz,),
        in_specs=[pl.BlockSpec((1, n_e, t), lambda b: (b, 0, 0))],
        out_specs=[pl.BlockSpec((1, n_e, t), lambda b: (b, 0, 0)), pl.BlockSpec((1, t, n_e), lambda b: (b, 0, 0)),
                   pl.BlockSpec((1, t, n_e), lambda b: (b, 0, 0))],
        compiler_params=_params(1),
        name="router",
    )(logits_t)


def _expert_kernel(u_ref, slot_ref, w1_ref, w3_ref, w2_ref, y_ref, *, cap, tk):
    t = u_ref.shape[1]
    rid = lax.broadcasted_iota(I32, (cap, tk), 0)
    xg = jnp.zeros((cap, D_MODEL), F32)
    for c in range(t // tk):
        sl = slice(c * tk, (c + 1) * tk)
        onehot = jnp.where(slot_ref[0, 0, :, sl] == rid, 1.0, 0.0).astype(BF16)
        xg = xg + _dot(onehot, u_ref[0, sl, :])
    xb = xg.astype(BF16)
    hid = (_silu(_dot(xb, w1_ref[0])) * _dot(xb, w3_ref[0])).astype(BF16)
    y_ref[0, 0] = _dot(hid, w2_ref[0]).astype(BF16)


def _expert_call(u2, slot_row, w1, w3, w2, cap):
    bsz, t, d = u2.shape
    n_e = w1.shape[0]
    tk = min(512, t)
    wspec = pl.BlockSpec((1, d, d), lambda b, e: (e, 0, 0))
    return pl.pallas_call(
        functools.partial(_expert_kernel, cap=cap, tk=tk),
        out_shape=jax.ShapeDtypeStruct((bsz, n_e, cap, d), BF16),
        grid=(bsz, n_e),
        in_specs=[pl.BlockSpec((1, t, d), lambda b, e: (b, 0, 0)),
                  pl.BlockSpec((1, 1, 1, t), lambda b, e: (b, e, 0, 0)), wspec, wspec, wspec],
        out_specs=pl.BlockSpec((1, 1, cap, d), lambda b, e: (b, e, 0, 0)),
        compiler_params=_params(2),
        name="expert",
    )(u2, slot_row.reshape(bsz, n_e, 1, t), w1, w3, w2)


def _scatter_kernel(h_ref, y_ref, slotc_ref, gc_ref, g2_ref, o_ref, *, cap):
    tk = h_ref.shape[1]
    lane = lax.broadcasted_iota(I32, (tk, cap), 1)
    slots = slotc_ref[0]
    gates = gc_ref[0]
    acc = jnp.zeros((tk, D_MODEL), F32)
    for e in range(N_EXPERTS):
        onehot = jnp.where(slots[:, e:e + 1] == lane, 1.0, 0.0).astype(BF16)
        acc = acc + gates[:, e:e + 1] * _dot(onehot, y_ref[0, e])
    o_ref[0] = h_ref[0] + g2_ref[0] * acc


def _scatter_call(h, y, slot_col, g_col, g2, cap, ctx_rows):
    bsz, t, d = h.shape
    n_e = y.shape[1]
    tk = min(512, t)
    row = (lambda b: MOD_ROWS - 8) if ctx_rows else (lambda b: b)
    tile = lambda w: pl.BlockSpec((1, tk, w), lambda b, i: (b, i, 0))
    return pl.pallas_call(
        functools.partial(_scatter_kernel, cap=cap),
        out_shape=jax.ShapeDtypeStruct((bsz, t, d), F32),
        grid=(bsz, t // tk),
        in_specs=[tile(d),
                  pl.BlockSpec((1, n_e, cap, d), lambda b, i: (b, 0, 0, 0), pipeline_mode=pl.Buffered(1)),
                  tile(n_e), tile(n_e), pl.BlockSpec((1, 1, d), lambda b, i: (row(b), 0, 0))],
        out_specs=tile(d),
        compiler_params=_params(2),
        name="moe_scatter",
    )(h, y, slot_col, g_col, g2)


def _final_norm_kernel(h_ref, g_ref, o_ref):
    o_ref[0] = _rms(h_ref[0]) * g_ref[...]


def _final_norm_call(h, g):
    bsz, t, d = h.shape
    tm = min(512, t)
    tile = pl.BlockSpec((1, tm, d), lambda b, i: (b, i, 0))
    return pl.pallas_call(
        _final_norm_kernel,
        out_shape=jax.ShapeDtypeStruct((bsz, t, d), F32),
        grid=(bsz, t // tm),
        in_specs=[tile, pl.BlockSpec((1, d), lambda b, i: (0, 0))],
        out_specs=tile,
        compiler_params=_params(2),
        name="final_norm",
    )(h, g)


def _rope_tables(n_lat, n_ctx):
    rows = n_lat // GRID_W
    row = jnp.repeat(jnp.arange(rows, dtype=F32), GRID_W)
    col = jnp.tile(jnp.arange(GRID_W, dtype=F32), rows)
    inv = jnp.power(ROPE_THETA, -jnp.arange(ROPE_PAIRS_AXIS, dtype=F32) / ROPE_PAIRS_AXIS)
    ang = jnp.concatenate([row[:, None] * inv, col[:, None] * inv], axis=-1)
    cos, sin = jnp.cos(ang), jnp.sin(ang)
    cs = jnp.concatenate([cos, cos], axis=-1)
    sn = jnp.concatenate([-sin, sin], axis=-1)
    return cs, sn, jnp.ones((n_ctx, HEAD_DIM), F32), jnp.zeros((n_ctx, HEAD_DIM), F32)


def _dft_tables(n):
    k = jnp.arange(n, dtype=I32)
    m = ((2 * k[:, None] + 1) * k[None, :]) % (4 * n)
    ang = m.astype(F32) * (2.0 * math.pi / (4 * n))
    c = jnp.cos(ang)
    s = jnp.sin(ang)
    return c.astype(BF16), s.astype(BF16), c.T.astype(BF16), (-s.T).astype(BF16)


def _filter_features(n):
    t = jnp.linspace(0.0, 1.0, n, dtype=F32)[:, None]
    w = (2.0 * math.pi / n) * jnp.arange(n, dtype=F32)[:, None]
    f = jnp.linspace(1e-4, HY_BANDS - 1, HY_BANDS, dtype=F32)[None, :]
    feats = jnp.concatenate([t, jnp.cos(f * w), -jnp.sin(f * w)], axis=-1)
    return jnp.pad(feats, ((0, 0), (0, LANES - HY_EMB)))


def _pad_to(x, shape):
    return jnp.pad(x, [(0, s - d) for d, s in zip(x.shape, shape)])


def _block_diag_chunks(w):
    per = LANES // LRU_BLOCK
    w = w.reshape(2, LRU_BLOCKS // per, per, LRU_BLOCK, LRU_BLOCK)
    eye = jnp.eye(per, dtype=w.dtype)
    return jnp.einsum('dcpkj,pq->dcpkqj', w, eye).reshape(2, LRU_BLOCKS // per, LANES, LANES)


def kernel(x, c, ctx, c_ctx, mod_w, mod_b, norm1_g, norm2_g, w_in, q_norm_g, k_norm_g, hy_conv_w, hy_conv_b,
           hy_fw1, hy_fb1, hy_fw2, hy_fb2, hy_fw3, hy_freq, hy_bias, lru_conv_w, lru_conv_b, lru_wa, lru_ba,
           lru_wx, lru_bx, lru_lambda, w_attn_out, w_hy_out, w_lru_out, w_out, router_w, exp_w1, exp_w3, exp_w2,
           final_norm_g):
    bsz, n_lat, d = x.shape
    n_ctx = ctx.shape[1]
    depth = mod_w.shape[0]
    assert d == D_MODEL and bsz <= MOD_ROWS - 8 and n_lat % GRID_W == 0

    cs_l, sn_l, cs_c, sn_c = _rope_tables(n_lat, n_ctx)
    dft_l = _dft_tables(n_lat)
    dft_c = _dft_tables(n_ctx)
    feats_l = _filter_features(n_lat)
    feats_c = _filter_features(n_ctx)
    max_decay = math.log(HY_DECAY_TARGET) / HY_FAST_DECAY_PCT
    min_decay = math.log(HY_DECAY_TARGET) / HY_SLOW_DECAY_PCT
    deltas_abs = jnp.abs(jnp.linspace(min_decay, max_decay, HY_WIDTH, dtype=F32))[None, :]
    cap_l = max(1, EC_CAPACITY * n_lat // N_EXPERTS)
    cap_c = max(1, EC_CAPACITY * n_ctx // N_EXPERTS)

    cvec = jnp.zeros((MOD_ROWS, d), F32).at[:bsz].set(c).at[MOD_ROWS - 8].set(c_ctx)
    mod = _mod_call(cvec, mod_w, mod_b)
    mod = mod.reshape(depth, MOD_ROWS, 6, 1, d).transpose(0, 2, 1, 3, 4)

    h, hc = x, ctx
    for l in range(depth):
        last = l == depth - 1
        sh1, sc1, g1, sh2, sc2, g2 = (mod[l, j] for j in range(6))
        w_in_b = w_in[l].astype(BF16)
        n1 = norm1_g[l][None, :]
        n2 = norm2_g[l][None, :]
        qg = q_norm_g[l][None, :]
        kg = k_norm_g[l][None, :]

        q, k, v, hy, rx, rg, gate = _inproj_call(h, sh1, sc1, n1, w_in_b, cs_l, sn_l, qg, kg, False)
        qc, kc, vc, hyc, rxc, rgc, gatec = _inproj_call(hc, sh1, sc1, n1, w_in_b, cs_c, sn_c, qg, kg, True)

        attn = _attn_call(q, [(k, v), (kc, vc)])

        fw1 = _pad_to(hy_fw1[l], (LANES, LANES))
        fb1 = _pad_to(hy_fb1[l][None, :], (1, LANES))
        fw2 = _pad_to(hy_fw2[l], (LANES, LANES))
        fb2 = _pad_to(hy_fb2[l][None, :], (1, LANES))
        fw3 = _pad_to(hy_fw3[l], (LANES, 2 * HY_WIDTH))
        freq = _pad_to(hy_freq[l], (2, LANES))
        hbias = hy_bias[l][None, :]

        def hyena(hy_in, feats, tabs):
            ctab, stab, cttab, stntab = tabs
            hsd = _hy_filter_call(feats, fw1, fb1, fw2, fb2, fw3, freq, deltas_abs)
            kre, kim = _hy_kspec_call(ctab, stab, hsd)
            x0, w, wb = _hy_pre_call(hy_in, hy_conv_w[l], hy_conv_b[l][None, :])
            yre, yim = _hy_fwd_call(ctab, stab, wb, kre, kim)
            return _hy_inv_call(cttab, stntab, yre, yim, x0, w, hbias)

        hyo = hyena(hy, feats_l, dft_l)

        lruo, lruoc = _lru_call(rx, rxc, rg, rgc, lru_conv_w[l], lru_conv_b[l][None, :],
                                _block_diag_chunks(lru_wa[l]).astype(BF16), _block_diag_chunks(lru_wx[l]).astype(BF16),
                                lru_ba[l][:, None, :], lru_bx[l][:, None, :], lru_lambda[l][:, None, :])

        wa = w_attn_out[l].astype(BF16)
        wh = w_hy_out[l].astype(BF16)
        wl = w_lru_out[l].astype(BF16)
        wo = w_out[l].astype(BF16)
        rw_t = router_w[l].T.astype(BF16)
        e1 = exp_w1[l].astype(BF16)
        e3 = exp_w3[l].astype(BF16)
        e2 = exp_w2[l].astype(BF16)

        def channel_mix(h_in, attn_in, hy_in, lru_in, gate_in, cap, ctx_rows):
            hn, u2, lt = _merge_call(attn_in, hy_in, lru_in, gate_in, h_in, g1, wa, wh, wl, wo, n2, sh2, sc2, rw_t,
                                     ctx_rows)
            slot_row, slot_col, g_col = _router_call(lt, cap)
            y = _expert_call(u2, slot_row, e1, e3, e2, cap)
            return _scatter_call(hn, y, slot_col, g_col, g2, cap, ctx_rows)

        h = channel_mix(h, attn, hyo, lruo, gate, cap_l, False)
        if not last:
            attn_c = _attn_call(qc, [(kc, vc)])
            hyo_c = hyena(hyc, feats_c, dft_c)
            hc = channel_mix(hc, attn_c, hyo_c, lruoc, gatec, cap_c, True)

    return _final_norm_call(h, final_norm_g[None, :])
```

```python
import functools
import math

import jax
import jax.numpy as jnp
from jax import lax
from jax.experimental import pallas as pl
from jax.experimental.pallas import tpu as pltpu

F32 = jnp.float32
BF16 = jnp.bfloat16
I32 = jnp.int32

D_MODEL = 1024
GRID_W = 64
NORM_EPS = 1e-6
N_HEADS = 8
N_KV_HEADS = 2
HEAD_DIM = 128
GROUP = N_HEADS // N_KV_HEADS
ROPE_PAIRS_AXIS = HEAD_DIM // 4
ROPE_THETA = 10000.0
ATTN_SCALE = HEAD_DIM ** -0.5
Q_PRESCALE = ATTN_SCALE * math.log2(math.e)
Q_WIDTH = N_HEADS * HEAD_DIM
KV_WIDTH = N_KV_HEADS * HEAD_DIM
HY_WIDTH = D_MODEL // 2
HY_BANDS = 16
HY_EMB = 1 + 2 * HY_BANDS
HY_FILTER_HIDDEN = 64
HY_FAST_DECAY_PCT = 0.3
HY_SLOW_DECAY_PCT = 1.5
HY_DECAY_TARGET = 1e-2
LRU_WIDTH = D_MODEL // 2
LRU_BLOCKS = 8
LRU_BLOCK = LRU_WIDTH // LRU_BLOCKS
LRU_C = 8.0
N_BRANCH = 3
N_EXPERTS = 16
EC_CAPACITY = 2
GATE_WIDTH = N_BRANCH * D_MODEL
OFF_Q = 0
OFF_K = OFF_Q + Q_WIDTH
OFF_V = OFF_K + KV_WIDTH
OFF_HY = OFF_V + KV_WIDTH
OFF_RX = OFF_HY + 3 * HY_WIDTH
OFF_RG = OFF_RX + LRU_WIDTH
OFF_GATE = OFF_RG + LRU_WIDTH
IN_TOTAL = OFF_GATE + GATE_WIDTH

LANES = 128
SUBLANES = 8
V7X_VMEM_LIMIT_BYTES = 56 * 1024 * 1024
MOD_ROWS = 24


def _params(n_axes, vmem=V7X_VMEM_LIMIT_BYTES):
    return pltpu.CompilerParams(dimension_semantics=("arbitrary",) * n_axes, vmem_limit_bytes=vmem)


def _dot(a, b):
    return jnp.dot(a, b, preferred_element_type=F32)


def _dot_nt(a, b):
    return lax.dot_general(a, b, (((1,), (1,)), ((), ())), preferred_element_type=F32)


def _split2(x):
    hi = x.astype(BF16)
    lo = (x - hi.astype(F32)).astype(BF16)
    return hi, lo


def _dot3(a, b):
    ah, al = _split2(a)
    bh, bl = _split2(b)
    return _dot(ah, bh) + (_dot(ah, bl) + _dot(al, bh))


def _rms(x):
    return x * lax.rsqrt(jnp.mean(x * x, axis=-1, keepdims=True) + NORM_EPS)


def _sigmoid(x):
    return 1.0 / (1.0 + jnp.exp(-x))


def _silu(x):
    return x * _sigmoid(x)


def _gelu_tanh(x):
    return 0.5 * x * (1.0 + jnp.tanh(math.sqrt(2.0 / math.pi) * (x + 0.044715 * (x * x * x))))


def _shift_down(x, k):
    row = lax.broadcasted_iota(I32, x.shape, 0)
    return jnp.where(row >= k, pltpu.roll(x, k, 0), 0.0)


def _shift_up(x, k):
    n = x.shape[0]
    row = lax.broadcasted_iota(I32, x.shape, 0)
    return jnp.where(row < n - k, pltpu.roll(x, n - k, 0), 0.0)


def _mod_kernel(c_ref, w_ref, b_ref, o_ref):
    o_ref[0] = _dot3(_silu(c_ref[...]), w_ref[0]) + b_ref[0]


def _mod_call(cvec, mod_w, mod_b):
    depth, d, six_d = mod_w.shape
    tn = 1536
    return pl.pallas_call(
        _mod_kernel,
        out_shape=jax.ShapeDtypeStruct((depth, MOD_ROWS, six_d), F32),
        grid=(depth, six_d // tn),
        in_specs=[
            pl.BlockSpec((MOD_ROWS, d), lambda l, j: (0, 0)),
            pl.BlockSpec((1, d, tn), lambda l, j: (l, 0, j)),
            pl.BlockSpec((1, 1, tn), lambda l, j: (l, 0, j)),
        ],
        out_specs=pl.BlockSpec((1, MOD_ROWS, tn), lambda l, j: (l, 0, j)),
        compiler_params=_params(2),
        name="mod",
    )(cvec, mod_w, mod_b.reshape(depth, 1, six_d))


def _inproj_kernel(h_ref, sh_ref, sc_ref, g_ref, w_ref, cs_ref, sn_ref, qg_ref, kg_ref,
                   q_ref, k_ref, v_ref, hy_ref, rx_ref, rg_ref, gate_ref):
    u = (_rms(h_ref[0]) * g_ref[...] * (1.0 + sc_ref[0]) + sh_ref[0]).astype(BF16)
    cs = cs_ref[...]
    sn = sn_ref[...]

    def normed_rope(z, g):
        r = _rms(z) * g
        return r * cs + pltpu.roll(r, HEAD_DIM // 2, 1) * sn

    zq = _dot(u, w_ref[:, OFF_Q:OFF_Q + Q_WIDTH])
    for hd in range(N_HEADS):
        sl = slice(hd * HEAD_DIM, (hd + 1) * HEAD_DIM)
        q_ref[0, hd] = (normed_rope(zq[:, sl], qg_ref[...]) * Q_PRESCALE).astype(BF16)
    zkv = _dot(u, w_ref[:, OFF_K:OFF_K + 2 * KV_WIDTH])
    for hd in range(N_KV_HEADS):
        sl = slice(hd * HEAD_DIM, (hd + 1) * HEAD_DIM)
        k_ref[0, :, sl] = normed_rope(zkv[:, sl], kg_ref[...]).astype(BF16)
    v_ref[0] = zkv[:, KV_WIDTH:].astype(BF16)
    for j in range(3):
        sl = slice(j * HY_WIDTH, (j + 1) * HY_WIDTH)
        hy_ref[0, :, sl] = _dot(u, w_ref[:, OFF_HY + j * HY_WIDTH:OFF_HY + (j + 1) * HY_WIDTH])
    rr = _dot(u, w_ref[:, OFF_RX:OFF_RX + 2 * LRU_WIDTH])
    rx_ref[0] = rr[:, :LRU_WIDTH]
    rg_ref[0] = rr[:, LRU_WIDTH:].astype(BF16)
    for j in range(N_BRANCH):
        sl = slice(j * D_MODEL, (j + 1) * D_MODEL)
        gate_ref[0, :, sl] = _dot(u, w_ref[:, OFF_GATE + j * D_MODEL:OFF_GATE + (j + 1) * D_MODEL]).astype(BF16)


def _inproj_call(h, sh, sc, g, w_bf, cs, sn, qg, kg, ctx_rows):
    bsz, t, d = h.shape
    tm = min(512, t)
    row = (lambda b: MOD_ROWS - 8) if ctx_rows else (lambda b: b)
    tok = lambda w: pl.BlockSpec((1, tm, w), lambda b, i: (b, i, 0))
    modspec = pl.BlockSpec((1, 1, d), lambda b, i: (row(b), 0, 0))
    const = lambda shape: pl.BlockSpec(shape, lambda b, i: (0,) * len(shape))
    outs = [(KV_WIDTH, BF16), (KV_WIDTH, BF16), (3 * HY_WIDTH, F32), (LRU_WIDTH, F32), (LRU_WIDTH, BF16),
            (GATE_WIDTH, BF16)]
    return pl.pallas_call(
        _inproj_kernel,
        out_shape=[jax.ShapeDtypeStruct((bsz, N_HEADS, t, HEAD_DIM), BF16)]
        + [jax.ShapeDtypeStruct((bsz, t, w), dt) for w, dt in outs],
        grid=(bsz, t // tm),
        in_specs=[
            tok(d), modspec, modspec, const((1, d)),
            pl.BlockSpec((d, IN_TOTAL), lambda b, i: (0, 0), pipeline_mode=pl.Buffered(1)),
            pl.BlockSpec((tm, HEAD_DIM), lambda b, i: (i, 0)),
            pl.BlockSpec((tm, HEAD_DIM), lambda b, i: (i, 0)),
            const((1, HEAD_DIM)), const((1, HEAD_DIM)),
        ],
        out_specs=[pl.BlockSpec((1, N_HEADS, tm, HEAD_DIM), lambda b, i: (b, 0, i, 0))] + [tok(w) for w, _ in outs],
        compiler_params=_params(2),
        name="inproj",
    )(h, sh, sc, g, w_bf, cs, sn, qg, kg)


ATTN_TQ = 256
ATTN_TK = 512
ATTN_STREAMS = 2
NEG_BIG = -1e30


def _attn_kernel(*refs, n_src, tq):
    q_ref = refs[0]
    kv = refs[1:1 + 2 * n_src]
    o_ref, kcat, vext = refs[1 + 2 * n_src:]
    n_keys = kcat.shape[0]

    @pl.when(pl.program_id(2) == 0)
    def _():
        off = 0
        for i in range(n_src):
            n = kv[2 * i].shape[1]
            kcat[off:off + n, :] = kv[2 * i][0]
            vext[off:off + n, 0:HEAD_DIM] = kv[2 * i + 1][0]
            off += n
        vext[:, HEAD_DIM:] = jnp.ones((n_keys, HEAD_DIM), BF16)

    rows = GROUP * tq
    q = q_ref[0].reshape(rows, HEAD_DIM)
    per = rows // ATTN_STREAMS
    qs = [q[i * per:(i + 1) * per] for i in range(ATTN_STREAMS)]
    ms = [jnp.full((per, 1), NEG_BIG, F32) for _ in qs]
    accs = [jnp.zeros((per, 2 * HEAD_DIM), F32) for _ in qs]
    for off in range(0, n_keys, ATTN_TK):
        size = min(ATTN_TK, n_keys - off)
        kk = kcat[off:off + size, :]
        vv = vext[off:off + size, :]
        for i in range(ATTN_STREAMS):
            s = _dot_nt(qs[i], kk)
            m_new = jnp.maximum(ms[i], s.max(axis=-1, keepdims=True))
            p = jnp.exp2(s - m_new).astype(BF16)
            accs[i] = jnp.exp2(ms[i] - m_new) * accs[i] + _dot(p, vv)
            ms[i] = m_new
    out = jnp.concatenate([a[:, :HEAD_DIM] / a[:, HEAD_DIM:] for a in accs], axis=0).astype(BF16)
    for g in range(GROUP):
        o_ref[0, :, g * HEAD_DIM:(g + 1) * HEAD_DIM] = out[g * tq:(g + 1) * tq]


def _attn_call(q, kvs):
    bsz, _, tq_all, _ = q.shape
    tq = min(ATTN_TQ, tq_all)
    gw = GROUP * HEAD_DIM
    in_specs = [pl.BlockSpec((1, GROUP, tq, HEAD_DIM), lambda b, h, i: (b, h, i, 0))]
    args = [q]
    n_keys = 0
    for k, v in kvs:
        tk = k.shape[1]
        n_keys += tk
        spec = pl.BlockSpec((1, tk, HEAD_DIM), lambda b, h, i: (b, 0, h))
        in_specs += [spec, spec]
        args += [k, v]
    return pl.pallas_call(
        functools.partial(_attn_kernel, n_src=len(kvs), tq=tq),
        out_shape=jax.ShapeDtypeStruct((bsz, tq_all, Q_WIDTH), BF16),
        grid=(bsz, N_KV_HEADS, tq_all // tq),
        in_specs=in_specs,
        out_specs=pl.BlockSpec((1, tq, gw), lambda b, h, i: (b, i, h)),
        scratch_shapes=[pltpu.VMEM((n_keys, HEAD_DIM), BF16), pltpu.VMEM((n_keys, 2 * HEAD_DIM), BF16)],
        compiler_params=_params(3),
        name="attn",
    )(*args)


def _hy_pre_kernel(h0_ref, h1_ref, h2_ref, w0_ref, w1_ref, w2_ref, b0_ref, b1_ref, b2_ref,
                   x0_ref, w_ref, wb_ref):
    def conv(x_ref, cw_ref, cb_ref):
        x = x_ref[0]
        cw = cw_ref[...]
        return cb_ref[...] + _shift_down(x, 1) * cw[0:1] + x * cw[1:2] + _shift_up(x, 1) * cw[2:3]

    x0_ref[0] = conv(h0_ref, w0_ref, b0_ref)
    w = conv(h2_ref, w2_ref, b2_ref) * conv(h1_ref, w1_ref, b1_ref)
    w_ref[0] = w
    wb_ref[0] = w.astype(BF16)


def _hy_pre_call(hy, conv_w, conv_b):
    bsz, t, _ = hy.shape
    nch = HY_WIDTH // LANES
    xs = lambda part: pl.BlockSpec((1, t, LANES), lambda b, j: (b, 0, part * nch + j))
    ws = lambda part: pl.BlockSpec((3, LANES), lambda b, j: (0, part * nch + j))
    bs = lambda part: pl.BlockSpec((1, LANES), lambda b, j: (0, part * nch + j))
    out = pl.BlockSpec((1, t, LANES), lambda b, j: (b, 0, j))
    return pl.pallas_call(
        _hy_pre_kernel,
        out_shape=[jax.ShapeDtypeStruct((bsz, t, HY_WIDTH), F32), jax.ShapeDtypeStruct((bsz, t, HY_WIDTH), F32),
                   jax.ShapeDtypeStruct((bsz, t, HY_WIDTH), BF16)],
        grid=(bsz, nch),
        in_specs=[xs(0), xs(1), xs(2), ws(0), ws(1), ws(2), bs(0), bs(1), bs(2)],
        out_specs=[out, out, out],
        compiler_params=_params(2),
        name="hy_pre",
    )(hy, hy, hy, conv_w, conv_w, conv_w, conv_b, conv_b, conv_b)


def _hy_filter_kernel(feat_ref, w1_ref, b1_ref, w2_ref, b2_ref, w3_ref, fr_ref, dl_ref, o_ref):
    feats = feat_ref[...]
    hid = jnp.sin(fr_ref[0:1] * (_dot3(feats, w1_ref[...]) + b1_ref[...]))
    hid = jnp.sin(fr_ref[1:2] * (_dot3(hid, w2_ref[...]) + b2_ref[...]))
    filt = _dot3(hid, w3_ref[...])
    decay = jnp.exp(-feats[:, 0:1] * dl_ref[...])
    h_fwd = filt[:, :HY_WIDTH] * decay
    h_bwd = filt[:, HY_WIDTH:] * decay
    o_ref[:, :HY_WIDTH] = h_fwd + h_bwd
    o_ref[:, HY_WIDTH:] = h_bwd - h_fwd


def _hy_filter_call(feats, fw1, fb1, fw2, fb2, fw3, freq, deltas_abs):
    n = feats.shape[0]
    tn = min(512, n)
    const = lambda shape: pl.BlockSpec(shape, lambda i: (0,) * len(shape))
    return pl.pallas_call(
        _hy_filter_kernel,
        out_shape=jax.ShapeDtypeStruct((n, 2 * HY_WIDTH), F32),
        grid=(n // tn,),
        in_specs=[pl.BlockSpec((tn, LANES), lambda i: (i, 0)), const((LANES, LANES)), const((1, LANES)),
                  const((LANES, LANES)), const((1, LANES)), const((LANES, 2 * HY_WIDTH)), const((2, LANES)),
                  const((1, HY_WIDTH))],
        out_specs=pl.BlockSpec((tn, 2 * HY_WIDTH), lambda i: (i, 0)),
        compiler_params=_params(1),
        name="hy_filter",
    )(feats, fw1, fb1, fw2, fb2, fw3, freq, deltas_abs)


def _hy_kspec_kernel(c_ref, s_ref, h_ref, kre_ref, kim_ref, *, scale):
    hs_hi, hs_lo = _split2(h_ref[:, :HY_WIDTH])
    hd_hi, hd_lo = _split2(h_ref[:, HY_WIDTH:])
    c = c_ref[...]
    s = s_ref[...]
    kre_ref[...] = (_dot(c, hs_hi) + _dot(c, hs_lo)) * scale
    kim_ref[...] = (_dot(s, hd_hi) + _dot(s, hd_lo)) * scale


def _hy_kspec_call(ctab, stab, hsd):
    n = ctab.shape[0]
    tf = min(512, n)
    out = pl.BlockSpec((tf, HY_WIDTH), lambda i: (i, 0))
    return pl.pallas_call(
        functools.partial(_hy_kspec_kernel, scale=1.0 / n),
        out_shape=[jax.ShapeDtypeStruct((n, HY_WIDTH), F32)] * 2,
        grid=(n // tf,),
        in_specs=[pl.BlockSpec((tf, n), lambda i: (i, 0)), pl.BlockSpec((tf, n), lambda i: (i, 0)),
                  pl.BlockSpec((n, 2 * HY_WIDTH), lambda i: (0, 0))],
        out_specs=[out, out],
        compiler_params=_params(1),
        name="hy_kspec",
    )(ctab, stab, hsd)


def _hy_fwd_kernel(c_ref, s_ref, w_ref, kre_ref, kim_ref, yre_ref, yim_ref):
    w = w_ref[0]
    a = _dot(c_ref[...], w)
    b = _dot(s_ref[...], w)
    kre = kre_ref[...]
    kim = kim_ref[...]
    yre_ref[0] = (a * kre + b * kim).astype(BF16)
    yim_ref[0] = (a * kim - b * kre).astype(BF16)


def _hy_fwd_call(ctab, stab, wb, kre, kim):
    bsz, n, _ = wb.shape
    tf = min(512, n)
    tab = pl.BlockSpec((tf, n), lambda i, b: (i, 0))
    kk = pl.BlockSpec((tf, HY_WIDTH), lambda i, b: (i, 0))
    out = pl.BlockSpec((1, tf, HY_WIDTH), lambda i, b: (b, i, 0))
    return pl.pallas_call(
        _hy_fwd_kernel,
        out_shape=[jax.ShapeDtypeStruct((bsz, n, HY_WIDTH), BF16)] * 2,
        grid=(n // tf, bsz),
        in_specs=[tab, tab, pl.BlockSpec((1, n, HY_WIDTH), lambda i, b: (b, 0, 0)), kk, kk],
        out_specs=[out, out],
        compiler_params=_params(2),
        name="hy_fwd",
    )(ctab, stab, wb, kre, kim)


def _hy_inv_kernel(ct_ref, st_ref, yre_ref, yim_ref, x0_ref, w_ref, bias_ref, o_ref):
    y = _dot(ct_ref[...], yre_ref[0]) + _dot(st_ref[...], yim_ref[0])
    o_ref[0] = (x0_ref[0] * (y + w_ref[0] * bias_ref[...])).astype(BF16)


def _hy_inv_call(cttab, stntab, yre, yim, x0, w, bias):
    bsz, n, _ = yre.shape
    tt = min(512, n)
    tab = pl.BlockSpec((tt, n), lambda i, b: (i, 0))
    full = pl.BlockSpec((1, n, HY_WIDTH), lambda i, b: (b, 0, 0))
    tile = pl.BlockSpec((1, tt, HY_WIDTH), lambda i, b: (b, i, 0))
    return pl.pallas_call(
        _hy_inv_kernel,
        out_shape=jax.ShapeDtypeStruct((bsz, n, HY_WIDTH), BF16),
        grid=(n // tt, bsz),
        in_specs=[tab, tab, full, full, tile, tile, pl.BlockSpec((1, HY_WIDTH), lambda i, b: (0, 0))],
        out_specs=tile,
        compiler_params=_params(2),
        name="hy_inv",
    )(cttab, stntab, yre, yim, x0, w, bias)


LRU_SCAN_UNROLL = 4


def _lru_kernel(rxl_ref, rxc_ref, rgl_ref, rgc_ref, cw_ref, cb_ref, wa_ref, wx_ref, ba_ref, bx_ref, lam_ref,
                ol_ref, oc_ref, a_scr, b_scr, h_scr):
    cw = cw_ref[...]
    cb = cb_ref[...]
    row8 = lax.broadcasted_iota(I32, (SUBLANES, LANES), 0)

    def coeffs(x_ref, n):
        x = x_ref[0]
        xc = (cb + _shift_down(x, 2) * cw[0:1] + _shift_down(x, 1) * cw[1:2] + x * cw[2:3]
              + _shift_up(x, 1) * cw[3:4])
        xb = xc.astype(BF16)
        for d in range(2):
            r = _sigmoid(_dot(xb, wa_ref[d, 0]) + ba_ref[d])
            i = _sigmoid(_dot(xb, wx_ref[d, 0]) + bx_ref[d])
            nl = -lam_ref[d]
            softplus = jnp.maximum(nl, 0.0) + jnp.log(1.0 + jnp.exp(-jnp.abs(nl)))
            log_a = (-LRU_C) * r * softplus
            a = jnp.exp(log_a)
            a_scr[d, 0:n, :] = a
            b_scr[d, 0:n, :] = jnp.sqrt(1.0 - a * a) * i * xc

    def scan(n, carry_f, carry_b):
        ng = n // SUBLANES

        def body(g, carry):
            cf, cbk = carry
            of = pl.multiple_of(g * SUBLANES, SUBLANES)
            ob = pl.multiple_of((ng - 1 - g) * SUBLANES, SUBLANES)
            a = a_scr[0, pl.ds(of, SUBLANES), :]
            b = b_scr[0, pl.ds(of, SUBLANES), :]
            a2 = a_scr[1, pl.ds(ob, SUBLANES), :]
            b2 = b_scr[1, pl.ds(ob, SUBLANES), :]
            for k in (1, 2, 4):
                keep = row8 >= k
                b = a * jnp.where(keep, pltpu.roll(b, k, 0), 0.0) + b
                a = a * jnp.where(keep, pltpu.roll(a, k, 0), 1.0)
                keep2 = row8 < SUBLANES - k
                b2 = a2 * jnp.where(keep2, pltpu.roll(b2, SUBLANES - k, 0), 0.0) + b2
                a2 = a2 * jnp.where(keep2, pltpu.roll(a2, SUBLANES - k, 0), 1.0)
            hf = a * cf + b
            hb = a2 * cbk + b2
            h_scr[0, pl.ds(of, SUBLANES), :] = hf
            h_scr[1, pl.ds(ob, SUBLANES), :] = hb
            return hf[SUBLANES - 1:SUBLANES, :], hb[0:1, :]

        return lax.fori_loop(0, ng, body, (carry_f, carry_b), unroll=LRU_SCAN_UNROLL)

    n_ctx = rxc_ref.shape[1]
    n_lat = rxl_ref.shape[1]
    zero = jnp.zeros((1, LANES), F32)
    coeffs(rxc_ref, n_ctx)
    cf, cbk = scan(n_ctx, zero, zero)
    oc_ref[0] = ((h_scr[0, 0:n_ctx, :] + h_scr[1, 0:n_ctx, :]) * _gelu_tanh(rgc_ref[0].astype(F32))).astype(BF16)
    coeffs(rxl_ref, n_lat)
    scan(n_lat, cf, cbk)
    ol_ref[0] = ((h_scr[0] + h_scr[1]) * _gelu_tanh(rgl_ref[0].astype(F32))).astype(BF16)


def _lru_call(rx, rx_c, rg, rg_c, conv_w, conv_b, wa_bd, wx_bd, ba, bx, lam):
    bsz, t, _ = rx.shape
    tc = rx_c.shape[1]
    nch = LRU_WIDTH // LANES
    seq = lambda n: pl.BlockSpec((1, n, LANES), lambda b, j: (b, 0, j))
    vec = lambda r: pl.BlockSpec((r, 1, LANES), lambda b, j: (0, 0, j))
    wsp = pl.BlockSpec((2, 1, LANES, LANES), lambda b, j: (0, j, 0, 0))
    return pl.pallas_call(
        _lru_kernel,
        out_shape=[jax.ShapeDtypeStruct((bsz, t, LRU_WIDTH), BF16), jax.ShapeDtypeStruct((bsz, tc, LRU_WIDTH), BF16)],
        grid=(bsz, nch),
        in_specs=[seq(t), seq(tc), seq(t), seq(tc),
                  pl.BlockSpec((4, LANES), lambda b, j: (0, j)), pl.BlockSpec((1, LANES), lambda b, j: (0, j)),
                  wsp, wsp, vec(2), vec(2), vec(2)],
        out_specs=[seq(t), seq(tc)],
        scratch_shapes=[pltpu.VMEM((2, t, LANES), F32)] * 3,
        compiler_params=_params(2),
        name="lru",
    )(rx, rx_c, rg, rg_c, conv_w, conv_b, wa_bd, wx_bd, ba, bx, lam)


def _merge_kernel(attn_ref, hy_ref, lru_ref, gate_ref, h_ref, g1_ref, wa_ref, wh_ref, wl_ref, wo_ref,
                  n2_ref, sh2_ref, sc2_ref, rw_ref, hn_ref, u2_ref, lt_ref):
    def gate(j):
        return _sigmoid(gate_ref[0, :, j * D_MODEL:(j + 1) * D_MODEL].astype(F32))

    y = gate(0) * _dot(attn_ref[0], wa_ref[...])
    y = y + gate(1) * _dot(hy_ref[0], wh_ref[...])
    y = y + gate(2) * _dot(lru_ref[0], wl_ref[...])
    hn = h_ref[0] + g1_ref[0] * _dot(y.astype(BF16), wo_ref[...])
    hn_ref[0] = hn
    u2 = (_rms(hn) * n2_ref[...] * (1.0 + sc2_ref[0]) + sh2_ref[0]).astype(BF16)
    u2_ref[0] = u2
    lt_ref[0] = _dot_nt(rw_ref[...], u2)


def _merge_call(attn, hyo, lruo, gate, h, g1, wa, wh, wl, wo, n2g, sh2, sc2, rw_t, ctx_rows):
    bsz, t, d = h.shape
    tm = min(512, t)
    row = (lambda b: MOD_ROWS - 8) if ctx_rows else (lambda b: b)
    tok = lambda w: pl.BlockSpec((1, tm, w), lambda b, i: (b, i, 0))
    modspec = pl.BlockSpec((1, 1, d), lambda b, i: (row(b), 0, 0))
    const = lambda shape: pl.BlockSpec(shape, lambda b, i: (0,) * len(shape))
    return pl.pallas_call(
        _merge_kernel,
        out_shape=[jax.ShapeDtypeStruct((bsz, t, d), F32), jax.ShapeDtypeStruct((bsz, t, d), BF16),
                   jax.ShapeDtypeStruct((bsz, N_EXPERTS, t), F32)],
        grid=(bsz, t // tm),
        in_specs=[tok(Q_WIDTH), tok(HY_WIDTH), tok(LRU_WIDTH), tok(GATE_WIDTH), tok(d), modspec,
                  const((Q_WIDTH, d)), const((HY_WIDTH, d)), const((LRU_WIDTH, d)), const((d, d)),
                  const((1, d)), modspec, modspec, const((N_EXPERTS, d))],
        out_specs=[tok(d), tok(d), pl.BlockSpec((1, N_EXPERTS, tm), lambda b, i: (b, 0, i))],
        compiler_params=_params(2),
        name="merge",
    )(attn, hyo, lruo, gate, h, g1, wa, wh, wl, wo, n2g, sh2, sc2, rw_t)


def _router_kernel(lt_ref, slot_ref, slotc_ref, gc_ref, coff_ref, *, cap):
    lg = lt_ref[0]
    n_e, t = lg.shape
    ex = jnp.exp(lg - lg.max(axis=0, keepdims=True))
    aff = ex / ex.sum(axis=0, keepdims=True)
    key = pltpu.bitcast(aff, I32)
    capf = float(cap)

    def count(mask):
        return jnp.where(mask, 1.0, 0.0).sum(axis=1, keepdims=True)

    def vbody(i, thr):
        cand = thr | lax.shift_left(jnp.int32(1), 29 - i)
        return jnp.where(count(key >= cand) >= capf, cand, thr)

    thr = lax.fori_loop(0, 30, vbody, jnp.zeros((n_e, 1), I32))
    gt = key > thr
    eq = key == thr
    need = capf - count(gt)
    idx = lax.broadcasted_iota(I32, (n_e, t), 1)
    nbits = t.bit_length() - 1

    def ibody(i, lo):
        cand = lo | lax.shift_left(jnp.int32(1), nbits - 1 - i)
        return jnp.where(count(eq & (idx < cand)) < need, cand, lo)

    last = lax.fori_loop(0, nbits, ibody, jnp.zeros((n_e, 1), I32))
    sel = gt | (eq & (idx <= last))
    self32 = jnp.where(sel, 1.0, 0.0)
    gsel = jnp.where(sel, aff, 0.0)

    r_i = lax.broadcasted_iota(I32, (LANES, LANES), 0)
    c_i = lax.broadcasted_iota(I32, (LANES, LANES), 1)
    tri = jnp.where(r_i <= c_i, 1.0, 0.0).astype(BF16)
    eye = jnp.where(r_i == c_i, 1.0, 0.0).astype(BF16)
    off = jnp.zeros((n_e, 1), F32)
    n_units = t // LANES
    coff_ref[0] = jnp.zeros((n_e, LANES), I32)
    for c in range(n_units):
        sl = slice(c * LANES, (c + 1) * LANES)
        coff_ref[0, :, c:c + 1] = off.astype(I32)
        xs = self32[:, sl]
        inc = _dot(xs.astype(BF16), tri)
        slot1 = jnp.where(sel[:, sl], inc - xs + off + 1.0, 0.0)
        off = off + inc[:, LANES - 1:LANES]
        slot_ref[0, :, sl] = slot1.astype(I32) - 1
        hi = jnp.floor(slot1 * (1.0 / 16.0))
        lo = slot1 - 16.0 * hi
        col = 16.0 * _dot_nt(eye, hi.astype(BF16)) + _dot_nt(eye, lo.astype(BF16))
        slotc_ref[0, sl, :] = col.astype(I32) - 1
        g = gsel[:, sl]
        g1 = g.astype(BF16)
        r1 = g - g1.astype(F32)
        g2 = r1.astype(BF16)
        g3 = (r1 - g2.astype(F32)).astype(BF16)
        gc_ref[0, sl, :] = (_dot_nt(eye, g1) + _dot_nt(eye, g2)) + _dot_nt(eye, g3)
    coff_ref[0, :, n_units:n_units + 1] = off.astype(I32)


def _router_call(logits_t, cap):
    bsz, n_e, t = logits_t.shape
    assert t % LANES == 0 and t // LANES < LANES
    slot_row, slot_col, g_col, coff = pl.pallas_call(
        functools.partial(_router_kernel, cap=cap),
        out_shape=[jax.ShapeDtypeStruct((bsz, n_e, t), I32), jax.ShapeDtypeStruct((bsz, t, n_e), I32),
                   jax.ShapeDtypeStruct((bsz, t, n_e), F32), jax.ShapeDtypeStruct((bsz, n_e, LANES), I32)],
        grid=(bsz,),
        in_specs=[pl.BlockSpec((1, n_e, t), lambda b: (b, 0, 0))],
        out_specs=[pl.BlockSpec((1, n_e, t), lambda b: (b, 0, 0)), pl.BlockSpec((1, t, n_e), lambda b: (b, 0, 0)),
                   pl.BlockSpec((1, t, n_e), lambda b: (b, 0, 0)), pl.BlockSpec((1, n_e, LANES), lambda b: (b, 0, 0))],
        compiler_params=_params(1),
        name="router",
    )(logits_t)
    return slot_row, slot_col, g_col, coff[:, :, :t // LANES + 1].reshape(-1)


MOE_GATHER_TOKENS = 256
MOE_SLOT_ROWS = 128
MOE_SCATTER_SLOTS = 256


def _log2(n):
    assert n & (n - 1) == 0
    return n.bit_length() - 1


def _expert_kernel(coff_ref, u_ref, slot_ref, w1_ref, w3_ref, w2_ref, y_ref, xg_scr, *, tkg, sb, n_units):
    base = (pl.program_id(0) * pl.num_programs(1) + pl.program_id(1)) * (n_units + 1)
    upc = tkg // LANES
    rid = lax.broadcasted_iota(I32, (sb, tkg), 0)
    xg_scr[...] = jnp.zeros(xg_scr.shape, F32)

    def chunk(c, carry):
        lo = coff_ref[base + c * upc]
        hi = coff_ref[base + (c + 1) * upc]
        first = lax.shift_right_logical(lo, _log2(sb))
        stop = jnp.where(hi > lo, lax.shift_right_logical(hi + (sb - 1), _log2(sb)), first)

        def block(j, carry2):
            r0 = pl.multiple_of(j * sb, sb)
            onehot = jnp.where(slot_ref[0, 0, pl.ds(c, 1), :] - r0 == rid, 1.0, 0.0).astype(BF16)
            xg_scr[pl.ds(r0, sb), :] += _dot(onehot, u_ref[0, pl.ds(pl.multiple_of(c * tkg, tkg), tkg), :])
            return carry2

        return lax.fori_loop(first, stop, block, carry)

    lax.fori_loop(0, u_ref.shape[1] // tkg, chunk, 0)
    xb = xg_scr[...].astype(BF16)
    hid = (_silu(_dot(xb, w1_ref[0])) * _dot(xb, w3_ref[0])).astype(BF16)
    y_ref[0, 0] = _dot(hid, w2_ref[0]).astype(BF16)


def _expert_call(u2, slot_row, coff, w1, w3, w2, cap):
    bsz, t, d = u2.shape
    n_e = w1.shape[0]
    tkg = min(MOE_GATHER_TOKENS, t)
    sb = min(MOE_SLOT_ROWS, cap)
    wspec = pl.BlockSpec((1, d, d), lambda b, e, co: (e, 0, 0))
    grid_spec = pltpu.PrefetchScalarGridSpec(
        num_scalar_prefetch=1,
        grid=(bsz, n_e),
        in_specs=[pl.BlockSpec((1, t, d), lambda b, e, co: (b, 0, 0)),
                  pl.BlockSpec((1, 1, t // tkg, tkg), lambda b, e, co: (b, e, 0, 0)), wspec, wspec, wspec],
        out_specs=pl.BlockSpec((1, 1, cap, d), lambda b, e, co: (b, e, 0, 0)),
        scratch_shapes=[pltpu.VMEM((cap, d), F32)],
    )
    return pl.pallas_call(
        functools.partial(_expert_kernel, tkg=tkg, sb=sb, n_units=t // LANES),
        out_shape=jax.ShapeDtypeStruct((bsz, n_e, cap, d), BF16),
        grid_spec=grid_spec,
        compiler_params=_params(2),
        name="expert",
    )(coff, u2, slot_row.reshape(bsz, n_e, t // tkg, tkg), w1, w3, w2)


def _scatter_kernel(coff_ref, h_ref, y_ref, slotc_ref, gc_ref, g2_ref, o_ref, acc_scr, *, kb, n_units):
    tk = h_ref.shape[1]
    upt = tk // LANES
    i = pl.program_id(1)
    lane = lax.broadcasted_iota(I32, (tk, kb), 1)
    acc_scr[...] = jnp.zeros(acc_scr.shape, F32)
    for e in range(N_EXPERTS):
        base = (pl.program_id(0) * N_EXPERTS + e) * (n_units + 1)
        lo = coff_ref[base + i * upt]
        hi = coff_ref[base + (i + 1) * upt]
        first = lax.shift_right_logical(lo, _log2(kb))
        stop = jnp.where(hi > lo, lax.shift_right_logical(hi + (kb - 1), _log2(kb)), first)

        def block(j, carry, e=e):
            r0 = pl.multiple_of(j * kb, kb)
            onehot = jnp.where(slotc_ref[0, :, e:e + 1] - r0 == lane, 1.0, 0.0).astype(BF16)
            acc_scr[...] += gc_ref[0, :, e:e + 1] * _dot(onehot, y_ref[0, e, pl.ds(r0, kb), :])
            return carry

        lax.fori_loop(first, stop, block, 0)
    o_ref[0] = h_ref[0] + g2_ref[0] * acc_scr[...]


def _scatter_call(h, y, slot_col, g_col, coff, g2, cap, ctx_rows):
    bsz, t, d = h.shape
    n_e = y.shape[1]
    tk = min(512, t)
    kb = min(MOE_SCATTER_SLOTS, cap)
    row = (lambda b: MOD_ROWS - 8) if ctx_rows else (lambda b: b)
    tile = lambda w: pl.BlockSpec((1, tk, w), lambda b, i, co: (b, i, 0))
    grid_spec = pltpu.PrefetchScalarGridSpec(
        num_scalar_prefetch=1,
        grid=(bsz, t // tk),
        in_specs=[tile(d),
                  pl.BlockSpec((1, n_e, cap, d), lambda b, i, co: (b, 0, 0, 0), pipeline_mode=pl.Buffered(1)),
                  tile(n_e), tile(n_e), pl.BlockSpec((1, 1, d), lambda b, i, co: (row(b), 0, 0))],
        out_specs=tile(d),
        scratch_shapes=[pltpu.VMEM((tk, d), F32)],
    )
    return pl.pallas_call(
        functools.partial(_scatter_kernel, kb=kb, n_units=t // LANES),
        out_shape=jax.ShapeDtypeStruct((bsz, t, d), F32),
        grid_spec=grid_spec,
        compiler_params=_params(2),
        name="moe_scatter",
    )(coff, h, y, slot_col, g_col, g2)


def _final_norm_kernel(h_ref, g_ref, o_ref):
    o_ref[0] = _rms(h_ref[0]) * g_ref[...]


def _final_norm_call(h, g):
    bsz, t, d = h.shape
    tm = min(512, t)
    tile = pl.BlockSpec((1, tm, d), lambda b, i: (b, i, 0))
    return pl.pallas_call(
        _final_norm_kernel,
        out_shape=jax.ShapeDtypeStruct((bsz, t, d), F32),
        grid=(bsz, t // tm),
        in_specs=[tile, pl.BlockSpec((1, d), lambda b, i: (0, 0))],
        out_specs=tile,
        compiler_params=_params(2),
        name="final_norm",
    )(h, g)


def _rope_tables(n_lat, n_ctx):
    rows = n_lat // GRID_W
    row = jnp.repeat(jnp.arange(rows, dtype=F32), GRID_W)
    col = jnp.tile(jnp.arange(GRID_W, dtype=F32), rows)
    inv = jnp.power(ROPE_THETA, -jnp.arange(ROPE_PAIRS_AXIS, dtype=F32) / ROPE_PAIRS_AXIS)
    ang = jnp.concatenate([row[:, None] * inv, col[:, None] * inv], axis=-1)
    cos, sin = jnp.cos(ang), jnp.sin(ang)
    cs = jnp.concatenate([cos, cos], axis=-1)
    sn = jnp.concatenate([-sin, sin], axis=-1)
    return cs, sn, jnp.ones((n_ctx, HEAD_DIM), F32), jnp.zeros((n_ctx, HEAD_DIM), F32)


def _dft_tables(n):
    k = jnp.arange(n, dtype=I32)
    m = ((2 * k[:, None] + 1) * k[None, :]) % (4 * n)
    ang = m.astype(F32) * (2.0 * math.pi / (4 * n))
    c = jnp.cos(ang)
    s = jnp.sin(ang)
    return c.astype(BF16), s.astype(BF16), c.T.astype(BF16), (-s.T).astype(BF16)


def _filter_features(n):
    t = jnp.linspace(0.0, 1.0, n, dtype=F32)[:, None]
    w = (2.0 * math.pi / n) * jnp.arange(n, dtype=F32)[:, None]
    f = jnp.linspace(1e-4, HY_BANDS - 1, HY_BANDS, dtype=F32)[None, :]
    feats = jnp.concatenate([t, jnp.cos(f * w), -jnp.sin(f * w)], axis=-1)
    return jnp.pad(feats, ((0, 0), (0, LANES - HY_EMB)))


def _pad_to(x, shape):
    return jnp.pad(x, [(0, s - d) for d, s in zip(x.shape, shape)])


def _block_diag_chunks(w):
    per = LANES // LRU_BLOCK
    w = w.reshape(2, LRU_BLOCKS // per, per, LRU_BLOCK, LRU_BLOCK)
    eye = jnp.eye(per, dtype=w.dtype)
    return jnp.einsum('dcpkj,pq->dcpkqj', w, eye).reshape(2, LRU_BLOCKS // per, LANES, LANES)


def kernel(x, c, ctx, c_ctx, mod_w, mod_b, norm1_g, norm2_g, w_in, q_norm_g, k_norm_g, hy_conv_w, hy_conv_b,
           hy_fw1, hy_fb1, hy_fw2, hy_fb2, hy_fw3, hy_freq, hy_bias, lru_conv_w, lru_conv_b, lru_wa, lru_ba,
           lru_wx, lru_bx, lru_lambda, w_attn_out, w_hy_out, w_lru_out, w_out, router_w, exp_w1, exp_w3, exp_w2,
           final_norm_g):
    bsz, n_lat, d = x.shape
    n_ctx = ctx.shape[1]
    depth = mod_w.shape[0]
    assert d == D_MODEL and bsz <= MOD_ROWS - 8 and n_lat % GRID_W == 0

    cs_l, sn_l, cs_c, sn_c = _rope_tables(n_lat, n_ctx)
    dft_l = _dft_tables(n_lat)
    dft_c = _dft_tables(n_ctx)
    feats_l = _filter_features(n_lat)
    feats_c = _filter_features(n_ctx)
    max_decay = math.log(HY_DECAY_TARGET) / HY_FAST_DECAY_PCT
    min_decay = math.log(HY_DECAY_TARGET) / HY_SLOW_DECAY_PCT
    deltas_abs = jnp.abs(jnp.linspace(min_decay, max_decay, HY_WIDTH, dtype=F32))[None, :]
    cap_l = max(1, EC_CAPACITY * n_lat // N_EXPERTS)
    cap_c = max(1, EC_CAPACITY * n_ctx // N_EXPERTS)

    cvec = jnp.zeros((MOD_ROWS, d), F32).at[:bsz].set(c).at[MOD_ROWS - 8].set(c_ctx)
    mod = _mod_call(cvec, mod_w, mod_b)
    mod = mod.reshape(depth, MOD_ROWS, 6, 1, d).transpose(0, 2, 1, 3, 4)

    h, hc = x, ctx
    for l in range(depth):
        last = l == depth - 1
        sh1, sc1, g1, sh2, sc2, g2 = (mod[l, j] for j in range(6))
        w_in_b = w_in[l].astype(BF16)
        n1 = norm1_g[l][None, :]
        n2 = norm2_g[l][None, :]
        qg = q_norm_g[l][None, :]
        kg = k_norm_g[l][None, :]

        q, k, v, hy, rx, rg, gate = _inproj_call(h, sh1, sc1, n1, w_in_b, cs_l, sn_l, qg, kg, False)
        qc, kc, vc, hyc, rxc, rgc, gatec = _inproj_call(hc, sh1, sc1, n1, w_in_b, cs_c, sn_c, qg, kg, True)

        attn = _attn_call(q, [(k, v), (kc, vc)])

        fw1 = _pad_to(hy_fw1[l], (LANES, LANES))
        fb1 = _pad_to(hy_fb1[l][None, :], (1, LANES))
        fw2 = _pad_to(hy_fw2[l], (LANES, LANES))
        fb2 = _pad_to(hy_fb2[l][None, :], (1, LANES))
        fw3 = _pad_to(hy_fw3[l], (LANES, 2 * HY_WIDTH))
        freq = _pad_to(hy_freq[l], (2, LANES))
        hbias = hy_bias[l][None, :]

        def hyena(hy_in, feats, tabs):
            ctab, stab, cttab, stntab = tabs
            hsd = _hy_filter_call(feats, fw1, fb1, fw2, fb2, fw3, freq, deltas_abs)
            kre, kim = _hy_kspec_call(ctab, stab, hsd)
            x0, w, wb = _hy_pre_call(hy_in, hy_conv_w[l], hy_conv_b[l][None, :])
            yre, yim = _hy_fwd_call(ctab, stab, wb, kre, kim)
            return _hy_inv_call(cttab, stntab, yre, yim, x0, w, hbias)

        hyo = hyena(hy, feats_l, dft_l)

        lruo, lruoc = _lru_call(rx, rxc, rg, rgc, lru_conv_w[l], lru_conv_b[l][None, :],
                                _block_diag_chunks(lru_wa[l]).astype(BF16), _block_diag_chunks(lru_wx[l]).astype(BF16),
                                lru_ba[l][:, None, :], lru_bx[l][:, None, :], lru_lambda[l][:, None, :])

        wa = w_attn_out[l].astype(BF16)
        wh = w_hy_out[l].astype(BF16)
        wl = w_lru_out[l].astype(BF16)
        wo = w_out[l].astype(BF16)
        rw_t = router_w[l].T.astype(BF16)
        e1 = exp_w1[l].astype(BF16)
        e3 = exp_w3[l].astype(BF16)
        e2 = exp_w2[l].astype(BF16)

        def channel_mix(h_in, attn_in, hy_in, lru_in, gate_in, cap, ctx_rows):
            hn, u2, lt = _merge_call(attn_in, hy_in, lru_in, gate_in, h_in, g1, wa, wh, wl, wo, n2, sh2, sc2, rw_t,
                                     ctx_rows)
            slot_row, slot_col, g_col, coff = _router_call(lt, cap)
            y = _expert_call(u2, slot_row, coff, e1, e3, e2, cap)
            return _scatter_call(hn, y, slot_col, g_col, coff, g2, cap, ctx_rows)

        h = channel_mix(h, attn, hyo, lruo, gate, cap_l, False)
        if not last:
            attn_c = _attn_call(qc, [(kc, vc)])
            hyo_c = hyena(hyc, feats_c, dft_c)
            hc = channel_mix(hc, attn_c, hyo_c, lruoc, gatec, cap_c, True)

    return _final_norm_call(h, final_norm_g[None, :])
```

```python
import functools
import math

import jax
import jax.numpy as jnp
from jax import lax
from jax.experimental import pallas as pl
from jax.experimental.pallas import tpu as pltpu

F32 = jnp.float32
BF16 = jnp.bfloat16
I32 = jnp.int32

D_MODEL = 1024
GRID_W = 64
NORM_EPS = 1e-6
N_HEADS = 8
N_KV_HEADS = 2
HEAD_DIM = 128
GROUP = N_HEADS // N_KV_HEADS
ROPE_PAIRS_AXIS = HEAD_DIM // 4
ROPE_THETA = 10000.0
ATTN_SCALE = HEAD_DIM ** -0.5
Q_PRESCALE = ATTN_SCALE * math.log2(math.e)
Q_WIDTH = N_HEADS * HEAD_DIM
KV_WIDTH = N_KV_HEADS * HEAD_DIM
HY_WIDTH = D_MODEL // 2
HY_BANDS = 16
HY_EMB = 1 + 2 * HY_BANDS
HY_FILTER_HIDDEN = 64
HY_FAST_DECAY_PCT = 0.3
HY_SLOW_DECAY_PCT = 1.5
HY_DECAY_TARGET = 1e-2
LRU_WIDTH = D_MODEL // 2
LRU_BLOCKS = 8
LRU_BLOCK = LRU_WIDTH // LRU_BLOCKS
LRU_C = 8.0
N_BRANCH = 3
N_EXPERTS = 16
EC_CAPACITY = 2
GATE_WIDTH = N_BRANCH * D_MODEL
OFF_Q = 0
OFF_K = OFF_Q + Q_WIDTH
OFF_V = OFF_K + KV_WIDTH
OFF_HY = OFF_V + KV_WIDTH
OFF_RX = OFF_HY + 3 * HY_WIDTH
OFF_RG = OFF_RX + LRU_WIDTH
OFF_GATE = OFF_RG + LRU_WIDTH
IN_TOTAL = OFF_GATE + GATE_WIDTH

LANES = 128
SUBLANES = 8
V7X_VMEM_LIMIT_BYTES = 56 * 1024 * 1024
MOD_ROWS = 24


def _params(n_axes, vmem=V7X_VMEM_LIMIT_BYTES):
    return pltpu.CompilerParams(dimension_semantics=("arbitrary",) * n_axes, vmem_limit_bytes=vmem)


def _dot(a, b):
    return jnp.dot(a, b, preferred_element_type=F32)


def _dot_nt(a, b):
    return lax.dot_general(a, b, (((1,), (1,)), ((), ())), preferred_element_type=F32)


def _split2(x):
    hi = x.astype(BF16)
    lo = (x - hi.astype(F32)).astype(BF16)
    return hi, lo


def _dot3(a, b):
    ah, al = _split2(a)
    bh, bl = _split2(b)
    return _dot(ah, bh) + (_dot(ah, bl) + _dot(al, bh))


def _rms(x):
    return x * lax.rsqrt(jnp.mean(x * x, axis=-1, keepdims=True) + NORM_EPS)


def _sigmoid(x):
    return 1.0 / (1.0 + jnp.exp(-x))


def _silu(x):
    return x * _sigmoid(x)


def _gelu_tanh(x):
    return 0.5 * x * (1.0 + jnp.tanh(math.sqrt(2.0 / math.pi) * (x + 0.044715 * (x * x * x))))


def _shift_down(x, k):
    row = lax.broadcasted_iota(I32, x.shape, 0)
    return jnp.where(row >= k, pltpu.roll(x, k, 0), 0.0)


def _shift_up(x, k):
    n = x.shape[0]
    row = lax.broadcasted_iota(I32, x.shape, 0)
    return jnp.where(row < n - k, pltpu.roll(x, n - k, 0), 0.0)


def _mod_kernel(c_ref, w_ref, b_ref, o_ref):
    o_ref[0] = _dot3(_silu(c_ref[...]), w_ref[0]) + b_ref[0]


def _mod_call(cvec, mod_w, mod_b):
    depth, d, six_d = mod_w.shape
    tn = 1536
    return pl.pallas_call(
        _mod_kernel,
        out_shape=jax.ShapeDtypeStruct((depth, MOD_ROWS, six_d), F32),
        grid=(depth, six_d // tn),
        in_specs=[
            pl.BlockSpec((MOD_ROWS, d), lambda l, j: (0, 0)),
            pl.BlockSpec((1, d, tn), lambda l, j: (l, 0, j)),
            pl.BlockSpec((1, 1, tn), lambda l, j: (l, 0, j)),
        ],
        out_specs=pl.BlockSpec((1, MOD_ROWS, tn), lambda l, j: (l, 0, j)),
        compiler_params=_params(2),
        name="mod",
    )(cvec, mod_w, mod_b.reshape(depth, 1, six_d))


def _inproj_kernel(h_ref, sh_ref, sc_ref, g_ref, w_ref, cs_ref, sn_ref, qg_ref, kg_ref,
                   q_ref, k_ref, v_ref, hy_ref, rx_ref, rg_ref, gate_ref):
    u = (_rms(h_ref[0]) * g_ref[...] * (1.0 + sc_ref[0]) + sh_ref[0]).astype(BF16)
    cs = cs_ref[...]
    sn = sn_ref[...]

    def normed_rope(z, g):
        r = _rms(z) * g
        return r * cs + pltpu.roll(r, HEAD_DIM // 2, 1) * sn

    zq = _dot(u, w_ref[:, OFF_Q:OFF_Q + Q_WIDTH])
    for hd in range(N_HEADS):
        sl = slice(hd * HEAD_DIM, (hd + 1) * HEAD_DIM)
        q_ref[0, hd] = (normed_rope(zq[:, sl], qg_ref[...]) * Q_PRESCALE).astype(BF16)
    zkv = _dot(u, w_ref[:, OFF_K:OFF_K + 2 * KV_WIDTH])
    for hd in range(N_KV_HEADS):
        sl = slice(hd * HEAD_DIM, (hd + 1) * HEAD_DIM)
        k_ref[0, :, sl] = normed_rope(zkv[:, sl], kg_ref[...]).astype(BF16)
    v_ref[0] = zkv[:, KV_WIDTH:].astype(BF16)
    for j in range(3):
        sl = slice(j * HY_WIDTH, (j + 1) * HY_WIDTH)
        hy_ref[0, :, sl] = _dot(u, w_ref[:, OFF_HY + j * HY_WIDTH:OFF_HY + (j + 1) * HY_WIDTH])
    rr = _dot(u, w_ref[:, OFF_RX:OFF_RX + 2 * LRU_WIDTH])
    rx_ref[0] = rr[:, :LRU_WIDTH]
    rg_ref[0] = rr[:, LRU_WIDTH:].astype(BF16)
    for j in range(N_BRANCH):
        sl = slice(j * D_MODEL, (j + 1) * D_MODEL)
        gate_ref[0, :, sl] = _dot(u, w_ref[:, OFF_GATE + j * D_MODEL:OFF_GATE + (j + 1) * D_MODEL]).astype(BF16)


def _inproj_call(h, sh, sc, g, w_bf, cs, sn, qg, kg, ctx_rows):
    bsz, t, d = h.shape
    tm = min(512, t)
    row = (lambda b: MOD_ROWS - 8) if ctx_rows else (lambda b: b)
    tok = lambda w: pl.BlockSpec((1, tm, w), lambda b, i: (b, i, 0))
    modspec = pl.BlockSpec((1, 1, d), lambda b, i: (row(b), 0, 0))
    const = lambda shape: pl.BlockSpec(shape, lambda b, i: (0,) * len(shape))
    outs = [(KV_WIDTH, BF16), (KV_WIDTH, BF16), (3 * HY_WIDTH, F32), (LRU_WIDTH, F32), (LRU_WIDTH, BF16),
            (GATE_WIDTH, BF16)]
    return pl.pallas_call(
        _inproj_kernel,
        out_shape=[jax.ShapeDtypeStruct((bsz, N_HEADS, t, HEAD_DIM), BF16)]
        + [jax.ShapeDtypeStruct((bsz, t, w), dt) for w, dt in outs],
        grid=(bsz, t // tm),
        in_specs=[
            tok(d), modspec, modspec, const((1, d)),
            pl.BlockSpec((d, IN_TOTAL), lambda b, i: (0, 0), pipeline_mode=pl.Buffered(1)),
            pl.BlockSpec((tm, HEAD_DIM), lambda b, i: (i, 0)),
            pl.BlockSpec((tm, HEAD_DIM), lambda b, i: (i, 0)),
            const((1, HEAD_DIM)), const((1, HEAD_DIM)),
        ],
        out_specs=[pl.BlockSpec((1, N_HEADS, tm, HEAD_DIM), lambda b, i: (b, 0, i, 0))] + [tok(w) for w, _ in outs],
        compiler_params=_params(2),
        name="inproj",
    )(h, sh, sc, g, w_bf, cs, sn, qg, kg)


ATTN_TQ = 256
ATTN_TK = 512
ATTN_STREAMS = 2
NEG_BIG = -1e30


def _attn_kernel(*refs, n_src, tq):
    q_ref = refs[0]
    kv = refs[1:1 + 2 * n_src]
    o_ref, kcat, vext = refs[1 + 2 * n_src:]
    n_keys = kcat.shape[0]

    @pl.when(pl.program_id(2) == 0)
    def _():
        off = 0
        for i in range(n_src):
            n = kv[2 * i].shape[1]
            kcat[off:off + n, :] = kv[2 * i][0]
            vext[off:off + n, 0:HEAD_DIM] = kv[2 * i + 1][0]
            off += n
        vext[:, HEAD_DIM:] = jnp.ones((n_keys, HEAD_DIM), BF16)

    rows = GROUP * tq
    q = q_ref[0].reshape(rows, HEAD_DIM)
    per = rows // ATTN_STREAMS
    qs = [q[i * per:(i + 1) * per] for i in range(ATTN_STREAMS)]
    ms = [jnp.full((per, 1), NEG_BIG, F32) for _ in qs]
    accs = [jnp.zeros((per, 2 * HEAD_DIM), F32) for _ in qs]
    for off in range(0, n_keys, ATTN_TK):
        size = min(ATTN_TK, n_keys - off)
        kk = kcat[off:off + size, :]
        vv = vext[off:off + size, :]
        for i in range(ATTN_STREAMS):
            s = _dot_nt(qs[i], kk)
            m_new = jnp.maximum(ms[i], s.max(axis=-1, keepdims=True))
            p = jnp.exp2(s - m_new).astype(BF16)
            accs[i] = jnp.exp2(ms[i] - m_new) * accs[i] + _dot(p, vv)
            ms[i] = m_new
    out = jnp.concatenate([a[:, :HEAD_DIM] / a[:, HEAD_DIM:] for a in accs], axis=0).astype(BF16)
    for g in range(GROUP):
        o_ref[0, :, g * HEAD_DIM:(g + 1) * HEAD_DIM] = out[g * tq:(g + 1) * tq]


def _attn_call(q, kvs):
    bsz, _, tq_all, _ = q.shape
    tq = min(ATTN_TQ, tq_all)
    gw = GROUP * HEAD_DIM
    in_specs = [pl.BlockSpec((1, GROUP, tq, HEAD_DIM), lambda b, h, i: (b, h, i, 0))]
    args = [q]
    n_keys = 0
    for k, v in kvs:
        tk = k.shape[1]
        n_keys += tk
        spec = pl.BlockSpec((1, tk, HEAD_DIM), lambda b, h, i: (b, 0, h))
        in_specs += [spec, spec]
        args += [k, v]
    return pl.pallas_call(
        functools.partial(_attn_kernel, n_src=len(kvs), tq=tq),
        out_shape=jax.ShapeDtypeStruct((bsz, tq_all, Q_WIDTH), BF16),
        grid=(bsz, N_KV_HEADS, tq_all // tq),
        in_specs=in_specs,
        out_specs=pl.BlockSpec((1, tq, gw), lambda b, h, i: (b, i, h)),
        scratch_shapes=[pltpu.VMEM((n_keys, HEAD_DIM), BF16), pltpu.VMEM((n_keys, 2 * HEAD_DIM), BF16)],
        compiler_params=_params(3),
        name="attn",
    )(*args)


def _hy_pre_kernel(h0_ref, h1_ref, h2_ref, w0_ref, w1_ref, w2_ref, b0_ref, b1_ref, b2_ref,
                   x0_ref, w_ref, wb_ref):
    def conv(x_ref, cw_ref, cb_ref):
        x = x_ref[0]
        cw = cw_ref[...]
        return cb_ref[...] + _shift_down(x, 1) * cw[0:1] + x * cw[1:2] + _shift_up(x, 1) * cw[2:3]

    x0_ref[0] = conv(h0_ref, w0_ref, b0_ref)
    w = conv(h2_ref, w2_ref, b2_ref) * conv(h1_ref, w1_ref, b1_ref)
    w_ref[0] = w
    wb_ref[0] = w.astype(BF16)


def _hy_pre_call(hy, conv_w, conv_b):
    bsz, t, _ = hy.shape
    nch = HY_WIDTH // LANES
    xs = lambda part: pl.BlockSpec((1, t, LANES), lambda b, j: (b, 0, part * nch + j))
    ws = lambda part: pl.BlockSpec((3, LANES), lambda b, j: (0, part * nch + j))
    bs = lambda part: pl.BlockSpec((1, LANES), lambda b, j: (0, part * nch + j))
    out = pl.BlockSpec((1, t, LANES), lambda b, j: (b, 0, j))
    return pl.pallas_call(
        _hy_pre_kernel,
        out_shape=[jax.ShapeDtypeStruct((bsz, t, HY_WIDTH), F32), jax.ShapeDtypeStruct((bsz, t, HY_WIDTH), F32),
                   jax.ShapeDtypeStruct((bsz, t, HY_WIDTH), BF16)],
        grid=(bsz, nch),
        in_specs=[xs(0), xs(1), xs(2), ws(0), ws(1), ws(2), bs(0), bs(1), bs(2)],
        out_specs=[out, out, out],
        compiler_params=_params(2),
        name="hy_pre",
    )(hy, hy, hy, conv_w, conv_w, conv_w, conv_b, conv_b, conv_b)


def _hy_filter_kernel(feat_ref, w1_ref, b1_ref, w2_ref, b2_ref, w3_ref, fr_ref, dl_ref, o_ref):
    feats = feat_ref[...]
    hid = jnp.sin(fr_ref[0:1] * (_dot3(feats, w1_ref[...]) + b1_ref[...]))
    hid = jnp.sin(fr_ref[1:2] * (_dot3(hid, w2_ref[...]) + b2_ref[...]))
    filt = _dot3(hid, w3_ref[...])
    decay = jnp.exp(-feats[:, 0:1] * dl_ref[...])
    h_fwd = filt[:, :HY_WIDTH] * decay
    h_bwd = filt[:, HY_WIDTH:] * decay
    o_ref[:, :HY_WIDTH] = h_fwd + h_bwd
    o_ref[:, HY_WIDTH:] = h_bwd - h_fwd


def _hy_filter_call(feats, fw1, fb1, fw2, fb2, fw3, freq, deltas_abs):
    n = feats.shape[0]
    tn = min(512, n)
    const = lambda shape: pl.BlockSpec(shape, lambda i: (0,) * len(shape))
    return pl.pallas_call(
        _hy_filter_kernel,
        out_shape=jax.ShapeDtypeStruct((n, 2 * HY_WIDTH), F32),
        grid=(n // tn,),
        in_specs=[pl.BlockSpec((tn, LANES), lambda i: (i, 0)), const((LANES, LANES)), const((1, LANES)),
                  const((LANES, LANES)), const((1, LANES)), const((LANES, 2 * HY_WIDTH)), const((2, LANES)),
                  const((1, HY_WIDTH))],
        out_specs=pl.BlockSpec((tn, 2 * HY_WIDTH), lambda i: (i, 0)),
        compiler_params=_params(1),
        name="hy_filter",
    )(feats, fw1, fb1, fw2, fb2, fw3, freq, deltas_abs)


def _hy_kspec_kernel(c_ref, s_ref, h_ref, kre_ref, kim_ref, *, scale):
    hs_hi, hs_lo = _split2(h_ref[:, :HY_WIDTH])
    hd_hi, hd_lo = _split2(h_ref[:, HY_WIDTH:])
    c = c_ref[...]
    s = s_ref[...]
    kre_ref[...] = (_dot(c, hs_hi) + _dot(c, hs_lo)) * scale
    kim_ref[...] = (_dot(s, hd_hi) + _dot(s, hd_lo)) * scale


def _hy_kspec_call(ctab, stab, hsd):
    n = ctab.shape[0]
    tf = min(512, n)
    out = pl.BlockSpec((tf, HY_WIDTH), lambda i: (i, 0))
    return pl.pallas_call(
        functools.partial(_hy_kspec_kernel, scale=1.0 / n),
        out_shape=[jax.ShapeDtypeStruct((n, HY_WIDTH), F32)] * 2,
        grid=(n // tf,),
        in_specs=[pl.BlockSpec((tf, n), lambda i: (i, 0)), pl.BlockSpec((tf, n), lambda i: (i, 0)),
                  pl.BlockSpec((n, 2 * HY_WIDTH), lambda i: (0, 0))],
        out_specs=[out, out],
        compiler_params=_params(1),
        name="hy_kspec",
    )(ctab, stab, hsd)


def _hy_fwd_kernel(c_ref, s_ref, w_ref, kre_ref, kim_ref, yre_ref, yim_ref):
    w = w_ref[0]
    a = _dot(c_ref[...], w)
    b = _dot(s_ref[...], w)
    kre = kre_ref[...]
    kim = kim_ref[...]
    yre_ref[0] = (a * kre + b * kim).astype(BF16)
    yim_ref[0] = (a * kim - b * kre).astype(BF16)


def _hy_fwd_call(ctab, stab, wb, kre, kim):
    bsz, n, _ = wb.shape
    tf = min(512, n)
    tab = pl.BlockSpec((tf, n), lambda i, b: (i, 0))
    kk = pl.BlockSpec((tf, HY_WIDTH), lambda i, b: (i, 0))
    out = pl.BlockSpec((1, tf, HY_WIDTH), lambda i, b: (b, i, 0))
    return pl.pallas_call(
        _hy_fwd_kernel,
        out_shape=[jax.ShapeDtypeStruct((bsz, n, HY_WIDTH), BF16)] * 2,
        grid=(n // tf, bsz),
        in_specs=[tab, tab, pl.BlockSpec((1, n, HY_WIDTH), lambda i, b: (b, 0, 0)), kk, kk],
        out_specs=[out, out],
        compiler_params=_params(2),
        name="hy_fwd",
    )(ctab, stab, wb, kre, kim)


def _hy_inv_kernel(ct_ref, st_ref, yre_ref, yim_ref, x0_ref, w_ref, bias_ref, o_ref):
    y = _dot(ct_ref[...], yre_ref[0]) + _dot(st_ref[...], yim_ref[0])
    o_ref[0] = (x0_ref[0] * (y + w_ref[0] * bias_ref[...])).astype(BF16)


def _hy_inv_call(cttab, stntab, yre, yim, x0, w, bias):
    bsz, n, _ = yre.shape
    tt = min(512, n)
    tab = pl.BlockSpec((tt, n), lambda i, b: (i, 0))
    full = pl.BlockSpec((1, n, HY_WIDTH), lambda i, b: (b, 0, 0))
    tile = pl.BlockSpec((1, tt, HY_WIDTH), lambda i, b: (b, i, 0))
    return pl.pallas_call(
        _hy_inv_kernel,
        out_shape=jax.ShapeDtypeStruct((bsz, n, HY_WIDTH), BF16),
        grid=(n // tt, bsz),
        in_specs=[tab, tab, full, full, tile, tile, pl.BlockSpec((1, HY_WIDTH), lambda i, b: (0, 0))],
        out_specs=tile,
        compiler_params=_params(2),
        name="hy_inv",
    )(cttab, stntab, yre, yim, x0, w, bias)


LRU_SCAN_UNROLL = 4


def _lru_kernel(rxl_ref, rxc_ref, rgl_ref, rgc_ref, cw_ref, cb_ref, wa_ref, wx_ref, ba_ref, bx_ref, lam_ref,
                ol_ref, oc_ref, a_scr, b_scr, h_scr):
    cw = cw_ref[...]
    cb = cb_ref[...]
    row8 = lax.broadcasted_iota(I32, (SUBLANES, LANES), 0)

    def coeffs(x_ref, n):
        x = x_ref[0]
        xc = (cb + _shift_down(x, 2) * cw[0:1] + _shift_down(x, 1) * cw[1:2] + x * cw[2:3]
              + _shift_up(x, 1) * cw[3:4])
        xb = xc.astype(BF16)
        for d in range(2):
            r = _sigmoid(_dot(xb, wa_ref[d, 0]) + ba_ref[d])
            i = _sigmoid(_dot(xb, wx_ref[d, 0]) + bx_ref[d])
            nl = -lam_ref[d]
            softplus = jnp.maximum(nl, 0.0) + jnp.log(1.0 + jnp.exp(-jnp.abs(nl)))
            log_a = (-LRU_C) * r * softplus
            a = jnp.exp(log_a)
            a_scr[d, 0:n, :] = a
            b_scr[d, 0:n, :] = jnp.sqrt(1.0 - a * a) * i * xc

    def scan(n, carry_f, carry_b):
        ng = n // SUBLANES

        def body(g, carry):
            cf, cbk = carry
            of = pl.multiple_of(g * SUBLANES, SUBLANES)
            ob = pl.multiple_of((ng - 1 - g) * SUBLANES, SUBLANES)
            a = a_scr[0, pl.ds(of, SUBLANES), :]
            b = b_scr[0, pl.ds(of, SUBLANES), :]
            a2 = a_scr[1, pl.ds(ob, SUBLANES), :]
            b2 = b_scr[1, pl.ds(ob, SUBLANES), :]
            for k in (1, 2, 4):
                keep = row8 >= k
                b = a * jnp.where(keep, pltpu.roll(b, k, 0), 0.0) + b
                a = a * jnp.where(keep, pltpu.roll(a, k, 0), 1.0)
                keep2 = row8 < SUBLANES - k
                b2 = a2 * jnp.where(keep2, pltpu.roll(b2, SUBLANES - k, 0), 0.0) + b2
                a2 = a2 * jnp.where(keep2, pltpu.roll(a2, SUBLANES - k, 0), 1.0)
            hf = a * cf + b
            hb = a2 * cbk + b2
            h_scr[0, pl.ds(of, SUBLANES), :] = hf
            h_scr[1, pl.ds(ob, SUBLANES), :] = hb
            return hf[SUBLANES - 1:SUBLANES, :], hb[0:1, :]

        return lax.fori_loop(0, ng, body, (carry_f, carry_b), unroll=LRU_SCAN_UNROLL)

    n_ctx = rxc_ref.shape[1]
    n_lat = rxl_ref.shape[1]
    zero = jnp.zeros((1, LANES), F32)
    coeffs(rxc_ref, n_ctx)
    cf, cbk = scan(n_ctx, zero, zero)
    oc_ref[0] = ((h_scr[0, 0:n_ctx, :] + h_scr[1, 0:n_ctx, :]) * _gelu_tanh(rgc_ref[0].astype(F32))).astype(BF16)
    coeffs(rxl_ref, n_lat)
    scan(n_lat, cf, cbk)
    ol_ref[0] = ((h_scr[0] + h_scr[1]) * _gelu_tanh(rgl_ref[0].astype(F32))).astype(BF16)


def _lru_call(rx, rx_c, rg, rg_c, conv_w, conv_b, wa_bd, wx_bd, ba, bx, lam):
    bsz, t, _ = rx.shape
    tc = rx_c.shape[1]
    nch = LRU_WIDTH // LANES
    seq = lambda n: pl.BlockSpec((1, n, LANES), lambda b, j: (b, 0, j))
    vec = lambda r: pl.BlockSpec((r, 1, LANES), lambda b, j: (0, 0, j))
    wsp = pl.BlockSpec((2, 1, LANES, LANES), lambda b, j: (0, j, 0, 0))
    return pl.pallas_call(
        _lru_kernel,
        out_shape=[jax.ShapeDtypeStruct((bsz, t, LRU_WIDTH), BF16), jax.ShapeDtypeStruct((bsz, tc, LRU_WIDTH), BF16)],
        grid=(bsz, nch),
        in_specs=[seq(t), seq(tc), seq(t), seq(tc),
                  pl.BlockSpec((4, LANES), lambda b, j: (0, j)), pl.BlockSpec((1, LANES), lambda b, j: (0, j)),
                  wsp, wsp, vec(2), vec(2), vec(2)],
        out_specs=[seq(t), seq(tc)],
        scratch_shapes=[pltpu.VMEM((2, t, LANES), F32)] * 3,
        compiler_params=_params(2),
        name="lru",
    )(rx, rx_c, rg, rg_c, conv_w, conv_b, wa_bd, wx_bd, ba, bx, lam)


def _merge_kernel(attn_ref, hy_ref, lru_ref, gate_ref, h_ref, g1_ref, wa_ref, wh_ref, wl_ref, wo_ref,
                  n2_ref, sh2_ref, sc2_ref, rw_ref, hn_ref, u2_ref, lt_ref):
    def gate(j):
        return _sigmoid(gate_ref[0, :, j * D_MODEL:(j + 1) * D_MODEL].astype(F32))

    y = gate(0) * _dot(attn_ref[0], wa_ref[...])
    y = y + gate(1) * _dot(hy_ref[0], wh_ref[...])
    y = y + gate(2) * _dot(lru_ref[0], wl_ref[...])
    hn = h_ref[0] + g1_ref[0] * _dot(y.astype(BF16), wo_ref[...])
    hn_ref[0] = hn
    u2 = (_rms(hn) * n2_ref[...] * (1.0 + sc2_ref[0]) + sh2_ref[0]).astype(BF16)
    u2_ref[0] = u2
    lt_ref[0] = _dot_nt(rw_ref[...], u2)


def _merge_call(attn, hyo, lruo, gate, h, g1, wa, wh, wl, wo, n2g, sh2, sc2, rw_t, ctx_rows):
    bsz, t, d = h.shape
    tm = min(512, t)
    row = (lambda b: MOD_ROWS - 8) if ctx_rows else (lambda b: b)
    tok = lambda w: pl.BlockSpec((1, tm, w), lambda b, i: (b, i, 0))
    modspec = pl.BlockSpec((1, 1, d), lambda b, i: (row(b), 0, 0))
    const = lambda shape: pl.BlockSpec(shape, lambda b, i: (0,) * len(shape))
    return pl.pallas_call(
        _merge_kernel,
        out_shape=[jax.ShapeDtypeStruct((bsz, t, d), F32), jax.ShapeDtypeStruct((bsz, t, d), BF16),
                   jax.ShapeDtypeStruct((bsz, N_EXPERTS, t), F32)],
        grid=(bsz, t // tm),
        in_specs=[tok(Q_WIDTH), tok(HY_WIDTH), tok(LRU_WIDTH), tok(GATE_WIDTH), tok(d), modspec,
                  const((Q_WIDTH, d)), const((HY_WIDTH, d)), const((LRU_WIDTH, d)), const((d, d)),
                  const((1, d)), modspec, modspec, const((N_EXPERTS, d))],
        out_specs=[tok(d), tok(d), pl.BlockSpec((1, N_EXPERTS, tm), lambda b, i: (b, 0, i))],
        compiler_params=_params(2),
        name="merge",
    )(attn, hyo, lruo, gate, h, g1, wa, wh, wl, wo, n2g, sh2, sc2, rw_t)


def _router_kernel(lt_ref, slot_ref, slotc_ref, gc_ref, coff_ref, *, cap):
    lg = lt_ref[0]
    n_e, t = lg.shape
    ex = jnp.exp(lg - lg.max(axis=0, keepdims=True))
    aff = ex / ex.sum(axis=0, keepdims=True)
    key = pltpu.bitcast(aff, I32)
    capf = float(cap)

    def count(mask):
        return jnp.where(mask, 1.0, 0.0).sum(axis=1, keepdims=True)

    def vbody(i, thr):
        cand = thr | lax.shift_left(jnp.int32(1), 29 - i)
        return jnp.where(count(key >= cand) >= capf, cand, thr)

    thr = lax.fori_loop(0, 30, vbody, jnp.zeros((n_e, 1), I32))
    gt = key > thr
    eq = key == thr
    need = capf - count(gt)
    idx = lax.broadcasted_iota(I32, (n_e, t), 1)
    nbits = t.bit_length() - 1

    def ibody(i, lo):
        cand = lo | lax.shift_left(jnp.int32(1), nbits - 1 - i)
        return jnp.where(count(eq & (idx < cand)) < need, cand, lo)

    last = lax.fori_loop(0, nbits, ibody, jnp.zeros((n_e, 1), I32))
    sel = gt | (eq & (idx <= last))
    self32 = jnp.where(sel, 1.0, 0.0)
    gsel = jnp.where(sel, aff, 0.0)

    r_i = lax.broadcasted_iota(I32, (LANES, LANES), 0)
    c_i = lax.broadcasted_iota(I32, (LANES, LANES), 1)
    tri = jnp.where(r_i <= c_i, 1.0, 0.0).astype(BF16)
    eye = jnp.where(r_i == c_i, 1.0, 0.0).astype(BF16)
    off = jnp.zeros((n_e, 1), F32)
    n_units = t // LANES
    coff_ref[0] = jnp.zeros((n_e, LANES), I32)
    for c in range(n_units):
        sl = slice(c * LANES, (c + 1) * LANES)
        coff_ref[0, :, c:c + 1] = off.astype(I32)
        xs = self32[:, sl]
        inc = _dot(xs.astype(BF16), tri)
        slot1 = jnp.where(sel[:, sl], inc - xs + off + 1.0, 0.0)
        off = off + inc[:, LANES - 1:LANES]
        slot_ref[0, :, sl] = slot1.astype(I32) - 1
        hi = jnp.floor(slot1 * (1.0 / 16.0))
        lo = slot1 - 16.0 * hi
        col = 16.0 * _dot_nt(eye, hi.astype(BF16)) + _dot_nt(eye, lo.astype(BF16))
        slotc_ref[0, sl, :] = col.astype(I32) - 1
        g = gsel[:, sl]
        g1 = g.astype(BF16)
        r1 = g - g1.astype(F32)
        g2 = r1.astype(BF16)
        g3 = (r1 - g2.astype(F32)).astype(BF16)
        gc_ref[0, sl, :] = (_dot_nt(eye, g1) + _dot_nt(eye, g2)) + _dot_nt(eye, g3)
    coff_ref[0, :, n_units:n_units + 1] = off.astype(I32)


def _router_call(logits_t, cap):
    bsz, n_e, t = logits_t.shape
    assert t % LANES == 0 and t // LANES < LANES
    slot_row, slot_col, g_col, coff = pl.pallas_call(
        functools.partial(_router_kernel, cap=cap),
        out_shape=[jax.ShapeDtypeStruct((bsz, n_e, t), I32), jax.ShapeDtypeStruct((bsz, t, n_e), I32),
                   jax.ShapeDtypeStruct((bsz, t, n_e), F32), jax.ShapeDtypeStruct((bsz, n_e, LANES), I32)],
        grid=(bsz,),
        in_specs=[pl.BlockSpec((1, n_e, t), lambda b: (b, 0, 0))],
        out_specs=[pl.BlockSpec((1, n_e, t), lambda b: (b, 0, 0)), pl.BlockSpec((1, t, n_e), lambda b: (b, 0, 0)),
                   pl.BlockSpec((1, t, n_e), lambda b: (b, 0, 0)), pl.BlockSpec((1, n_e, LANES), lambda b: (b, 0, 0))],
        compiler_params=_params(1),
        name="router",
    )(logits_t)
    return slot_row, slot_col, g_col, coff[:, :, :t // LANES + 1].reshape(-1)


MOE_GATHER_TOKENS = 256
MOE_SLOT_ROWS = 128
MOE_SCATTER_SLOTS = 256


def _log2(n):
    assert n & (n - 1) == 0
    return n.bit_length() - 1


def _expert_kernel(coff_ref, u_ref, slot_ref, w1_ref, w3_ref, w2_ref, y_ref, xg_scr, *, tkg, sb, n_units):
    cap = xg_scr.shape[0]
    n_chunks = u_ref.shape[1] // tkg
    base = (pl.program_id(0) * pl.num_programs(1) + pl.program_id(1)) * (n_units + 1)
    upc = tkg // LANES
    rid = lax.broadcasted_iota(I32, (sb, tkg), 0)
    xg_scr[...] = jnp.zeros(xg_scr.shape, F32)

    def window_start(c):
        lo = coff_ref[base + c * upc]
        return jnp.minimum(lo & (-SUBLANES), cap - sb)

    for c in range(n_chunks):
        r0 = pl.multiple_of(window_start(c), SUBLANES)
        onehot = jnp.where(slot_ref[0, 0, c:c + 1, :] - r0 == rid, 1.0, 0.0).astype(BF16)
        xg_scr[pl.ds(r0, sb), :] += _dot(onehot, u_ref[0, c * tkg:(c + 1) * tkg, :])

    def overflow(c, carry):
        covered = window_start(c) + sb
        hi = coff_ref[base + (c + 1) * upc]
        first = lax.shift_right_logical(covered, _log2(sb))
        stop = jnp.where(hi > covered, lax.shift_right_logical(hi + (sb - 1), _log2(sb)), first)

        def block(j, carry2):
            b0 = pl.multiple_of(j * sb, sb)
            srow = slot_ref[0, 0, pl.ds(c, 1), :]
            onehot = jnp.where((srow - b0 == rid) & (srow >= covered), 1.0, 0.0).astype(BF16)
            xg_scr[pl.ds(b0, sb), :] += _dot(onehot, u_ref[0, pl.ds(pl.multiple_of(c * tkg, tkg), tkg), :])
            return carry2

        return lax.fori_loop(first, stop, block, carry)

    lax.fori_loop(0, n_chunks, overflow, 0)
    xb = xg_scr[...].astype(BF16)
    hid = (_silu(_dot(xb, w1_ref[0])) * _dot(xb, w3_ref[0])).astype(BF16)
    y_ref[0, 0] = _dot(hid, w2_ref[0]).astype(BF16)


def _expert_call(u2, slot_row, coff, w1, w3, w2, cap):
    bsz, t, d = u2.shape
    n_e = w1.shape[0]
    tkg = min(MOE_GATHER_TOKENS, t)
    sb = min(MOE_SLOT_ROWS, cap)
    wspec = pl.BlockSpec((1, d, d), lambda b, e, co: (e, 0, 0))
    grid_spec = pltpu.PrefetchScalarGridSpec(
        num_scalar_prefetch=1,
        grid=(bsz, n_e),
        in_specs=[pl.BlockSpec((1, t, d), lambda b, e, co: (b, 0, 0)),
                  pl.BlockSpec((1, 1, t // tkg, tkg), lambda b, e, co: (b, e, 0, 0)), wspec, wspec, wspec],
        out_specs=pl.BlockSpec((1, 1, cap, d), lambda b, e, co: (b, e, 0, 0)),
        scratch_shapes=[pltpu.VMEM((cap, d), F32)],
    )
    return pl.pallas_call(
        functools.partial(_expert_kernel, tkg=tkg, sb=sb, n_units=t // LANES),
        out_shape=jax.ShapeDtypeStruct((bsz, n_e, cap, d), BF16),
        grid_spec=grid_spec,
        compiler_params=_params(2),
        name="expert",
    )(coff, u2, slot_row.reshape(bsz, n_e, t // tkg, tkg), w1, w3, w2)


def _scatter_kernel(coff_ref, h_ref, y_ref, slotc_ref, gc_ref, g2_ref, o_ref, acc_scr, *, kb, n_units):
    tk = h_ref.shape[1]
    cap = y_ref.shape[2]
    upt = tk // LANES
    i = pl.program_id(1)
    lane = lax.broadcasted_iota(I32, (tk, kb), 1)
    bf16_rows = 2 * SUBLANES

    def span(e):
        base = (pl.program_id(0) * N_EXPERTS + e) * (n_units + 1)
        lo = coff_ref[base + i * upt]
        hi = coff_ref[base + (i + 1) * upt]
        return jnp.minimum(lo & (-bf16_rows), cap - kb), hi

    acc = jnp.zeros((tk, D_MODEL), F32)
    for e in range(N_EXPERTS):
        r0 = pl.multiple_of(span(e)[0], bf16_rows)
        onehot = jnp.where(slotc_ref[0, :, e:e + 1] - r0 == lane, 1.0, 0.0).astype(BF16)
        acc = acc + gc_ref[0, :, e:e + 1] * _dot(onehot, y_ref[0, e, pl.ds(r0, kb), :])
    acc_scr[...] = acc

    for e in range(N_EXPERTS):
        r0, hi = span(e)
        covered = r0 + kb
        first = lax.shift_right_logical(covered, _log2(kb))
        stop = jnp.where(hi > covered, lax.shift_right_logical(hi + (kb - 1), _log2(kb)), first)

        def block(j, carry, e=e, covered=covered):
            b0 = pl.multiple_of(j * kb, kb)
            scol = slotc_ref[0, :, e:e + 1]
            onehot = jnp.where((scol - b0 == lane) & (scol >= covered), 1.0, 0.0).astype(BF16)
            acc_scr[...] += gc_ref[0, :, e:e + 1] * _dot(onehot, y_ref[0, e, pl.ds(b0, kb), :])
            return carry

        lax.fori_loop(first, stop, block, 0)
    o_ref[0] = h_ref[0] + g2_ref[0] * acc_scr[...]


def _scatter_call(h, y, slot_col, g_col, coff, g2, cap, ctx_rows):
    bsz, t, d = h.shape
    n_e = y.shape[1]
    tk = min(512, t)
    kb = min(MOE_SCATTER_SLOTS, cap)
    row = (lambda b: MOD_ROWS - 8) if ctx_rows else (lambda b: b)
    tile = lambda w: pl.BlockSpec((1, tk, w), lambda b, i, co: (b, i, 0))
    grid_spec = pltpu.PrefetchScalarGridSpec(
        num_scalar_prefetch=1,
        grid=(bsz, t // tk),
        in_specs=[tile(d),
                  pl.BlockSpec((1, n_e, cap, d), lambda b, i, co: (b, 0, 0, 0), pipeline_mode=pl.Buffered(1)),
                  tile(n_e), tile(n_e), pl.BlockSpec((1, 1, d), lambda b, i, co: (row(b), 0, 0))],
        out_specs=tile(d),
        scratch_shapes=[pltpu.VMEM((tk, d), F32)],
    )
    return pl.pallas_call(
        functools.partial(_scatter_kernel, kb=kb, n_units=t // LANES),
        out_shape=jax.ShapeDtypeStruct((bsz, t, d), F32),
        grid_spec=grid_spec,
        compiler_params=_params(2),
        name="moe_scatter",
    )(coff, h, y, slot_col, g_col, g2)


def _final_norm_kernel(h_ref, g_ref, o_ref):
    o_ref[0] = _rms(h_ref[0]) * g_ref[...]


def _final_norm_call(h, g):
    bsz, t, d = h.shape
    tm = min(512, t)
    tile = pl.BlockSpec((1, tm, d), lambda b, i: (b, i, 0))
    return pl.pallas_call(
        _final_norm_kernel,
        out_shape=jax.ShapeDtypeStruct((bsz, t, d), F32),
        grid=(bsz, t // tm),
        in_specs=[tile, pl.BlockSpec((1, d), lambda b, i: (0, 0))],
        out_specs=tile,
        compiler_params=_params(2),
        name="final_norm",
    )(h, g)


def _rope_tables(n_lat, n_ctx):
    rows = n_lat // GRID_W
    row = jnp.repeat(jnp.arange(rows, dtype=F32), GRID_W)
    col = jnp.tile(jnp.arange(GRID_W, dtype=F32), rows)
    inv = jnp.power(ROPE_THETA, -jnp.arange(ROPE_PAIRS_AXIS, dtype=F32) / ROPE_PAIRS_AXIS)
    ang = jnp.concatenate([row[:, None] * inv, col[:, None] * inv], axis=-1)
    cos, sin = jnp.cos(ang), jnp.sin(ang)
    cs = jnp.concatenate([cos, cos], axis=-1)
    sn = jnp.concatenate([-sin, sin], axis=-1)
    return cs, sn, jnp.ones((n_ctx, HEAD_DIM), F32), jnp.zeros((n_ctx, HEAD_DIM), F32)


def _dft_tables(n):
    k = jnp.arange(n, dtype=I32)
    m = ((2 * k[:, None] + 1) * k[None, :]) % (4 * n)
    ang = m.astype(F32) * (2.0 * math.pi / (4 * n))
    c = jnp.cos(ang)
    s = jnp.sin(ang)
    return c.astype(BF16), s.astype(BF16), c.T.astype(BF16), (-s.T).astype(BF16)


def _filter_features(n):
    t = jnp.linspace(0.0, 1.0, n, dtype=F32)[:, None]
    w = (2.0 * math.pi / n) * jnp.arange(n, dtype=F32)[:, None]
    f = jnp.linspace(1e-4, HY_BANDS - 1, HY_BANDS, dtype=F32)[None, :]
    feats = jnp.concatenate([t, jnp.cos(f * w), -jnp.sin(f * w)], axis=-1)
    return jnp.pad(feats, ((0, 0), (0, LANES - HY_EMB)))


def _pad_to(x, shape):
    return jnp.pad(x, [(0, s - d) for d, s in zip(x.shape, shape)])


def _block_diag_chunks(w):
    per = LANES // LRU_BLOCK
    w = w.reshape(2, LRU_BLOCKS // per, per, LRU_BLOCK, LRU_BLOCK)
    eye = jnp.eye(per, dtype=w.dtype)
    return jnp.einsum('dcpkj,pq->dcpkqj', w, eye).reshape(2, LRU_BLOCKS // per, LANES, LANES)


def kernel(x, c, ctx, c_ctx, mod_w, mod_b, norm1_g, norm2_g, w_in, q_norm_g, k_norm_g, hy_conv_w, hy_conv_b,
           hy_fw1, hy_fb1, hy_fw2, hy_fb2, hy_fw3, hy_freq, hy_bias, lru_conv_w, lru_conv_b, lru_wa, lru_ba,
           lru_wx, lru_bx, lru_lambda, w_attn_out, w_hy_out, w_lru_out, w_out, router_w, exp_w1, exp_w3, exp_w2,
           final_norm_g):
    bsz, n_lat, d = x.shape
    n_ctx = ctx.shape[1]
    depth = mod_w.shape[0]
    assert d == D_MODEL and bsz <= MOD_ROWS - 8 and n_lat % GRID_W == 0

    cs_l, sn_l, cs_c, sn_c = _rope_tables(n_lat, n_ctx)
    dft_l = _dft_tables(n_lat)
    dft_c = _dft_tables(n_ctx)
    feats_l = _filter_features(n_lat)
    feats_c = _filter_features(n_ctx)
    max_decay = math.log(HY_DECAY_TARGET) / HY_FAST_DECAY_PCT
    min_decay = math.log(HY_DECAY_TARGET) / HY_SLOW_DECAY_PCT
    deltas_abs = jnp.abs(jnp.linspace(min_decay, max_decay, HY_WIDTH, dtype=F32))[None, :]
    cap_l = max(1, EC_CAPACITY * n_lat // N_EXPERTS)
    cap_c = max(1, EC_CAPACITY * n_ctx // N_EXPERTS)

    cvec = jnp.zeros((MOD_ROWS, d), F32).at[:bsz].set(c).at[MOD_ROWS - 8].set(c_ctx)
    mod = _mod_call(cvec, mod_w, mod_b)
    mod = mod.reshape(depth, MOD_ROWS, 6, 1, d).transpose(0, 2, 1, 3, 4)

    h, hc = x, ctx
    for l in range(depth):
        last = l == depth - 1
        sh1, sc1, g1, sh2, sc2, g2 = (mod[l, j] for j in range(6))
        w_in_b = w_in[l].astype(BF16)
        n1 = norm1_g[l][None, :]
        n2 = norm2_g[l][None, :]
        qg = q_norm_g[l][None, :]
        kg = k_norm_g[l][None, :]

        q, k, v, hy, rx, rg, gate = _inproj_call(h, sh1, sc1, n1, w_in_b, cs_l, sn_l, qg, kg, False)
        qc, kc, vc, hyc, rxc, rgc, gatec = _inproj_call(hc, sh1, sc1, n1, w_in_b, cs_c, sn_c, qg, kg, True)

        attn = _attn_call(q, [(k, v), (kc, vc)])

        fw1 = _pad_to(hy_fw1[l], (LANES, LANES))
        fb1 = _pad_to(hy_fb1[l][None, :], (1, LANES))
        fw2 = _pad_to(hy_fw2[l], (LANES, LANES))
        fb2 = _pad_to(hy_fb2[l][None, :], (1, LANES))
        fw3 = _pad_to(hy_fw3[l], (LANES, 2 * HY_WIDTH))
        freq = _pad_to(hy_freq[l], (2, LANES))
        hbias = hy_bias[l][None, :]

        def hyena(hy_in, feats, tabs):
            ctab, stab, cttab, stntab = tabs
            hsd = _hy_filter_call(feats, fw1, fb1, fw2, fb2, fw3, freq, deltas_abs)
            kre, kim = _hy_kspec_call(ctab, stab, hsd)
            x0, w, wb = _hy_pre_call(hy_in, hy_conv_w[l], hy_conv_b[l][None, :])
            yre, yim = _hy_fwd_call(ctab, stab, wb, kre, kim)
            return _hy_inv_call(cttab, stntab, yre, yim, x0, w, hbias)

        hyo = hyena(hy, feats_l, dft_l)

        lruo, lruoc = _lru_call(rx, rxc, rg, rgc, lru_conv_w[l], lru_conv_b[l][None, :],
                                _block_diag_chunks(lru_wa[l]).astype(BF16), _block_diag_chunks(lru_wx[l]).astype(BF16),
                                lru_ba[l][:, None, :], lru_bx[l][:, None, :], lru_lambda[l][:, None, :])

        wa = w_attn_out[l].astype(BF16)
        wh = w_hy_out[l].astype(BF16)
        wl = w_lru_out[l].astype(BF16)
        wo = w_out[l].astype(BF16)
        rw_t = router_w[l].T.astype(BF16)
        e1 = exp_w1[l].astype(BF16)
        e3 = exp_w3[l].astype(BF16)
        e2 = exp_w2[l].astype(BF16)

        def channel_mix(h_in, attn_in, hy_in, lru_in, gate_in, cap, ctx_rows):
            hn, u2, lt = _merge_call(attn_in, hy_in, lru_in, gate_in, h_in, g1, wa, wh, wl, wo, n2, sh2, sc2, rw_t,
                                     ctx_rows)
            slot_row, slot_col, g_col, coff = _router_call(lt, cap)
            y = _expert_call(u2, slot_row, coff, e1, e3, e2, cap)
            return _scatter_call(hn, y, slot_col, g_col, coff, g2, cap, ctx_rows)

        h = channel_mix(h, attn, hyo, lruo, gate, cap_l, False)
        if not last:
            attn_c = _attn_call(qc, [(kc, vc)])
            hyo_c = hyena(hyc, feats_c, dft_c)
            hc = channel_mix(hc, attn_c, hyo_c, lruoc, gatec, cap_c, True)

    return _final_norm_call(h, final_norm_g[None, :])
```

```python
import functools
import math

import jax
import jax.numpy as jnp
from jax import lax
from jax.experimental import pallas as pl
from jax.experimental.pallas import tpu as pltpu

F32 = jnp.float32
BF16 = jnp.bfloat16
I32 = jnp.int32

D_MODEL = 1024
GRID_W = 64
NORM_EPS = 1e-6
N_HEADS = 8
N_KV_HEADS = 2
HEAD_DIM = 128
GROUP = N_HEADS // N_KV_HEADS
ROPE_PAIRS_AXIS = HEAD_DIM // 4
ROPE_THETA = 10000.0
ATTN_SCALE = HEAD_DIM ** -0.5
Q_PRESCALE = ATTN_SCALE * math.log2(math.e)
Q_WIDTH = N_HEADS * HEAD_DIM
KV_WIDTH = N_KV_HEADS * HEAD_DIM
HY_WIDTH = D_MODEL // 2
HY_BANDS = 16
HY_EMB = 1 + 2 * HY_BANDS
HY_FILTER_HIDDEN = 64
HY_FAST_DECAY_PCT = 0.3
HY_SLOW_DECAY_PCT = 1.5
HY_DECAY_TARGET = 1e-2
LRU_WIDTH = D_MODEL // 2
LRU_BLOCKS = 8
LRU_BLOCK = LRU_WIDTH // LRU_BLOCKS
LRU_C = 8.0
N_BRANCH = 3
N_EXPERTS = 16
EC_CAPACITY = 2
GATE_WIDTH = N_BRANCH * D_MODEL
OFF_Q = 0
OFF_K = OFF_Q + Q_WIDTH
OFF_V = OFF_K + KV_WIDTH
OFF_HY = OFF_V + KV_WIDTH
OFF_RX = OFF_HY + 3 * HY_WIDTH
OFF_RG = OFF_RX + LRU_WIDTH
OFF_GATE = OFF_RG + LRU_WIDTH
IN_TOTAL = OFF_GATE + GATE_WIDTH

LANES = 128
SUBLANES = 8
V7X_VMEM_LIMIT_BYTES = 56 * 1024 * 1024
MOD_ROWS = 24


def _params(n_axes, vmem=V7X_VMEM_LIMIT_BYTES):
    return pltpu.CompilerParams(dimension_semantics=("arbitrary",) * n_axes, vmem_limit_bytes=vmem)


def _dot(a, b):
    return jnp.dot(a, b, preferred_element_type=F32)


def _dot_nt(a, b):
    return lax.dot_general(a, b, (((1,), (1,)), ((), ())), preferred_element_type=F32)


def _split2(x):
    hi = x.astype(BF16)
    lo = (x - hi.astype(F32)).astype(BF16)
    return hi, lo


def _dot3(a, b):
    ah, al = _split2(a)
    bh, bl = _split2(b)
    return _dot(ah, bh) + (_dot(ah, bl) + _dot(al, bh))


def _rms(x):
    return x * lax.rsqrt(jnp.mean(x * x, axis=-1, keepdims=True) + NORM_EPS)


def _sigmoid(x):
    return 1.0 / (1.0 + jnp.exp(-x))


def _silu(x):
    return x * _sigmoid(x)


def _gelu_tanh(x):
    return 0.5 * x * (1.0 + jnp.tanh(math.sqrt(2.0 / math.pi) * (x + 0.044715 * (x * x * x))))


def _shift_down(x, k):
    row = lax.broadcasted_iota(I32, x.shape, 0)
    return jnp.where(row >= k, pltpu.roll(x, k, 0), 0.0)


def _shift_up(x, k):
    n = x.shape[0]
    row = lax.broadcasted_iota(I32, x.shape, 0)
    return jnp.where(row < n - k, pltpu.roll(x, n - k, 0), 0.0)


def _mod_kernel(c_ref, w_ref, b_ref, o_ref):
    o_ref[0] = _dot3(_silu(c_ref[...]), w_ref[0]) + b_ref[0]


def _mod_call(cvec, mod_w, mod_b):
    depth, d, six_d = mod_w.shape
    tn = 1536
    return pl.pallas_call(
        _mod_kernel,
        out_shape=jax.ShapeDtypeStruct((depth, MOD_ROWS, six_d), F32),
        grid=(depth, six_d // tn),
        in_specs=[
            pl.BlockSpec((MOD_ROWS, d), lambda l, j: (0, 0)),
            pl.BlockSpec((1, d, tn), lambda l, j: (l, 0, j)),
            pl.BlockSpec((1, 1, tn), lambda l, j: (l, 0, j)),
        ],
        out_specs=pl.BlockSpec((1, MOD_ROWS, tn), lambda l, j: (l, 0, j)),
        compiler_params=_params(2),
        name="mod",
    )(cvec, mod_w, mod_b.reshape(depth, 1, six_d))


def _inproj_kernel(h_ref, sh_ref, sc_ref, g_ref, w_ref, cs_ref, sn_ref, qg_ref, kg_ref,
                   q_ref, k_ref, v_ref, hy_ref, rx_ref, rg_ref, gate_ref):
    u = (_rms(h_ref[0]) * g_ref[...] * (1.0 + sc_ref[0]) + sh_ref[0]).astype(BF16)
    cs = cs_ref[...]
    sn = sn_ref[...]

    def normed_rope(z, g):
        r = _rms(z) * g
        return r * cs + pltpu.roll(r, HEAD_DIM // 2, 1) * sn

    zq = _dot(u, w_ref[:, OFF_Q:OFF_Q + Q_WIDTH])
    for hd in range(N_HEADS):
        sl = slice(hd * HEAD_DIM, (hd + 1) * HEAD_DIM)
        q_ref[0, hd] = (normed_rope(zq[:, sl], qg_ref[...]) * Q_PRESCALE).astype(BF16)
    zkv = _dot(u, w_ref[:, OFF_K:OFF_K + 2 * KV_WIDTH])
    for hd in range(N_KV_HEADS):
        sl = slice(hd * HEAD_DIM, (hd + 1) * HEAD_DIM)
        k_ref[0, :, sl] = normed_rope(zkv[:, sl], kg_ref[...]).astype(BF16)
    v_ref[0] = zkv[:, KV_WIDTH:].astype(BF16)
    for j in range(3):
        sl = slice(j * HY_WIDTH, (j + 1) * HY_WIDTH)
        hy_ref[0, :, sl] = _dot(u, w_ref[:, OFF_HY + j * HY_WIDTH:OFF_HY + (j + 1) * HY_WIDTH])
    rr = _dot(u, w_ref[:, OFF_RX:OFF_RX + 2 * LRU_WIDTH])
    rx_ref[0] = rr[:, :LRU_WIDTH]
    rg_ref[0] = rr[:, LRU_WIDTH:].astype(BF16)
    for j in range(N_BRANCH):
        sl = slice(j * D_MODEL, (j + 1) * D_MODEL)
        gate_ref[0, :, sl] = _dot(u, w_ref[:, OFF_GATE + j * D_MODEL:OFF_GATE + (j + 1) * D_MODEL]).astype(BF16)


def _inproj_call(h, sh, sc, g, w_bf, cs, sn, qg, kg, ctx_rows):
    bsz, t, d = h.shape
    tm = min(512, t)
    row = (lambda b: MOD_ROWS - 8) if ctx_rows else (lambda b: b)
    tok = lambda w: pl.BlockSpec((1, tm, w), lambda b, i: (b, i, 0))
    modspec = pl.BlockSpec((1, 1, d), lambda b, i: (row(b), 0, 0))
    const = lambda shape: pl.BlockSpec(shape, lambda b, i: (0,) * len(shape))
    outs = [(KV_WIDTH, BF16), (KV_WIDTH, BF16), (3 * HY_WIDTH, F32), (LRU_WIDTH, F32), (LRU_WIDTH, BF16),
            (GATE_WIDTH, BF16)]
    return pl.pallas_call(
        _inproj_kernel,
        out_shape=[jax.ShapeDtypeStruct((bsz, N_HEADS, t, HEAD_DIM), BF16)]
        + [jax.ShapeDtypeStruct((bsz, t, w), dt) for w, dt in outs],
        grid=(bsz, t // tm),
        in_specs=[
            tok(d), modspec, modspec, const((1, d)),
            pl.BlockSpec((d, IN_TOTAL), lambda b, i: (0, 0), pipeline_mode=pl.Buffered(1)),
            pl.BlockSpec((tm, HEAD_DIM), lambda b, i: (i, 0)),
            pl.BlockSpec((tm, HEAD_DIM), lambda b, i: (i, 0)),
            const((1, HEAD_DIM)), const((1, HEAD_DIM)),
        ],
        out_specs=[pl.BlockSpec((1, N_HEADS, tm, HEAD_DIM), lambda b, i: (b, 0, i, 0))] + [tok(w) for w, _ in outs],
        compiler_params=_params(2),
        name="inproj",
    )(h, sh, sc, g, w_bf, cs, sn, qg, kg)


ATTN_TQ = 512
ATTN_TK = 512
ATTN_STREAMS = 4
NEG_BIG = -1e30


def _attn_kernel(*refs, n_src, tq):
    q_ref = refs[0]
    kv = refs[1:1 + 2 * n_src]
    o_ref, kcat, vext = refs[1 + 2 * n_src:]
    n_keys = kcat.shape[0]

    @pl.when(pl.program_id(2) == 0)
    def _():
        off = 0
        for i in range(n_src):
            n = kv[2 * i].shape[1]
            kcat[off:off + n, :] = kv[2 * i][0]
            vext[off:off + n, 0:HEAD_DIM] = kv[2 * i + 1][0]
            off += n
        vext[:, HEAD_DIM:] = jnp.ones((n_keys, HEAD_DIM), BF16)

    rows = GROUP * tq
    q = q_ref[0].reshape(rows, HEAD_DIM)
    per = rows // ATTN_STREAMS
    qs = [q[i * per:(i + 1) * per] for i in range(ATTN_STREAMS)]
    ms = [jnp.full((per, 1), NEG_BIG, F32) for _ in qs]
    accs = [jnp.zeros((per, 2 * HEAD_DIM), F32) for _ in qs]
    for off in range(0, n_keys, ATTN_TK):
        size = min(ATTN_TK, n_keys - off)
        kk = kcat[off:off + size, :]
        vv = vext[off:off + size, :]
        for i in range(ATTN_STREAMS):
            s = _dot_nt(qs[i], kk)
            m_new = jnp.maximum(ms[i], s.max(axis=-1, keepdims=True))
            p = jnp.exp2(s - m_new).astype(BF16)
            accs[i] = jnp.exp2(ms[i] - m_new) * accs[i] + _dot(p, vv)
            ms[i] = m_new
    out = jnp.concatenate([a[:, :HEAD_DIM] / a[:, HEAD_DIM:] for a in accs], axis=0).astype(BF16)
    for g in range(GROUP):
        o_ref[0, :, g * HEAD_DIM:(g + 1) * HEAD_DIM] = out[g * tq:(g + 1) * tq]


def _attn_call(q, kvs):
    bsz, _, tq_all, _ = q.shape
    tq = min(ATTN_TQ, tq_all)
    gw = GROUP * HEAD_DIM
    in_specs = [pl.BlockSpec((1, GROUP, tq, HEAD_DIM), lambda b, h, i: (b, h, i, 0))]
    args = [q]
    n_keys = 0
    for k, v in kvs:
        tk = k.shape[1]
        n_keys += tk
        spec = pl.BlockSpec((1, tk, HEAD_DIM), lambda b, h, i: (b, 0, h))
        in_specs += [spec, spec]
        args += [k, v]
    return pl.pallas_call(
        functools.partial(_attn_kernel, n_src=len(kvs), tq=tq),
        out_shape=jax.ShapeDtypeStruct((bsz, tq_all, Q_WIDTH), BF16),
        grid=(bsz, N_KV_HEADS, tq_all // tq),
        in_specs=in_specs,
        out_specs=pl.BlockSpec((1, tq, gw), lambda b, h, i: (b, i, h)),
        scratch_shapes=[pltpu.VMEM((n_keys, HEAD_DIM), BF16), pltpu.VMEM((n_keys, 2 * HEAD_DIM), BF16)],
        compiler_params=_params(3),
        name="attn",
    )(*args)


def _hy_pre_kernel(h0_ref, h1_ref, h2_ref, w0_ref, w1_ref, w2_ref, b0_ref, b1_ref, b2_ref,
                   x0_ref, w_ref, we_ref, wo_ref):
    def conv(x_ref, cw_ref, cb_ref):
        x = x_ref[0]
        cw = cw_ref[...]
        return cb_ref[...] + _shift_down(x, 1) * cw[0:1] + x * cw[1:2] + _shift_up(x, 1) * cw[2:3]

    x0_ref[0] = conv(h0_ref, w0_ref, b0_ref)
    w = conv(h2_ref, w2_ref, b2_ref) * conv(h1_ref, w1_ref, b1_ref)
    w_ref[0] = w
    half = w.shape[0] // 2
    we_ref[0] = w_ref[0, pl.ds(0, half, stride=2), :].astype(BF16)
    wo_ref[0] = w_ref[0, pl.ds(1, half, stride=2), :].astype(BF16)


def _hy_pre_call(hy, conv_w, conv_b):
    bsz, t, _ = hy.shape
    nch = HY_WIDTH // LANES
    xs = lambda part: pl.BlockSpec((1, t, LANES), lambda b, j: (b, 0, part * nch + j))
    ws = lambda part: pl.BlockSpec((3, LANES), lambda b, j: (0, part * nch + j))
    bs = lambda part: pl.BlockSpec((1, LANES), lambda b, j: (0, part * nch + j))
    out = pl.BlockSpec((1, t, LANES), lambda b, j: (b, 0, j))
    outh = pl.BlockSpec((1, t // 2, LANES), lambda b, j: (b, 0, j))
    return pl.pallas_call(
        _hy_pre_kernel,
        out_shape=[jax.ShapeDtypeStruct((bsz, t, HY_WIDTH), F32), jax.ShapeDtypeStruct((bsz, t, HY_WIDTH), F32),
                   jax.ShapeDtypeStruct((bsz, t // 2, HY_WIDTH), BF16),
                   jax.ShapeDtypeStruct((bsz, t // 2, HY_WIDTH), BF16)],
        grid=(bsz, nch),
        in_specs=[xs(0), xs(1), xs(2), ws(0), ws(1), ws(2), bs(0), bs(1), bs(2)],
        out_specs=[out, out, outh, outh],
        compiler_params=_params(2),
        name="hy_pre",
    )(hy, hy, hy, conv_w, conv_w, conv_w, conv_b, conv_b, conv_b)


def _hy_filter_kernel(feat_ref, w1_ref, b1_ref, w2_ref, b2_ref, w3_ref, fr_ref, dl_ref, o_ref):
    feats = feat_ref[...]
    hid = jnp.sin(fr_ref[0:1] * (_dot3(feats, w1_ref[...]) + b1_ref[...]))
    hid = jnp.sin(fr_ref[1:2] * (_dot3(hid, w2_ref[...]) + b2_ref[...]))
    filt = _dot3(hid, w3_ref[...])
    decay = jnp.exp(-feats[:, 0:1] * dl_ref[...])
    h_fwd = filt[:, :HY_WIDTH] * decay
    h_bwd = filt[:, HY_WIDTH:] * decay
    o_ref[:, :HY_WIDTH] = h_fwd + h_bwd
    o_ref[:, HY_WIDTH:] = h_bwd - h_fwd


def _hy_filter_call(feats, fw1, fb1, fw2, fb2, fw3, freq, deltas_abs):
    n = feats.shape[0]
    tn = min(512, n)
    const = lambda shape: pl.BlockSpec(shape, lambda i: (0,) * len(shape))
    return pl.pallas_call(
        _hy_filter_kernel,
        out_shape=jax.ShapeDtypeStruct((n, 2 * HY_WIDTH), F32),
        grid=(n // tn,),
        in_specs=[pl.BlockSpec((tn, LANES), lambda i: (i, 0)), const((LANES, LANES)), const((1, LANES)),
                  const((LANES, LANES)), const((1, LANES)), const((LANES, 2 * HY_WIDTH)), const((2, LANES)),
                  const((1, HY_WIDTH))],
        out_specs=pl.BlockSpec((tn, 2 * HY_WIDTH), lambda i: (i, 0)),
        compiler_params=_params(1),
        name="hy_filter",
    )(feats, fw1, fb1, fw2, fb2, fw3, freq, deltas_abs)


def _hy_kspec_kernel(ce_ref, co_ref, se_ref, so_ref, he_ref, ho_ref, krl_ref, kil_ref, krh_ref, kih_ref, *, scale):
    def branch(tab_e, tab_o, cols):
        he_hi, he_lo = _split2(he_ref[:, cols])
        ho_hi, ho_lo = _split2(ho_ref[:, cols])
        pe = _dot(tab_e[...], he_hi) + _dot(tab_e[...], he_lo)
        po = _dot(tab_o[...], ho_hi) + _dot(tab_o[...], ho_lo)
        return pe, po

    pc, qc = branch(ce_ref, co_ref, slice(0, HY_WIDTH))
    ps, qs = branch(se_ref, so_ref, slice(HY_WIDTH, 2 * HY_WIDTH))
    krl_ref[...] = (pc + qc) * scale
    krh_ref[...] = (pc - qc) * scale
    kil_ref[...] = (ps + qs) * scale
    kih_ref[...] = (qs - ps) * scale


def _hy_kspec_call(tabs, hsd):
    n = hsd.shape[0]
    half = n // 2
    tf = min(512, half)
    tab = pl.BlockSpec((tf, half), lambda i: (i, 0))
    out = pl.BlockSpec((tf, HY_WIDTH), lambda i: (i, 0))
    return pl.pallas_call(
        functools.partial(_hy_kspec_kernel, scale=1.0 / n),
        out_shape=[jax.ShapeDtypeStruct((half, HY_WIDTH), F32)] * 4,
        grid=(half // tf,),
        in_specs=[tab, tab, tab, tab, pl.BlockSpec((half, 2 * HY_WIDTH), lambda i: (0, 0)),
                  pl.BlockSpec((half, 2 * HY_WIDTH), lambda i: (0, 0))],
        out_specs=[out] * 4,
        compiler_params=_params(1),
        name="hy_kspec",
    )(*tabs, hsd[0::2], hsd[1::2])


def _hy_fwd_kernel(ce_ref, co_ref, se_ref, so_ref, we_ref, wo_ref, krl_ref, kil_ref, krh_ref, kih_ref,
                   urp_ref, urm_ref, uim_ref, uip_ref):
    we = we_ref[0]
    wo = wo_ref[0]
    pc = _dot(ce_ref[...], we)
    qc = _dot(co_ref[...], wo)
    ps = _dot(se_ref[...], we)
    qs = _dot(so_ref[...], wo)
    a_lo, a_hi = pc + qc, pc - qc
    b_lo, b_hi = ps + qs, qs - ps
    krl, kil, krh, kih = krl_ref[...], kil_ref[...], krh_ref[...], kih_ref[...]
    yre_lo = a_lo * krl + b_lo * kil
    yim_lo = a_lo * kil - b_lo * krl
    yre_hi = a_hi * krh + b_hi * kih
    yim_hi = a_hi * kih - b_hi * krh
    urp_ref[0] = (yre_lo + yre_hi).astype(BF16)
    urm_ref[0] = (yre_lo - yre_hi).astype(BF16)
    uim_ref[0] = (yim_lo - yim_hi).astype(BF16)
    uip_ref[0] = (yim_lo + yim_hi).astype(BF16)


def _hy_fwd_call(tabs, we, wo, kspec):
    bsz, half, _ = we.shape
    tf = min(512, half)
    tab = pl.BlockSpec((tf, half), lambda i, b: (i, 0))
    sig = pl.BlockSpec((1, half, HY_WIDTH), lambda i, b: (b, 0, 0))
    kk = pl.BlockSpec((tf, HY_WIDTH), lambda i, b: (i, 0))
    out = pl.BlockSpec((1, tf, HY_WIDTH), lambda i, b: (b, i, 0))
    return pl.pallas_call(
        _hy_fwd_kernel,
        out_shape=[jax.ShapeDtypeStruct((bsz, half, HY_WIDTH), BF16)] * 4,
        grid=(half // tf, bsz),
        in_specs=[tab, tab, tab, tab, sig, sig, kk, kk, kk, kk],
        out_specs=[out] * 4,
        compiler_params=_params(2),
        name="hy_fwd",
    )(*tabs, we, wo, *kspec)


def _hy_inv_kernel(cte_ref, cto_ref, ste_ref, sto_ref, urp_ref, urm_ref, uim_ref, uip_ref, x0_ref, w_ref,
                   bias_ref, o_ref, y_scr):
    tt = cte_ref.shape[0]
    y_even = _dot(cte_ref[...], urp_ref[0]) + _dot(ste_ref[...], uim_ref[0])
    y_odd = _dot(cto_ref[...], urm_ref[0]) + _dot(sto_ref[...], uip_ref[0])
    for j in range(HY_WIDTH // LANES):
        sl = slice(j * LANES, (j + 1) * LANES)
        y_scr[j, pl.ds(0, tt, stride=2), :] = y_even[:, sl]
        y_scr[j, pl.ds(1, tt, stride=2), :] = y_odd[:, sl]
        o_ref[0, :, sl] = (x0_ref[0, :, sl] * (y_scr[j] + w_ref[0, :, sl] * bias_ref[:, sl])).astype(BF16)


def _hy_inv_call(tabs_t, us, x0, w, bias):
    bsz, n, _ = x0.shape
    half = n // 2
    tt = min(256, half)
    tab = pl.BlockSpec((tt, half), lambda i, b: (i, 0))
    full = pl.BlockSpec((1, half, HY_WIDTH), lambda i, b: (b, 0, 0))
    tile = pl.BlockSpec((1, 2 * tt, HY_WIDTH), lambda i, b: (b, i, 0))
    return pl.pallas_call(
        _hy_inv_kernel,
        out_shape=jax.ShapeDtypeStruct((bsz, n, HY_WIDTH), BF16),
        grid=(half // tt, bsz),
        in_specs=[tab, tab, tab, tab, full, full, full, full, tile, tile,
                  pl.BlockSpec((1, HY_WIDTH), lambda i, b: (0, 0))],
        out_specs=tile,
        scratch_shapes=[pltpu.VMEM((HY_WIDTH // LANES, 2 * tt, LANES), F32)],
        compiler_params=_params(2),
        name="hy_inv",
    )(*tabs_t, *us, x0, w, bias)


LRU_SCAN_UNROLL = 4


def _lru_kernel(rxl_ref, rxc_ref, rgl_ref, rgc_ref, cw_ref, cb_ref, wa_ref, wx_ref, ba_ref, bx_ref, lam_ref,
                ol_ref, oc_ref, a_scr, b_scr, h_scr):
    cw = cw_ref[...]
    cb = cb_ref[...]
    row8 = lax.broadcasted_iota(I32, (SUBLANES, LANES), 0)

    def coeffs(x_ref, n):
        x = x_ref[0]
        xc = (cb + _shift_down(x, 2) * cw[0:1] + _shift_down(x, 1) * cw[1:2] + x * cw[2:3]
              + _shift_up(x, 1) * cw[3:4])
        xb = xc.astype(BF16)
        for d in range(2):
            r = _sigmoid(_dot(xb, wa_ref[d, 0]) + ba_ref[d])
            i = _sigmoid(_dot(xb, wx_ref[d, 0]) + bx_ref[d])
            nl = -lam_ref[d]
            softplus = jnp.maximum(nl, 0.0) + jnp.log(1.0 + jnp.exp(-jnp.abs(nl)))
            log_a = (-LRU_C) * r * softplus
            a = jnp.exp(log_a)
            a_scr[d, 0:n, :] = a
            b_scr[d, 0:n, :] = jnp.sqrt(1.0 - a * a) * i * xc

    def scan(n, carry_f, carry_b):
        ng = n // SUBLANES

        def body(g, carry):
            cf, cbk = carry
            of = pl.multiple_of(g * SUBLANES, SUBLANES)
            ob = pl.multiple_of((ng - 1 - g) * SUBLANES, SUBLANES)
            a = a_scr[0, pl.ds(of, SUBLANES), :]
            b = b_scr[0, pl.ds(of, SUBLANES), :]
            a2 = a_scr[1, pl.ds(ob, SUBLANES), :]
            b2 = b_scr[1, pl.ds(ob, SUBLANES), :]
            for k in (1, 2, 4):
                keep = row8 >= k
                b = a * jnp.where(keep, pltpu.roll(b, k, 0), 0.0) + b
                a = a * jnp.where(keep, pltpu.roll(a, k, 0), 1.0)
                keep2 = row8 < SUBLANES - k
                b2 = a2 * jnp.where(keep2, pltpu.roll(b2, SUBLANES - k, 0), 0.0) + b2
                a2 = a2 * jnp.where(keep2, pltpu.roll(a2, SUBLANES - k, 0), 1.0)
            hf = a * cf + b
            hb = a2 * cbk + b2
            h_scr[0, pl.ds(of, SUBLANES), :] = hf
            h_scr[1, pl.ds(ob, SUBLANES), :] = hb
            return hf[SUBLANES - 1:SUBLANES, :], hb[0:1, :]

        return lax.fori_loop(0, ng, body, (carry_f, carry_b), unroll=LRU_SCAN_UNROLL)

    n_ctx = rxc_ref.shape[1]
    n_lat = rxl_ref.shape[1]
    zero = jnp.zeros((1, LANES), F32)
    coeffs(rxc_ref, n_ctx)
    cf, cbk = scan(n_ctx, zero, zero)
    oc_ref[0] = ((h_scr[0, 0:n_ctx, :] + h_scr[1, 0:n_ctx, :]) * _gelu_tanh(rgc_ref[0].astype(F32))).astype(BF16)
    coeffs(rxl_ref, n_lat)
    scan(n_lat, cf, cbk)
    ol_ref[0] = ((h_scr[0] + h_scr[1]) * _gelu_tanh(rgl_ref[0].astype(F32))).astype(BF16)


def _lru_call(rx, rx_c, rg, rg_c, conv_w, conv_b, wa_bd, wx_bd, ba, bx, lam):
    bsz, t, _ = rx.shape
    tc = rx_c.shape[1]
    nch = LRU_WIDTH // LANES
    seq = lambda n: pl.BlockSpec((1, n, LANES), lambda b, j: (b, 0, j))
    vec = lambda r: pl.BlockSpec((r, 1, LANES), lambda b, j: (0, 0, j))
    wsp = pl.BlockSpec((2, 1, LANES, LANES), lambda b, j: (0, j, 0, 0))
    return pl.pallas_call(
        _lru_kernel,
        out_shape=[jax.ShapeDtypeStruct((bsz, t, LRU_WIDTH), BF16), jax.ShapeDtypeStruct((bsz, tc, LRU_WIDTH), BF16)],
        grid=(bsz, nch),
        in_specs=[seq(t), seq(tc), seq(t), seq(tc),
                  pl.BlockSpec((4, LANES), lambda b, j: (0, j)), pl.BlockSpec((1, LANES), lambda b, j: (0, j)),
                  wsp, wsp, vec(2), vec(2), vec(2)],
        out_specs=[seq(t), seq(tc)],
        scratch_shapes=[pltpu.VMEM((2, t, LANES), F32)] * 3,
        compiler_params=_params(2),
        name="lru",
    )(rx, rx_c, rg, rg_c, conv_w, conv_b, wa_bd, wx_bd, ba, bx, lam)


def _merge_kernel(attn_ref, hy_ref, lru_ref, gate_ref, h_ref, g1_ref, wa_ref, wh_ref, wl_ref, wo_ref,
                  n2_ref, sh2_ref, sc2_ref, rw_ref, hn_ref, u2_ref, lt_ref):
    def gate(j):
        return _sigmoid(gate_ref[0, :, j * D_MODEL:(j + 1) * D_MODEL].astype(F32))

    y = gate(0) * _dot(attn_ref[0], wa_ref[...])
    y = y + gate(1) * _dot(hy_ref[0], wh_ref[...])
    y = y + gate(2) * _dot(lru_ref[0], wl_ref[...])
    hn = h_ref[0] + g1_ref[0] * _dot(y.astype(BF16), wo_ref[...])
    hn_ref[0] = hn
    u2 = (_rms(hn) * n2_ref[...] * (1.0 + sc2_ref[0]) + sh2_ref[0]).astype(BF16)
    u2_ref[0] = u2
    lt_ref[0] = _dot_nt(rw_ref[...], u2)


def _merge_call(attn, hyo, lruo, gate, h, g1, wa, wh, wl, wo, n2g, sh2, sc2, rw_t, ctx_rows):
    bsz, t, d = h.shape
    tm = min(512, t)
    row = (lambda b: MOD_ROWS - 8) if ctx_rows else (lambda b: b)
    tok = lambda w: pl.BlockSpec((1, tm, w), lambda b, i: (b, i, 0))
    modspec = pl.BlockSpec((1, 1, d), lambda b, i: (row(b), 0, 0))
    const = lambda shape: pl.BlockSpec(shape, lambda b, i: (0,) * len(shape))
    return pl.pallas_call(
        _merge_kernel,
        out_shape=[jax.ShapeDtypeStruct((bsz, t, d), F32), jax.ShapeDtypeStruct((bsz, t, d), BF16),
                   jax.ShapeDtypeStruct((bsz, N_EXPERTS, t), F32)],
        grid=(bsz, t // tm),
        in_specs=[tok(Q_WIDTH), tok(HY_WIDTH), tok(LRU_WIDTH), tok(GATE_WIDTH), tok(d), modspec,
                  const((Q_WIDTH, d)), const((HY_WIDTH, d)), const((LRU_WIDTH, d)), const((d, d)),
                  const((1, d)), modspec, modspec, const((N_EXPERTS, d))],
        out_specs=[tok(d), tok(d), pl.BlockSpec((1, N_EXPERTS, tm), lambda b, i: (b, 0, i))],
        compiler_params=_params(2),
        name="merge",
    )(attn, hyo, lruo, gate, h, g1, wa, wh, wl, wo, n2g, sh2, sc2, rw_t)


def _router_kernel(lt_ref, slot_ref, slotc_ref, gc_ref, coff_ref, *, cap):
    lg = lt_ref[0]
    n_e, t = lg.shape
    ex = jnp.exp(lg - lg.max(axis=0, keepdims=True))
    aff = ex / ex.sum(axis=0, keepdims=True)
    key = pltpu.bitcast(aff, I32)
    capf = float(cap)

    def count(mask):
        return jnp.where(mask, 1.0, 0.0).sum(axis=1, keepdims=True)

    def vbody(i, thr):
        cand = thr | lax.shift_left(jnp.int32(1), 29 - i)
        return jnp.where(count(key >= cand) >= capf, cand, thr)

    thr = lax.fori_loop(0, 30, vbody, jnp.zeros((n_e, 1), I32))
    gt = key > thr
    eq = key == thr
    need = capf - count(gt)
    idx = lax.broadcasted_iota(I32, (n_e, t), 1)
    nbits = t.bit_length() - 1

    def ibody(i, lo):
        cand = lo | lax.shift_left(jnp.int32(1), nbits - 1 - i)
        return jnp.where(count(eq & (idx < cand)) < need, cand, lo)

    last = lax.fori_loop(0, nbits, ibody, jnp.zeros((n_e, 1), I32))
    sel = gt | (eq & (idx <= last))
    self32 = jnp.where(sel, 1.0, 0.0)
    gsel = jnp.where(sel, aff, 0.0)

    r_i = lax.broadcasted_iota(I32, (LANES, LANES), 0)
    c_i = lax.broadcasted_iota(I32, (LANES, LANES), 1)
    tri = jnp.where(r_i <= c_i, 1.0, 0.0).astype(BF16)
    eye = jnp.where(r_i == c_i, 1.0, 0.0).astype(BF16)
    off = jnp.zeros((n_e, 1), F32)
    n_units = t // LANES
    coff_ref[0] = jnp.zeros((n_e, LANES), I32)
    for c in range(n_units):
        sl = slice(c * LANES, (c + 1) * LANES)
        coff_ref[0, :, c:c + 1] = off.astype(I32)
        xs = self32[:, sl]
        inc = _dot(xs.astype(BF16), tri)
        slot1 = jnp.where(sel[:, sl], inc - xs + off + 1.0, 0.0)
        off = off + inc[:, LANES - 1:LANES]
        slot_ref[0, :, sl] = slot1.astype(I32) - 1
        hi = jnp.floor(slot1 * (1.0 / 16.0))
        lo = slot1 - 16.0 * hi
        col = 16.0 * _dot_nt(eye, hi.astype(BF16)) + _dot_nt(eye, lo.astype(BF16))
        slotc_ref[0, sl, :] = col.astype(I32) - 1
        g = gsel[:, sl]
        g1 = g.astype(BF16)
        r1 = g - g1.astype(F32)
        g2 = r1.astype(BF16)
        g3 = (r1 - g2.astype(F32)).astype(BF16)
        gc_ref[0, sl, :] = (_dot_nt(eye, g1) + _dot_nt(eye, g2)) + _dot_nt(eye, g3)
    coff_ref[0, :, n_units:n_units + 1] = off.astype(I32)


def _router_call(logits_t, cap):
    bsz, n_e, t = logits_t.shape
    assert t % LANES == 0 and t // LANES < LANES
    slot_row, slot_col, g_col, coff = pl.pallas_call(
        functools.partial(_router_kernel, cap=cap),
        out_shape=[jax.ShapeDtypeStruct((bsz, n_e, t), I32), jax.ShapeDtypeStruct((bsz, t, n_e), I32),
                   jax.ShapeDtypeStruct((bsz, t, n_e), F32), jax.ShapeDtypeStruct((bsz, n_e, LANES), I32)],
        grid=(bsz,),
        in_specs=[pl.BlockSpec((1, n_e, t), lambda b: (b, 0, 0))],
        out_specs=[pl.BlockSpec((1, n_e, t), lambda b: (b, 0, 0)), pl.BlockSpec((1, t, n_e), lambda b: (b, 0, 0)),
                   pl.BlockSpec((1, t, n_e), lambda b: (b, 0, 0)), pl.BlockSpec((1, n_e, LANES), lambda b: (b, 0, 0))],
        compiler_params=_params(1),
        name="router",
    )(logits_t)
    return slot_row, slot_col, g_col, coff[:, :, :t // LANES + 1].reshape(-1)


MOE_GATHER_TOKENS = 256
MOE_SLOT_ROWS = 128
MOE_SCATTER_SLOTS = 256


def _log2(n):
    assert n & (n - 1) == 0
    return n.bit_length() - 1


def _expert_kernel(coff_ref, u_ref, slot_ref, w1_ref, w3_ref, w2_ref, y_ref, xg_scr, *, tkg, sb, n_units):
    cap = xg_scr.shape[0]
    n_chunks = u_ref.shape[1] // tkg
    base = (pl.program_id(0) * pl.num_programs(1) + pl.program_id(1)) * (n_units + 1)
    upc = tkg // LANES
    rid = lax.broadcasted_iota(I32, (sb, tkg), 0)
    xg_scr[...] = jnp.zeros(xg_scr.shape, F32)

    def window_start(c):
        lo = coff_ref[base + c * upc]
        return jnp.minimum(lo & (-SUBLANES), cap - sb)

    for c in range(n_chunks):
        r0 = pl.multiple_of(window_start(c), SUBLANES)
        onehot = jnp.where(slot_ref[0, 0, c:c + 1, :] - r0 == rid, 1.0, 0.0).astype(BF16)
        xg_scr[pl.ds(r0, sb), :] += _dot(onehot, u_ref[0, c * tkg:(c + 1) * tkg, :])

    def overflow(c, carry):
        covered = window_start(c) + sb
        hi = coff_ref[base + (c + 1) * upc]
        first = lax.shift_right_logical(covered, _log2(sb))
        stop = jnp.where(hi > covered, lax.shift_right_logical(hi + (sb - 1), _log2(sb)), first)

        def block(j, carry2):
            b0 = pl.multiple_of(j * sb, sb)
            srow = slot_ref[0, 0, pl.ds(c, 1), :]
            onehot = jnp.where((srow - b0 == rid) & (srow >= covered), 1.0, 0.0).astype(BF16)
            xg_scr[pl.ds(b0, sb), :] += _dot(onehot, u_ref[0, pl.ds(pl.multiple_of(c * tkg, tkg), tkg), :])
            return carry2

        return lax.fori_loop(first, stop, block, carry)

    lax.fori_loop(0, n_chunks, overflow, 0)
    xb = xg_scr[...].astype(BF16)
    hid = (_silu(_dot(xb, w1_ref[0])) * _dot(xb, w3_ref[0])).astype(BF16)
    y_ref[0, 0] = _dot(hid, w2_ref[0]).astype(BF16)


def _expert_call(u2, slot_row, coff, w1, w3, w2, cap):
    bsz, t, d = u2.shape
    n_e = w1.shape[0]
    tkg = min(MOE_GATHER_TOKENS, t)
    sb = min(MOE_SLOT_ROWS, cap)
    wspec = pl.BlockSpec((1, d, d), lambda b, e, co: (e, 0, 0))
    grid_spec = pltpu.PrefetchScalarGridSpec(
        num_scalar_prefetch=1,
        grid=(bsz, n_e),
        in_specs=[pl.BlockSpec((1, t, d), lambda b, e, co: (b, 0, 0)),
                  pl.BlockSpec((1, 1, t // tkg, tkg), lambda b, e, co: (b, e, 0, 0)), wspec, wspec, wspec],
        out_specs=pl.BlockSpec((1, 1, cap, d), lambda b, e, co: (b, e, 0, 0)),
        scratch_shapes=[pltpu.VMEM((cap, d), F32)],
    )
    return pl.pallas_call(
        functools.partial(_expert_kernel, tkg=tkg, sb=sb, n_units=t // LANES),
        out_shape=jax.ShapeDtypeStruct((bsz, n_e, cap, d), BF16),
        grid_spec=grid_spec,
        compiler_params=_params(2),
        name="expert",
    )(coff, u2, slot_row.reshape(bsz, n_e, t // tkg, tkg), w1, w3, w2)


def _expert_call_folded(u2, slot_row, coff, w1, w3, w2, cap):
    bsz, t, d = u2.shape
    n_e = slot_row.shape[1]
    n_units = t // LANES
    first = (jnp.arange(bsz, dtype=I32) * cap)[:, None, None]
    slot_all = jnp.where(slot_row >= 0, slot_row + first, -1).transpose(1, 0, 2).reshape(1, n_e, bsz * t)
    coff_all = (coff.reshape(bsz, n_e, n_units + 1)[:, :, :n_units] + first).transpose(1, 0, 2)
    coff_all = jnp.concatenate([coff_all.reshape(n_e, bsz * n_units), jnp.full((n_e, 1), bsz * cap, I32)], axis=1)
    y = _expert_call(u2.reshape(1, bsz * t, d), slot_all, coff_all.reshape(-1), w1, w3, w2, bsz * cap)
    return y.reshape(n_e, bsz, cap, d).transpose(1, 0, 2, 3)


def _scatter_kernel(coff_ref, h_ref, y_ref, slotc_ref, gc_ref, g2_ref, o_ref, acc_scr, *, kb, n_units):
    tk = h_ref.shape[1]
    cap = y_ref.shape[2]
    upt = tk // LANES
    i = pl.program_id(1)
    lane = lax.broadcasted_iota(I32, (tk, kb), 1)
    bf16_rows = 2 * SUBLANES

    def span(e):
        base = (pl.program_id(0) * N_EXPERTS + e) * (n_units + 1)
        lo = coff_ref[base + i * upt]
        hi = coff_ref[base + (i + 1) * upt]
        return jnp.minimum(lo & (-bf16_rows), cap - kb), hi

    acc = jnp.zeros((tk, D_MODEL), F32)
    for e in range(N_EXPERTS):
        r0 = pl.multiple_of(span(e)[0], bf16_rows)
        onehot = jnp.where(slotc_ref[0, :, e:e + 1] - r0 == lane, 1.0, 0.0).astype(BF16)
        acc = acc + gc_ref[0, :, e:e + 1] * _dot(onehot, y_ref[0, e, pl.ds(r0, kb), :])
    acc_scr[...] = acc

    for e in range(N_EXPERTS):
        r0, hi = span(e)
        covered = r0 + kb
        first = lax.shift_right_logical(covered, _log2(kb))
        stop = jnp.where(hi > covered, lax.shift_right_logical(hi + (kb - 1), _log2(kb)), first)

        def block(j, carry, e=e, covered=covered):
            b0 = pl.multiple_of(j * kb, kb)
            scol = slotc_ref[0, :, e:e + 1]
            onehot = jnp.where((scol - b0 == lane) & (scol >= covered), 1.0, 0.0).astype(BF16)
            acc_scr[...] += gc_ref[0, :, e:e + 1] * _dot(onehot, y_ref[0, e, pl.ds(b0, kb), :])
            return carry

        lax.fori_loop(first, stop, block, 0)
    o_ref[0] = h_ref[0] + g2_ref[0] * acc_scr[...]


def _scatter_call(h, y, slot_col, g_col, coff, g2, cap, ctx_rows):
    bsz, t, d = h.shape
    n_e = y.shape[1]
    tk = min(512, t)
    kb = min(MOE_SCATTER_SLOTS, cap)
    row = (lambda b: MOD_ROWS - 8) if ctx_rows else (lambda b: b)
    tile = lambda w: pl.BlockSpec((1, tk, w), lambda b, i, co: (b, i, 0))
    grid_spec = pltpu.PrefetchScalarGridSpec(
        num_scalar_prefetch=1,
        grid=(bsz, t // tk),
        in_specs=[tile(d),
                  pl.BlockSpec((1, n_e, cap, d), lambda b, i, co: (b, 0, 0, 0), pipeline_mode=pl.Buffered(1)),
                  tile(n_e), tile(n_e), pl.BlockSpec((1, 1, d), lambda b, i, co: (row(b), 0, 0))],
        out_specs=tile(d),
        scratch_shapes=[pltpu.VMEM((tk, d), F32)],
    )
    return pl.pallas_call(
        functools.partial(_scatter_kernel, kb=kb, n_units=t // LANES),
        out_shape=jax.ShapeDtypeStruct((bsz, t, d), F32),
        grid_spec=grid_spec,
        compiler_params=_params(2),
        name="moe_scatter",
    )(coff, h, y, slot_col, g_col, g2)


def _final_norm_kernel(h_ref, g_ref, o_ref):
    o_ref[0] = _rms(h_ref[0]) * g_ref[...]


def _final_norm_call(h, g):
    bsz, t, d = h.shape
    tm = min(512, t)
    tile = pl.BlockSpec((1, tm, d), lambda b, i: (b, i, 0))
    return pl.pallas_call(
        _final_norm_kernel,
        out_shape=jax.ShapeDtypeStruct((bsz, t, d), F32),
        grid=(bsz, t // tm),
        in_specs=[tile, pl.BlockSpec((1, d), lambda b, i: (0, 0))],
        out_specs=tile,
        compiler_params=_params(2),
        name="final_norm",
    )(h, g)


def _rope_tables(n_lat, n_ctx):
    rows = n_lat // GRID_W
    row = jnp.repeat(jnp.arange(rows, dtype=F32), GRID_W)
    col = jnp.tile(jnp.arange(GRID_W, dtype=F32), rows)
    inv = jnp.power(ROPE_THETA, -jnp.arange(ROPE_PAIRS_AXIS, dtype=F32) / ROPE_PAIRS_AXIS)
    ang = jnp.concatenate([row[:, None] * inv, col[:, None] * inv], axis=-1)
    cos, sin = jnp.cos(ang), jnp.sin(ang)
    cs = jnp.concatenate([cos, cos], axis=-1)
    sn = jnp.concatenate([-sin, sin], axis=-1)
    return cs, sn, jnp.ones((n_ctx, HEAD_DIM), F32), jnp.zeros((n_ctx, HEAD_DIM), F32)


def _dft_tables(n):
    k = jnp.arange(n // 2, dtype=I32)

    def tables(first):
        m = ((2 * k[:, None] + 1) * (2 * k[None, :] + first)) % (4 * n)
        ang = m.astype(F32) * (2.0 * math.pi / (4 * n))
        return jnp.cos(ang), jnp.sin(ang)

    ce, se = tables(0)
    co, so = tables(1)
    fwd = tuple(x.astype(BF16) for x in (ce, co, se, so))
    inv = tuple(x.T.astype(BF16) for x in (ce, co, -se, -so))
    return fwd, inv


def _filter_features(n):
    t = jnp.linspace(0.0, 1.0, n, dtype=F32)[:, None]
    w = (2.0 * math.pi / n) * jnp.arange(n, dtype=F32)[:, None]
    f = jnp.linspace(1e-4, HY_BANDS - 1, HY_BANDS, dtype=F32)[None, :]
    feats = jnp.concatenate([t, jnp.cos(f * w), -jnp.sin(f * w)], axis=-1)
    return jnp.pad(feats, ((0, 0), (0, LANES - HY_EMB)))


def _pad_to(x, shape):
    return jnp.pad(x, [(0, s - d) for d, s in zip(x.shape, shape)])


def _block_diag_chunks(w):
    per = LANES // LRU_BLOCK
    w = w.reshape(2, LRU_BLOCKS // per, per, LRU_BLOCK, LRU_BLOCK)
    eye = jnp.eye(per, dtype=w.dtype)
    return jnp.einsum('dcpkj,pq->dcpkqj', w, eye).reshape(2, LRU_BLOCKS // per, LANES, LANES)


def kernel(x, c, ctx, c_ctx, mod_w, mod_b, norm1_g, norm2_g, w_in, q_norm_g, k_norm_g, hy_conv_w, hy_conv_b,
           hy_fw1, hy_fb1, hy_fw2, hy_fb2, hy_fw3, hy_freq, hy_bias, lru_conv_w, lru_conv_b, lru_wa, lru_ba,
           lru_wx, lru_bx, lru_lambda, w_attn_out, w_hy_out, w_lru_out, w_out, router_w, exp_w1, exp_w3, exp_w2,
           final_norm_g):
    bsz, n_lat, d = x.shape
    n_ctx = ctx.shape[1]
    depth = mod_w.shape[0]
    assert d == D_MODEL and bsz <= MOD_ROWS - 8 and n_lat % GRID_W == 0

    cs_l, sn_l, cs_c, sn_c = _rope_tables(n_lat, n_ctx)
    dft_l = _dft_tables(n_lat)
    dft_c = _dft_tables(n_ctx)
    feats_l = _filter_features(n_lat)
    feats_c = _filter_features(n_ctx)
    max_decay = math.log(HY_DECAY_TARGET) / HY_FAST_DECAY_PCT
    min_decay = math.log(HY_DECAY_TARGET) / HY_SLOW_DECAY_PCT
    deltas_abs = jnp.abs(jnp.linspace(min_decay, max_decay, HY_WIDTH, dtype=F32))[None, :]
    cap_l = max(1, EC_CAPACITY * n_lat // N_EXPERTS)
    cap_c = max(1, EC_CAPACITY * n_ctx // N_EXPERTS)

    cvec = jnp.zeros((MOD_ROWS, d), F32).at[:bsz].set(c).at[MOD_ROWS - 8].set(c_ctx)
    mod = _mod_call(cvec, mod_w, mod_b)
    mod = mod.reshape(depth, MOD_ROWS, 6, 1, d).transpose(0, 2, 1, 3, 4)

    h, hc = x, ctx
    for l in range(depth):
        last = l == depth - 1
        sh1, sc1, g1, sh2, sc2, g2 = (mod[l, j] for j in range(6))
        w_in_b = w_in[l].astype(BF16)
        n1 = norm1_g[l][None, :]
        n2 = norm2_g[l][None, :]
        qg = q_norm_g[l][None, :]
        kg = k_norm_g[l][None, :]

        q, k, v, hy, rx, rg, gate = _inproj_call(h, sh1, sc1, n1, w_in_b, cs_l, sn_l, qg, kg, False)
        qc, kc, vc, hyc, rxc, rgc, gatec = _inproj_call(hc, sh1, sc1, n1, w_in_b, cs_c, sn_c, qg, kg, True)

        attn = _attn_call(q, [(k, v), (kc, vc)])

        fw1 = _pad_to(hy_fw1[l], (LANES, LANES))
        fb1 = _pad_to(hy_fb1[l][None, :], (1, LANES))
        fw2 = _pad_to(hy_fw2[l], (LANES, LANES))
        fb2 = _pad_to(hy_fb2[l][None, :], (1, LANES))
        fw3 = _pad_to(hy_fw3[l], (LANES, 2 * HY_WIDTH))
        freq = _pad_to(hy_freq[l], (2, LANES))
        hbias = hy_bias[l][None, :]

        def hyena(hy_in, feats, tabs):
            tabs_fwd, tabs_inv = tabs
            hsd = _hy_filter_call(feats, fw1, fb1, fw2, fb2, fw3, freq, deltas_abs)
            kspec = _hy_kspec_call(tabs_fwd, hsd)
            x0, w, we, wo = _hy_pre_call(hy_in, hy_conv_w[l], hy_conv_b[l][None, :])
            us = _hy_fwd_call(tabs_fwd, we, wo, kspec)
            return _hy_inv_call(tabs_inv, us, x0, w, hbias)

        hyo = hyena(hy, feats_l, dft_l)

        lruo, lruoc = _lru_call(rx, rxc, rg, rgc, lru_conv_w[l], lru_conv_b[l][None, :],
                                _block_diag_chunks(lru_wa[l]).astype(BF16), _block_diag_chunks(lru_wx[l]).astype(BF16),
                                lru_ba[l][:, None, :], lru_bx[l][:, None, :], lru_lambda[l][:, None, :])

        wa = w_attn_out[l].astype(BF16)
        wh = w_hy_out[l].astype(BF16)
        wl = w_lru_out[l].astype(BF16)
        wo = w_out[l].astype(BF16)
        rw_t = router_w[l].T.astype(BF16)
        e1 = exp_w1[l].astype(BF16)
        e3 = exp_w3[l].astype(BF16)
        e2 = exp_w2[l].astype(BF16)

        def channel_mix(h_in, attn_in, hy_in, lru_in, gate_in, cap, ctx_rows):
            hn, u2, lt = _merge_call(attn_in, hy_in, lru_in, gate_in, h_in, g1, wa, wh, wl, wo, n2, sh2, sc2, rw_t,
                                     ctx_rows)
            slot_row, slot_col, g_col, coff = _router_call(lt, cap)
            if ctx_rows:
                y = _expert_call_folded(u2, slot_row, coff, e1, e3, e2, cap)
            else:
                y = _expert_call(u2, slot_row, coff, e1, e3, e2, cap)
            return _scatter_call(hn, y, slot_col, g_col, coff, g2, cap, ctx_rows)

        h = channel_mix(h, attn, hyo, lruo, gate, cap_l, False)
        if not last:
            attn_c = _attn_call(qc, [(kc, vc)])
            hyo_c = hyena(hyc, feats_c, dft_c)
            hc = channel_mix(hc, attn_c, hyo_c, lruoc, gatec, cap_c, True)

    return _final_norm_call(h, final_norm_g[None, :])
```

```python
import functools
import math

import jax
import jax.numpy as jnp
from jax import lax
from jax.experimental import pallas as pl
from jax.experimental.pallas import tpu as pltpu

F32 = jnp.float32
BF16 = jnp.bfloat16
I32 = jnp.int32

D_MODEL = 1024
GRID_W = 64
NORM_EPS = 1e-6
N_HEADS = 8
N_KV_HEADS = 2
HEAD_DIM = 128
GROUP = N_HEADS // N_KV_HEADS
ROPE_PAIRS_AXIS = HEAD_DIM // 4
ROPE_THETA = 10000.0
ATTN_SCALE = HEAD_DIM ** -0.5
Q_PRESCALE = ATTN_SCALE * math.log2(math.e)
Q_WIDTH = N_HEADS * HEAD_DIM
KV_WIDTH = N_KV_HEADS * HEAD_DIM
HY_WIDTH = D_MODEL // 2
HY_BANDS = 16
HY_EMB = 1 + 2 * HY_BANDS
HY_FILTER_HIDDEN = 64
HY_FAST_DECAY_PCT = 0.3
HY_SLOW_DECAY_PCT = 1.5
HY_DECAY_TARGET = 1e-2
LRU_WIDTH = D_MODEL // 2
LRU_BLOCKS = 8
LRU_BLOCK = LRU_WIDTH // LRU_BLOCKS
LRU_C = 8.0
N_BRANCH = 3
N_EXPERTS = 16
EC_CAPACITY = 2
GATE_WIDTH = N_BRANCH * D_MODEL
OFF_Q = 0
OFF_K = OFF_Q + Q_WIDTH
OFF_V = OFF_K + KV_WIDTH
OFF_HY = OFF_V + KV_WIDTH
OFF_RX = OFF_HY + 3 * HY_WIDTH
OFF_RG = OFF_RX + LRU_WIDTH
OFF_GATE = OFF_RG + LRU_WIDTH
IN_TOTAL = OFF_GATE + GATE_WIDTH

LANES = 128
SUBLANES = 8
V7X_VMEM_LIMIT_BYTES = 56 * 1024 * 1024
MOD_ROWS = 24


def _params(n_axes, vmem=V7X_VMEM_LIMIT_BYTES):
    return pltpu.CompilerParams(dimension_semantics=("arbitrary",) * n_axes, vmem_limit_bytes=vmem)


def _dot(a, b):
    return jnp.dot(a, b, preferred_element_type=F32)


def _dot_nt(a, b):
    return lax.dot_general(a, b, (((1,), (1,)), ((), ())), preferred_element_type=F32)


def _split2(x):
    hi = x.astype(BF16)
    lo = (x - hi.astype(F32)).astype(BF16)
    return hi, lo


def _dot3(a, b):
    ah, al = _split2(a)
    bh, bl = _split2(b)
    return _dot(ah, bh) + (_dot(ah, bl) + _dot(al, bh))


def _rms(x):
    return x * lax.rsqrt(jnp.mean(x * x, axis=-1, keepdims=True) + NORM_EPS)


def _sigmoid(x):
    return 1.0 / (1.0 + jnp.exp(-x))


def _silu(x):
    return x * _sigmoid(x)


def _gelu_tanh(x):
    return 0.5 * x * (1.0 + jnp.tanh(math.sqrt(2.0 / math.pi) * (x + 0.044715 * (x * x * x))))


def _shift_down(x, k):
    row = lax.broadcasted_iota(I32, x.shape, 0)
    return jnp.where(row >= k, pltpu.roll(x, k, 0), 0.0)


def _shift_up(x, k):
    n = x.shape[0]
    row = lax.broadcasted_iota(I32, x.shape, 0)
    return jnp.where(row < n - k, pltpu.roll(x, n - k, 0), 0.0)


def _mod_kernel(c_ref, w_ref, b_ref, o_ref):
    o_ref[0] = _dot3(_silu(c_ref[...]), w_ref[0]) + b_ref[0]


def _mod_call(cvec, mod_w, mod_b):
    depth, d, six_d = mod_w.shape
    tn = 1536
    return pl.pallas_call(
        _mod_kernel,
        out_shape=jax.ShapeDtypeStruct((depth, MOD_ROWS, six_d), F32),
        grid=(depth, six_d // tn),
        in_specs=[
            pl.BlockSpec((MOD_ROWS, d), lambda l, j: (0, 0)),
            pl.BlockSpec((1, d, tn), lambda l, j: (l, 0, j)),
            pl.BlockSpec((1, 1, tn), lambda l, j: (l, 0, j)),
        ],
        out_specs=pl.BlockSpec((1, MOD_ROWS, tn), lambda l, j: (l, 0, j)),
        compiler_params=_params(2),
        name="mod",
    )(cvec, mod_w, mod_b.reshape(depth, 1, six_d))


def _inproj_kernel(h_ref, sh_ref, sc_ref, g_ref, w_ref, cs_ref, sn_ref, qg_ref, kg_ref,
                   q_ref, k_ref, v_ref, hy_ref, rx_ref, rg_ref, gate_ref):
    u = (_rms(h_ref[0]) * g_ref[...] * (1.0 + sc_ref[0]) + sh_ref[0]).astype(BF16)
    cs = cs_ref[...]
    sn = sn_ref[...]

    def normed_rope(z, g):
        r = _rms(z) * g
        return r * cs + pltpu.roll(r, HEAD_DIM // 2, 1) * sn

    zq = _dot(u, w_ref[:, OFF_Q:OFF_Q + Q_WIDTH])
    for hd in range(N_HEADS):
        sl = slice(hd * HEAD_DIM, (hd + 1) * HEAD_DIM)
        q_ref[0, hd] = (normed_rope(zq[:, sl], qg_ref[...]) * Q_PRESCALE).astype(BF16)
    zkv = _dot(u, w_ref[:, OFF_K:OFF_K + 2 * KV_WIDTH])
    for hd in range(N_KV_HEADS):
        sl = slice(hd * HEAD_DIM, (hd + 1) * HEAD_DIM)
        k_ref[0, :, sl] = normed_rope(zkv[:, sl], kg_ref[...]).astype(BF16)
    v_ref[0] = zkv[:, KV_WIDTH:].astype(BF16)
    for j in range(3):
        sl = slice(j * HY_WIDTH, (j + 1) * HY_WIDTH)
        hy_ref[0, :, sl] = _dot(u, w_ref[:, OFF_HY + j * HY_WIDTH:OFF_HY + (j + 1) * HY_WIDTH])
    rr = _dot(u, w_ref[:, OFF_RX:OFF_RX + 2 * LRU_WIDTH])
    rx_ref[0] = rr[:, :LRU_WIDTH]
    rg_ref[0] = rr[:, LRU_WIDTH:].astype(BF16)
    for j in range(N_BRANCH):
        sl = slice(j * D_MODEL, (j + 1) * D_MODEL)
        gate_ref[0, :, sl] = _dot(u, w_ref[:, OFF_GATE + j * D_MODEL:OFF_GATE + (j + 1) * D_MODEL]).astype(BF16)


def _inproj_call(h, sh, sc, g, w_bf, cs, sn, qg, kg, ctx_rows):
    bsz, t, d = h.shape
    tm = min(512, t)
    row = (lambda b: MOD_ROWS - 8) if ctx_rows else (lambda b: b)
    tok = lambda w: pl.BlockSpec((1, tm, w), lambda b, i: (b, i, 0))
    modspec = pl.BlockSpec((1, 1, d), lambda b, i: (row(b), 0, 0))
    const = lambda shape: pl.BlockSpec(shape, lambda b, i: (0,) * len(shape))
    outs = [(KV_WIDTH, BF16), (KV_WIDTH, BF16), (3 * HY_WIDTH, F32), (LRU_WIDTH, F32), (LRU_WIDTH, BF16),
            (GATE_WIDTH, BF16)]
    return pl.pallas_call(
        _inproj_kernel,
        out_shape=[jax.ShapeDtypeStruct((bsz, N_HEADS, t, HEAD_DIM), BF16)]
        + [jax.ShapeDtypeStruct((bsz, t, w), dt) for w, dt in outs],
        grid=(bsz, t // tm),
        in_specs=[
            tok(d), modspec, modspec, const((1, d)),
            pl.BlockSpec((d, IN_TOTAL), lambda b, i: (0, 0), pipeline_mode=pl.Buffered(1)),
            pl.BlockSpec((tm, HEAD_DIM), lambda b, i: (i, 0)),
            pl.BlockSpec((tm, HEAD_DIM), lambda b, i: (i, 0)),
            const((1, HEAD_DIM)), const((1, HEAD_DIM)),
        ],
        out_specs=[pl.BlockSpec((1, N_HEADS, tm, HEAD_DIM), lambda b, i: (b, 0, i, 0))] + [tok(w) for w, _ in outs],
        compiler_params=_params(2),
        name="inproj",
    )(h, sh, sc, g, w_bf, cs, sn, qg, kg)


ATTN_TQ = 512
ATTN_TK = 512
ATTN_STREAMS = 4
NEG_BIG = -1e30


def _attn_kernel(*refs, n_src, tq):
    q_ref = refs[0]
    kv = refs[1:1 + 2 * n_src]
    o_ref, kcat, vext = refs[1 + 2 * n_src:]
    n_keys = kcat.shape[0]

    @pl.when(pl.program_id(2) == 0)
    def _():
        off = 0
        for i in range(n_src):
            n = kv[2 * i].shape[1]
            kcat[off:off + n, :] = kv[2 * i][0]
            vext[off:off + n, 0:HEAD_DIM] = kv[2 * i + 1][0]
            off += n
        vext[:, HEAD_DIM:] = jnp.ones((n_keys, HEAD_DIM), BF16)

    rows = GROUP * tq
    q = q_ref[0].reshape(rows, HEAD_DIM)
    per = rows // ATTN_STREAMS
    qs = [q[i * per:(i + 1) * per] for i in range(ATTN_STREAMS)]
    ms = [jnp.full((per, 1), NEG_BIG, F32) for _ in qs]
    accs = [jnp.zeros((per, 2 * HEAD_DIM), F32) for _ in qs]
    for off in range(0, n_keys, ATTN_TK):
        size = min(ATTN_TK, n_keys - off)
        kk = kcat[off:off + size, :]
        vv = vext[off:off + size, :]
        for i in range(ATTN_STREAMS):
            s = _dot_nt(qs[i], kk)
            m_new = jnp.maximum(ms[i], s.max(axis=-1, keepdims=True))
            p = jnp.exp2(s - m_new).astype(BF16)
            accs[i] = jnp.exp2(ms[i] - m_new) * accs[i] + _dot(p, vv)
            ms[i] = m_new
    out = jnp.concatenate([a[:, :HEAD_DIM] / a[:, HEAD_DIM:] for a in accs], axis=0).astype(BF16)
    for g in range(GROUP):
        o_ref[0, :, g * HEAD_DIM:(g + 1) * HEAD_DIM] = out[g * tq:(g + 1) * tq]


def _attn_call(q, kvs):
    bsz, _, tq_all, _ = q.shape
    tq = min(ATTN_TQ, tq_all)
    gw = GROUP * HEAD_DIM
    in_specs = [pl.BlockSpec((1, GROUP, tq, HEAD_DIM), lambda b, h, i: (b, h, i, 0))]
    args = [q]
    n_keys = 0
    for k, v in kvs:
        tk = k.shape[1]
        n_keys += tk
        spec = pl.BlockSpec((1, tk, HEAD_DIM), lambda b, h, i: (b, 0, h))
        in_specs += [spec, spec]
        args += [k, v]
    return pl.pallas_call(
        functools.partial(_attn_kernel, n_src=len(kvs), tq=tq),
        out_shape=jax.ShapeDtypeStruct((bsz, tq_all, Q_WIDTH), BF16),
        grid=(bsz, N_KV_HEADS, tq_all // tq),
        in_specs=in_specs,
        out_specs=pl.BlockSpec((1, tq, gw), lambda b, h, i: (b, i, h)),
        scratch_shapes=[pltpu.VMEM((n_keys, HEAD_DIM), BF16), pltpu.VMEM((n_keys, 2 * HEAD_DIM), BF16)],
        compiler_params=_params(3),
        name="attn",
    )(*args)


def _hy_pre_kernel(h0_ref, h1_ref, h2_ref, w0_ref, w1_ref, w2_ref, b0_ref, b1_ref, b2_ref,
                   x0_ref, w_ref, we_ref, wo_ref):
    def conv(x_ref, cw_ref, cb_ref):
        x = x_ref[0]
        cw = cw_ref[...]
        return cb_ref[...] + _shift_down(x, 1) * cw[0:1] + x * cw[1:2] + _shift_up(x, 1) * cw[2:3]

    x0_ref[0] = conv(h0_ref, w0_ref, b0_ref)
    w = conv(h2_ref, w2_ref, b2_ref) * conv(h1_ref, w1_ref, b1_ref)
    w_ref[0] = w
    half = w.shape[0] // 2
    we_ref[0] = w_ref[0, pl.ds(0, half, stride=2), :].astype(BF16)
    wo_ref[0] = w_ref[0, pl.ds(1, half, stride=2), :].astype(BF16)


def _hy_pre_call(hy, conv_w, conv_b):
    bsz, t, _ = hy.shape
    nch = HY_WIDTH // LANES
    xs = lambda part: pl.BlockSpec((1, t, LANES), lambda b, j: (b, 0, part * nch + j))
    ws = lambda part: pl.BlockSpec((3, LANES), lambda b, j: (0, part * nch + j))
    bs = lambda part: pl.BlockSpec((1, LANES), lambda b, j: (0, part * nch + j))
    out = pl.BlockSpec((1, t, LANES), lambda b, j: (b, 0, j))
    outh = pl.BlockSpec((1, t // 2, LANES), lambda b, j: (b, 0, j))
    return pl.pallas_call(
        _hy_pre_kernel,
        out_shape=[jax.ShapeDtypeStruct((bsz, t, HY_WIDTH), F32), jax.ShapeDtypeStruct((bsz, t, HY_WIDTH), F32),
                   jax.ShapeDtypeStruct((bsz, t // 2, HY_WIDTH), BF16),
                   jax.ShapeDtypeStruct((bsz, t // 2, HY_WIDTH), BF16)],
        grid=(bsz, nch),
        in_specs=[xs(0), xs(1), xs(2), ws(0), ws(1), ws(2), bs(0), bs(1), bs(2)],
        out_specs=[out, out, outh, outh],
        compiler_params=_params(2),
        name="hy_pre",
    )(hy, hy, hy, conv_w, conv_w, conv_w, conv_b, conv_b, conv_b)


def _hy_filter_kernel(feat_ref, w1_ref, b1_ref, w2_ref, b2_ref, w3_ref, fr_ref, dl_ref, o_ref):
    feats = feat_ref[...]
    hid = jnp.sin(fr_ref[0:1] * (_dot3(feats, w1_ref[...]) + b1_ref[...]))
    hid = jnp.sin(fr_ref[1:2] * (_dot3(hid, w2_ref[...]) + b2_ref[...]))
    filt = _dot3(hid, w3_ref[...])
    decay = jnp.exp(-feats[:, 0:1] * dl_ref[...])
    h_fwd = filt[:, :HY_WIDTH] * decay
    h_bwd = filt[:, HY_WIDTH:] * decay
    o_ref[:, :HY_WIDTH] = h_fwd + h_bwd
    o_ref[:, HY_WIDTH:] = h_bwd - h_fwd


def _hy_filter_call(feats, fw1, fb1, fw2, fb2, fw3, freq, deltas_abs):
    n = feats.shape[0]
    tn = min(512, n)
    const = lambda shape: pl.BlockSpec(shape, lambda i: (0,) * len(shape))
    return pl.pallas_call(
        _hy_filter_kernel,
        out_shape=jax.ShapeDtypeStruct((n, 2 * HY_WIDTH), F32),
        grid=(n // tn,),
        in_specs=[pl.BlockSpec((tn, LANES), lambda i: (i, 0)), const((LANES, LANES)), const((1, LANES)),
                  const((LANES, LANES)), const((1, LANES)), const((LANES, 2 * HY_WIDTH)), const((2, LANES)),
                  const((1, HY_WIDTH))],
        out_specs=pl.BlockSpec((tn, 2 * HY_WIDTH), lambda i: (i, 0)),
        compiler_params=_params(1),
        name="hy_filter",
    )(feats, fw1, fb1, fw2, fb2, fw3, freq, deltas_abs)


def _hy_kspec_kernel(ce_ref, co_ref, se_ref, so_ref, he_ref, ho_ref, krl_ref, kil_ref, krh_ref, kih_ref, *, scale):
    def branch(tab_e, tab_o, cols):
        he_hi, he_lo = _split2(he_ref[:, cols])
        ho_hi, ho_lo = _split2(ho_ref[:, cols])
        pe = _dot(tab_e[...], he_hi) + _dot(tab_e[...], he_lo)
        po = _dot(tab_o[...], ho_hi) + _dot(tab_o[...], ho_lo)
        return pe, po

    pc, qc = branch(ce_ref, co_ref, slice(0, HY_WIDTH))
    ps, qs = branch(se_ref, so_ref, slice(HY_WIDTH, 2 * HY_WIDTH))
    krl_ref[...] = (pc + qc) * scale
    krh_ref[...] = (pc - qc) * scale
    kil_ref[...] = (ps + qs) * scale
    kih_ref[...] = (qs - ps) * scale


def _hy_kspec_call(tabs, hsd):
    n = hsd.shape[0]
    half = n // 2
    tf = min(512, half)
    tab = pl.BlockSpec((tf, half), lambda i: (i, 0))
    out = pl.BlockSpec((tf, HY_WIDTH), lambda i: (i, 0))
    return pl.pallas_call(
        functools.partial(_hy_kspec_kernel, scale=1.0 / n),
        out_shape=[jax.ShapeDtypeStruct((half, HY_WIDTH), F32)] * 4,
        grid=(half // tf,),
        in_specs=[tab, tab, tab, tab, pl.BlockSpec((half, 2 * HY_WIDTH), lambda i: (0, 0)),
                  pl.BlockSpec((half, 2 * HY_WIDTH), lambda i: (0, 0))],
        out_specs=[out] * 4,
        compiler_params=_params(1),
        name="hy_kspec",
    )(*tabs, hsd[0::2], hsd[1::2])


def _hy_fwd_kernel(ce_ref, co_ref, se_ref, so_ref, we_ref, wo_ref, krl_ref, kil_ref, krh_ref, kih_ref,
                   urp_ref, urm_ref, uim_ref, uip_ref):
    we = we_ref[0]
    wo = wo_ref[0]
    pc = _dot(ce_ref[...], we)
    qc = _dot(co_ref[...], wo)
    ps = _dot(se_ref[...], we)
    qs = _dot(so_ref[...], wo)
    a_lo, a_hi = pc + qc, pc - qc
    b_lo, b_hi = ps + qs, qs - ps
    krl, kil, krh, kih = krl_ref[...], kil_ref[...], krh_ref[...], kih_ref[...]
    yre_lo = a_lo * krl + b_lo * kil
    yim_lo = a_lo * kil - b_lo * krl
    yre_hi = a_hi * krh + b_hi * kih
    yim_hi = a_hi * kih - b_hi * krh
    urp_ref[0] = (yre_lo + yre_hi).astype(BF16)
    urm_ref[0] = (yre_lo - yre_hi).astype(BF16)
    uim_ref[0] = (yim_lo - yim_hi).astype(BF16)
    uip_ref[0] = (yim_lo + yim_hi).astype(BF16)


def _hy_fwd_call(tabs, we, wo, kspec):
    bsz, half, _ = we.shape
    tf = min(512, half)
    tab = pl.BlockSpec((tf, half), lambda i, b: (i, 0))
    sig = pl.BlockSpec((1, half, HY_WIDTH), lambda i, b: (b, 0, 0))
    kk = pl.BlockSpec((tf, HY_WIDTH), lambda i, b: (i, 0))
    out = pl.BlockSpec((1, tf, HY_WIDTH), lambda i, b: (b, i, 0))
    return pl.pallas_call(
        _hy_fwd_kernel,
        out_shape=[jax.ShapeDtypeStruct((bsz, half, HY_WIDTH), BF16)] * 4,
        grid=(half // tf, bsz),
        in_specs=[tab, tab, tab, tab, sig, sig, kk, kk, kk, kk],
        out_specs=[out] * 4,
        compiler_params=_params(2),
        name="hy_fwd",
    )(*tabs, we, wo, *kspec)


def _hy_inv_kernel(cte_ref, cto_ref, ste_ref, sto_ref, urp_ref, urm_ref, uim_ref, uip_ref, x0_ref, w_ref,
                   bias_ref, o_ref, y_scr):
    tt = cte_ref.shape[0]
    y_even = _dot(cte_ref[...], urp_ref[0]) + _dot(ste_ref[...], uim_ref[0])
    y_odd = _dot(cto_ref[...], urm_ref[0]) + _dot(sto_ref[...], uip_ref[0])
    for j in range(HY_WIDTH // LANES):
        sl = slice(j * LANES, (j + 1) * LANES)
        y_scr[j, pl.ds(0, tt, stride=2), :] = y_even[:, sl]
        y_scr[j, pl.ds(1, tt, stride=2), :] = y_odd[:, sl]
        o_ref[0, :, sl] = (x0_ref[0, :, sl] * (y_scr[j] + w_ref[0, :, sl] * bias_ref[:, sl])).astype(BF16)


def _hy_inv_call(tabs_t, us, x0, w, bias):
    bsz, n, _ = x0.shape
    half = n // 2
    tt = min(256, half)
    tab = pl.BlockSpec((tt, half), lambda i, b: (i, 0))
    full = pl.BlockSpec((1, half, HY_WIDTH), lambda i, b: (b, 0, 0))
    tile = pl.BlockSpec((1, 2 * tt, HY_WIDTH), lambda i, b: (b, i, 0))
    return pl.pallas_call(
        _hy_inv_kernel,
        out_shape=jax.ShapeDtypeStruct((bsz, n, HY_WIDTH), BF16),
        grid=(half // tt, bsz),
        in_specs=[tab, tab, tab, tab, full, full, full, full, tile, tile,
                  pl.BlockSpec((1, HY_WIDTH), lambda i, b: (0, 0))],
        out_specs=tile,
        scratch_shapes=[pltpu.VMEM((HY_WIDTH // LANES, 2 * tt, LANES), F32)],
        compiler_params=_params(2),
        name="hy_inv",
    )(*tabs_t, *us, x0, w, bias)


LRU_SCAN_UNROLL = 4


def _lru_kernel(rxl_ref, rxc_ref, rgl_ref, rgc_ref, cw_ref, cb_ref, wa_ref, wx_ref, ba_ref, bx_ref, lam_ref,
                ol_ref, oc_ref, a_scr, b_scr, h_scr):
    cw = cw_ref[...]
    cb = cb_ref[...]
    row8 = lax.broadcasted_iota(I32, (SUBLANES, LANES), 0)

    def coeffs(x_ref, n):
        x = x_ref[0]
        xc = (cb + _shift_down(x, 2) * cw[0:1] + _shift_down(x, 1) * cw[1:2] + x * cw[2:3]
              + _shift_up(x, 1) * cw[3:4])
        xb = xc.astype(BF16)
        for d in range(2):
            r = _sigmoid(_dot(xb, wa_ref[d, 0]) + ba_ref[d])
            i = _sigmoid(_dot(xb, wx_ref[d, 0]) + bx_ref[d])
            nl = -lam_ref[d]
            softplus = jnp.maximum(nl, 0.0) + jnp.log(1.0 + jnp.exp(-jnp.abs(nl)))
            log_a = (-LRU_C) * r * softplus
            a = jnp.exp(log_a)
            a_scr[d, 0:n, :] = a
            b_scr[d, 0:n, :] = jnp.sqrt(1.0 - a * a) * i * xc

    def scan(n, carry_f, carry_b):
        ng = n // SUBLANES

        def body(g, carry):
            cf, cbk = carry
            of = pl.multiple_of(g * SUBLANES, SUBLANES)
            ob = pl.multiple_of((ng - 1 - g) * SUBLANES, SUBLANES)
            a = a_scr[0, pl.ds(of, SUBLANES), :]
            b = b_scr[0, pl.ds(of, SUBLANES), :]
            a2 = a_scr[1, pl.ds(ob, SUBLANES), :]
            b2 = b_scr[1, pl.ds(ob, SUBLANES), :]
            for k in (1, 2, 4):
                keep = row8 >= k
                b = a * jnp.where(keep, pltpu.roll(b, k, 0), 0.0) + b
                a = a * jnp.where(keep, pltpu.roll(a, k, 0), 1.0)
                keep2 = row8 < SUBLANES - k
                b2 = a2 * jnp.where(keep2, pltpu.roll(b2, SUBLANES - k, 0), 0.0) + b2
                a2 = a2 * jnp.where(keep2, pltpu.roll(a2, SUBLANES - k, 0), 1.0)
            hf = a * cf + b
            hb = a2 * cbk + b2
            h_scr[0, pl.ds(of, SUBLANES), :] = hf
            h_scr[1, pl.ds(ob, SUBLANES), :] = hb
            return hf[SUBLANES - 1:SUBLANES, :], hb[0:1, :]

        return lax.fori_loop(0, ng, body, (carry_f, carry_b), unroll=LRU_SCAN_UNROLL)

    n_ctx = rxc_ref.shape[1]
    n_lat = rxl_ref.shape[1]
    zero = jnp.zeros((1, LANES), F32)
    coeffs(rxc_ref, n_ctx)
    cf, cbk = scan(n_ctx, zero, zero)
    oc_ref[0] = ((h_scr[0, 0:n_ctx, :] + h_scr[1, 0:n_ctx, :]) * _gelu_tanh(rgc_ref[0].astype(F32))).astype(BF16)
    coeffs(rxl_ref, n_lat)
    scan(n_lat, cf, cbk)
    ol_ref[0] = ((h_scr[0] + h_scr[1]) * _gelu_tanh(rgl_ref[0].astype(F32))).astype(BF16)


def _lru_call(rx, rx_c, rg, rg_c, conv_w, conv_b, wa_bd, wx_bd, ba, bx, lam):
    bsz, t, _ = rx.shape
    tc = rx_c.shape[1]
    nch = LRU_WIDTH // LANES
    seq = lambda n: pl.BlockSpec((1, n, LANES), lambda b, j: (b, 0, j))
    vec = lambda r: pl.BlockSpec((r, 1, LANES), lambda b, j: (0, 0, j))
    wsp = pl.BlockSpec((2, 1, LANES, LANES), lambda b, j: (0, j, 0, 0))
    return pl.pallas_call(
        _lru_kernel,
        out_shape=[jax.ShapeDtypeStruct((bsz, t, LRU_WIDTH), BF16), jax.ShapeDtypeStruct((bsz, tc, LRU_WIDTH), BF16)],
        grid=(bsz, nch),
        in_specs=[seq(t), seq(tc), seq(t), seq(tc),
                  pl.BlockSpec((4, LANES), lambda b, j: (0, j)), pl.BlockSpec((1, LANES), lambda b, j: (0, j)),
                  wsp, wsp, vec(2), vec(2), vec(2)],
        out_specs=[seq(t), seq(tc)],
        scratch_shapes=[pltpu.VMEM((2, t, LANES), F32)] * 3,
        compiler_params=_params(2),
        name="lru",
    )(rx, rx_c, rg, rg_c, conv_w, conv_b, wa_bd, wx_bd, ba, bx, lam)


def _merge_kernel(attn_ref, hy_ref, lru_ref, gate_ref, h_ref, g1_ref, wa_ref, wh_ref, wl_ref, wo_ref,
                  n2_ref, sh2_ref, sc2_ref, rw_ref, hn_ref, u2_ref, lt_ref):
    def gate(j):
        return _sigmoid(gate_ref[0, :, j * D_MODEL:(j + 1) * D_MODEL].astype(F32))

    y = gate(0) * _dot(attn_ref[0], wa_ref[...])
    y = y + gate(1) * _dot(hy_ref[0], wh_ref[...])
    y = y + gate(2) * _dot(lru_ref[0], wl_ref[...])
    hn = h_ref[0] + g1_ref[0] * _dot(y.astype(BF16), wo_ref[...])
    hn_ref[0] = hn
    u2 = (_rms(hn) * n2_ref[...] * (1.0 + sc2_ref[0]) + sh2_ref[0]).astype(BF16)
    u2_ref[0] = u2
    lt_ref[0] = _dot_nt(rw_ref[...], u2)


def _merge_call(attn, hyo, lruo, gate, h, g1, wa, wh, wl, wo, n2g, sh2, sc2, rw_t, ctx_rows):
    bsz, t, d = h.shape
    tm = min(512, t)
    row = (lambda b: MOD_ROWS - 8) if ctx_rows else (lambda b: b)
    tok = lambda w: pl.BlockSpec((1, tm, w), lambda b, i: (b, i, 0))
    modspec = pl.BlockSpec((1, 1, d), lambda b, i: (row(b), 0, 0))
    const = lambda shape: pl.BlockSpec(shape, lambda b, i: (0,) * len(shape))
    return pl.pallas_call(
        _merge_kernel,
        out_shape=[jax.ShapeDtypeStruct((bsz, t, d), F32), jax.ShapeDtypeStruct((bsz, t, d), BF16),
                   jax.ShapeDtypeStruct((bsz, N_EXPERTS, t), F32)],
        grid=(bsz, t // tm),
        in_specs=[tok(Q_WIDTH), tok(HY_WIDTH), tok(LRU_WIDTH), tok(GATE_WIDTH), tok(d), modspec,
                  const((Q_WIDTH, d)), const((HY_WIDTH, d)), const((LRU_WIDTH, d)), const((d, d)),
                  const((1, d)), modspec, modspec, const((N_EXPERTS, d))],
        out_specs=[tok(d), tok(d), pl.BlockSpec((1, N_EXPERTS, tm), lambda b, i: (b, 0, i))],
        compiler_params=_params(2),
        name="merge",
    )(attn, hyo, lruo, gate, h, g1, wa, wh, wl, wo, n2g, sh2, sc2, rw_t)


def _router_kernel(lt_ref, slot_ref, slotc_ref, g_ref, coff_ref, *, cap):
    lg = lt_ref[0]
    n_e, t = lg.shape
    ex = jnp.exp(lg - lg.max(axis=0, keepdims=True))
    aff = ex / ex.sum(axis=0, keepdims=True)
    key = pltpu.bitcast(aff, I32)
    capf = float(cap)

    def count(mask):
        return jnp.where(mask, 1.0, 0.0).sum(axis=1, keepdims=True)

    def vbody(i, thr):
        cand = thr | lax.shift_left(jnp.int32(1), 29 - i)
        return jnp.where(count(key >= cand) >= capf, cand, thr)

    thr = lax.fori_loop(0, 30, vbody, jnp.zeros((n_e, 1), I32))
    gt = key > thr
    eq = key == thr
    need = capf - count(gt)
    idx = lax.broadcasted_iota(I32, (n_e, t), 1)
    nbits = t.bit_length() - 1

    def ibody(i, lo):
        cand = lo | lax.shift_left(jnp.int32(1), nbits - 1 - i)
        return jnp.where(count(eq & (idx < cand)) < need, cand, lo)

    last = lax.fori_loop(0, nbits, ibody, jnp.zeros((n_e, 1), I32))
    sel = gt | (eq & (idx <= last))
    self32 = jnp.where(sel, 1.0, 0.0)
    g_ref[0] = jnp.where(sel, aff, 0.0)

    r_i = lax.broadcasted_iota(I32, (LANES, LANES), 0)
    c_i = lax.broadcasted_iota(I32, (LANES, LANES), 1)
    tri = jnp.where(r_i <= c_i, 1.0, 0.0).astype(BF16)
    eye = jnp.where(r_i == c_i, 1.0, 0.0).astype(BF16)
    off = jnp.zeros((n_e, 1), F32)
    n_units = t // LANES
    coff_ref[0] = jnp.zeros((n_e, LANES), I32)
    for c in range(n_units):
        sl = slice(c * LANES, (c + 1) * LANES)
        coff_ref[0, :, c:c + 1] = off.astype(I32)
        xs = self32[:, sl]
        inc = _dot(xs.astype(BF16), tri)
        slot1 = jnp.where(sel[:, sl], inc - xs + off + 1.0, 0.0)
        off = off + inc[:, LANES - 1:LANES]
        slot_ref[0, :, sl] = slot1.astype(I32) - 1
        hi = jnp.floor(slot1 * (1.0 / 16.0))
        lo = slot1 - 16.0 * hi
        col = 16.0 * _dot_nt(eye, hi.astype(BF16)) + _dot_nt(eye, lo.astype(BF16))
        slotc_ref[0, sl, :] = col.astype(I32) - 1
    coff_ref[0, :, n_units:n_units + 1] = off.astype(I32)


def _router_call(logits_t, cap):
    bsz, n_e, t = logits_t.shape
    assert t % LANES == 0 and t // LANES < LANES
    row = pl.BlockSpec((1, n_e, t), lambda b: (b, 0, 0))
    slot_row, slot_col, g_row, coff = pl.pallas_call(
        functools.partial(_router_kernel, cap=cap),
        out_shape=[jax.ShapeDtypeStruct((bsz, n_e, t), I32), jax.ShapeDtypeStruct((bsz, t, n_e), I32),
                   jax.ShapeDtypeStruct((bsz, n_e, t), F32), jax.ShapeDtypeStruct((bsz, n_e, LANES), I32)],
        grid=(bsz,),
        in_specs=[row],
        out_specs=[row, pl.BlockSpec((1, t, n_e), lambda b: (b, 0, 0)), row,
                   pl.BlockSpec((1, n_e, LANES), lambda b: (b, 0, 0))],
        compiler_params=_params(1),
        name="router",
    )(logits_t)
    return slot_row, slot_col, g_row, coff[:, :, :t // LANES + 1].reshape(-1)


MOE_GATHER_TOKENS = 256
MOE_SLOT_ROWS = 128
MOE_SCATTER_WINDOW = 128
MOE_SCATTER_SLOTS = 256


def _log2(n):
    assert n & (n - 1) == 0
    return n.bit_length() - 1


def _expert_kernel(coff_ref, u_ref, slot_ref, g_ref, w1_ref, w3_ref, w2_ref, y_ref, xg_scr, gs_scr, *, tkg, sb,
                   n_units):
    cap = xg_scr.shape[0]
    n_chunks = u_ref.shape[1] // tkg
    base = (pl.program_id(0) * pl.num_programs(1) + pl.program_id(1)) * (n_units + 1)
    upc = tkg // LANES
    rid = lax.broadcasted_iota(I32, (sb, tkg), 0)
    xg_scr[...] = jnp.zeros(xg_scr.shape, F32)
    gs_scr[...] = jnp.zeros(gs_scr.shape, F32)

    def gather(match, r0, c_tokens, g_row):
        xg_scr[pl.ds(r0, sb), :] += _dot(jnp.where(match, 1.0, 0.0).astype(BF16), u_ref[0, c_tokens, :])
        gs_scr[pl.ds(r0, sb), :] += jnp.where(match, g_row, 0.0).sum(axis=1, keepdims=True)

    def window_start(c):
        lo = coff_ref[base + c * upc]
        return jnp.minimum(lo & (-SUBLANES), cap - sb)

    for c in range(n_chunks):
        r0 = pl.multiple_of(window_start(c), SUBLANES)
        gather(slot_ref[0, 0, c:c + 1, :] - r0 == rid, r0, slice(c * tkg, (c + 1) * tkg), g_ref[0, 0, c:c + 1, :])

    def overflow(c, carry):
        covered = window_start(c) + sb
        hi = coff_ref[base + (c + 1) * upc]
        first = lax.shift_right_logical(covered, _log2(sb))
        stop = jnp.where(hi > covered, lax.shift_right_logical(hi + (sb - 1), _log2(sb)), first)

        def block(j, carry2):
            b0 = pl.multiple_of(j * sb, sb)
            srow = slot_ref[0, 0, pl.ds(c, 1), :]
            gather((srow - b0 == rid) & (srow >= covered), b0, pl.ds(pl.multiple_of(c * tkg, tkg), tkg),
                   g_ref[0, 0, pl.ds(c, 1), :])
            return carry2

        return lax.fori_loop(first, stop, block, carry)

    lax.fori_loop(0, n_chunks, overflow, 0)
    xb = xg_scr[...].astype(BF16)
    hid = (_silu(_dot(xb, w1_ref[0])) * _dot(xb, w3_ref[0])).astype(BF16)
    y_ref[0, 0] = (_dot(hid, w2_ref[0]) * gs_scr[...]).astype(BF16)


def _expert_call(u2, slot_row, g_row, coff, w1, w3, w2, cap):
    bsz, t, d = u2.shape
    n_e = w1.shape[0]
    tkg = min(MOE_GATHER_TOKENS, t)
    sb = min(MOE_SLOT_ROWS, cap)
    wspec = pl.BlockSpec((1, d, d), lambda b, e, co: (e, 0, 0))
    chunks = pl.BlockSpec((1, 1, t // tkg, tkg), lambda b, e, co: (b, e, 0, 0))
    grid_spec = pltpu.PrefetchScalarGridSpec(
        num_scalar_prefetch=1,
        grid=(bsz, n_e),
        in_specs=[pl.BlockSpec((1, t, d), lambda b, e, co: (b, 0, 0)),
                  chunks, chunks, wspec, wspec, wspec],
        out_specs=pl.BlockSpec((1, 1, cap, d), lambda b, e, co: (b, e, 0, 0)),
        scratch_shapes=[pltpu.VMEM((cap, d), F32), pltpu.VMEM((cap, 1), F32)],
    )
    return pl.pallas_call(
        functools.partial(_expert_kernel, tkg=tkg, sb=sb, n_units=t // LANES),
        out_shape=jax.ShapeDtypeStruct((bsz, n_e, cap, d), BF16),
        grid_spec=grid_spec,
        compiler_params=_params(2),
        name="expert",
    )(coff, u2, slot_row.reshape(bsz, n_e, t // tkg, tkg), g_row.reshape(bsz, n_e, t // tkg, tkg), w1, w3, w2)


def _expert_call_folded(u2, slot_row, g_row, coff, w1, w3, w2, cap):
    bsz, t, d = u2.shape
    n_e = slot_row.shape[1]
    n_units = t // LANES
    first = (jnp.arange(bsz, dtype=I32) * cap)[:, None, None]
    slot_all = jnp.where(slot_row >= 0, slot_row + first, -1).transpose(1, 0, 2).reshape(1, n_e, bsz * t)
    coff_all = (coff.reshape(bsz, n_e, n_units + 1)[:, :, :n_units] + first).transpose(1, 0, 2)
    coff_all = jnp.concatenate([coff_all.reshape(n_e, bsz * n_units), jnp.full((n_e, 1), bsz * cap, I32)], axis=1)
    g_all = g_row.transpose(1, 0, 2).reshape(1, n_e, bsz * t)
    y = _expert_call(u2.reshape(1, bsz * t, d), slot_all, g_all, coff_all.reshape(-1), w1, w3, w2, bsz * cap)
    return y.reshape(n_e, bsz, cap, d).transpose(1, 0, 2, 3)


def _scatter_kernel(coff_ref, h_ref, y_ref, slotc_ref, g2_ref, o_ref, acc_scr, ystack, *, win, group, kb, n_units):
    tk = h_ref.shape[1]
    cap = y_ref.shape[2]
    upt = tk // LANES
    i = pl.program_id(1)
    lane = lax.broadcasted_iota(I32, (tk, win), 1)
    bf16_rows = 2 * SUBLANES

    def span(e):
        base = (pl.program_id(0) * N_EXPERTS + e) * (n_units + 1)
        lo = coff_ref[base + i * upt]
        hi = coff_ref[base + (i + 1) * upt]
        return jnp.minimum(lo & (-bf16_rows), cap - win), hi

    acc = jnp.zeros((tk, D_MODEL), F32)
    for p in range(N_EXPERTS // group):
        hots = []
        for j in range(group):
            e = p * group + j
            r0 = pl.multiple_of(span(e)[0], bf16_rows)
            ystack[p, j * win:(j + 1) * win, :] = y_ref[0, e, pl.ds(r0, win), :]
            hots.append(jnp.where(slotc_ref[0, :, e:e + 1] - r0 == lane, 1.0, 0.0).astype(BF16))
        onehot = hots[0] if group == 1 else jnp.concatenate(hots, axis=1)
        acc = acc + _dot(onehot, ystack[p])
    acc_scr[...] = acc

    lane_kb = lax.broadcasted_iota(I32, (tk, kb), 1)
    for e in range(N_EXPERTS):
        r0, hi = span(e)
        covered = r0 + win
        first = lax.shift_right_logical(covered, _log2(kb))
        stop = jnp.where(hi > covered, lax.shift_right_logical(hi + (kb - 1), _log2(kb)), first)

        def block(j, carry, e=e, covered=covered):
            b0 = pl.multiple_of(j * kb, kb)
            scol = slotc_ref[0, :, e:e + 1]
            onehot = jnp.where((scol - b0 == lane_kb) & (scol >= covered), 1.0, 0.0).astype(BF16)
            acc_scr[...] += _dot(onehot, y_ref[0, e, pl.ds(b0, kb), :])
            return carry

        lax.fori_loop(first, stop, block, 0)
    o_ref[0] = h_ref[0] + g2_ref[0] * acc_scr[...]


def _scatter_call(h, y, slot_col, coff, g2, cap, ctx_rows):
    bsz, t, d = h.shape
    n_e = y.shape[1]
    tk = min(512, t)
    win = min(MOE_SCATTER_WINDOW, cap)
    group = 2 if win == LANES else 1
    kb = min(MOE_SCATTER_SLOTS, cap)
    row = (lambda b: MOD_ROWS - 8) if ctx_rows else (lambda b: b)
    tile = lambda w: pl.BlockSpec((1, tk, w), lambda b, i, co: (b, i, 0))
    grid_spec = pltpu.PrefetchScalarGridSpec(
        num_scalar_prefetch=1,
        grid=(bsz, t // tk),
        in_specs=[tile(d),
                  pl.BlockSpec((1, n_e, cap, d), lambda b, i, co: (b, 0, 0, 0), pipeline_mode=pl.Buffered(1)),
                  tile(n_e), pl.BlockSpec((1, 1, d), lambda b, i, co: (row(b), 0, 0))],
        out_specs=tile(d),
        scratch_shapes=[pltpu.VMEM((tk, d), F32), pltpu.VMEM((n_e // group, group * win, d), BF16)],
    )
    return pl.pallas_call(
        functools.partial(_scatter_kernel, win=win, group=group, kb=kb, n_units=t // LANES),
        out_shape=jax.ShapeDtypeStruct((bsz, t, d), F32),
        grid_spec=grid_spec,
        compiler_params=_params(2),
        name="moe_scatter",
    )(coff, h, y, slot_col, g2)


def _final_norm_kernel(h_ref, g_ref, o_ref):
    o_ref[0] = _rms(h_ref[0]) * g_ref[...]


def _final_norm_call(h, g):
    bsz, t, d = h.shape
    tm = min(512, t)
    tile = pl.BlockSpec((1, tm, d), lambda b, i: (b, i, 0))
    return pl.pallas_call(
        _final_norm_kernel,
        out_shape=jax.ShapeDtypeStruct((bsz, t, d), F32),
        grid=(bsz, t // tm),
        in_specs=[tile, pl.BlockSpec((1, d), lambda b, i: (0, 0))],
        out_specs=tile,
        compiler_params=_params(2),
        name="final_norm",
    )(h, g)


def _rope_tables(n_lat, n_ctx):
    rows = n_lat // GRID_W
    row = jnp.repeat(jnp.arange(rows, dtype=F32), GRID_W)
    col = jnp.tile(jnp.arange(GRID_W, dtype=F32), rows)
    inv = jnp.power(ROPE_THETA, -jnp.arange(ROPE_PAIRS_AXIS, dtype=F32) / ROPE_PAIRS_AXIS)
    ang = jnp.concatenate([row[:, None] * inv, col[:, None] * inv], axis=-1)
    cos, sin = jnp.cos(ang), jnp.sin(ang)
    cs = jnp.concatenate([cos, cos], axis=-1)
    sn = jnp.concatenate([-sin, sin], axis=-1)
    return cs, sn, jnp.ones((n_ctx, HEAD_DIM), F32), jnp.zeros((n_ctx, HEAD_DIM), F32)


def _dft_tables(n):
    k = jnp.arange(n // 2, dtype=I32)

    def tables(first):
        m = ((2 * k[:, None] + 1) * (2 * k[None, :] + first)) % (4 * n)
        ang = m.astype(F32) * (2.0 * math.pi / (4 * n))
        return jnp.cos(ang), jnp.sin(ang)

    ce, se = tables(0)
    co, so = tables(1)
    fwd = tuple(x.astype(BF16) for x in (ce, co, se, so))
    inv = tuple(x.T.astype(BF16) for x in (ce, co, -se, -so))
    return fwd, inv


def _filter_features(n):
    t = jnp.linspace(0.0, 1.0, n, dtype=F32)[:, None]
    w = (2.0 * math.pi / n) * jnp.arange(n, dtype=F32)[:, None]
    f = jnp.linspace(1e-4, HY_BANDS - 1, HY_BANDS, dtype=F32)[None, :]
    feats = jnp.concatenate([t, jnp.cos(f * w), -jnp.sin(f * w)], axis=-1)
    return jnp.pad(feats, ((0, 0), (0, LANES - HY_EMB)))


def _pad_to(x, shape):
    return jnp.pad(x, [(0, s - d) for d, s in zip(x.shape, shape)])


def _block_diag_chunks(w):
    per = LANES // LRU_BLOCK
    w = w.reshape(2, LRU_BLOCKS // per, per, LRU_BLOCK, LRU_BLOCK)
    eye = jnp.eye(per, dtype=w.dtype)
    return jnp.einsum('dcpkj,pq->dcpkqj', w, eye).reshape(2, LRU_BLOCKS // per, LANES, LANES)


def kernel(x, c, ctx, c_ctx, mod_w, mod_b, norm1_g, norm2_g, w_in, q_norm_g, k_norm_g, hy_conv_w, hy_conv_b,
           hy_fw1, hy_fb1, hy_fw2, hy_fb2, hy_fw3, hy_freq, hy_bias, lru_conv_w, lru_conv_b, lru_wa, lru_ba,
           lru_wx, lru_bx, lru_lambda, w_attn_out, w_hy_out, w_lru_out, w_out, router_w, exp_w1, exp_w3, exp_w2,
           final_norm_g):
    bsz, n_lat, d = x.shape
    n_ctx = ctx.shape[1]
    depth = mod_w.shape[0]
    assert d == D_MODEL and bsz <= MOD_ROWS - 8 and n_lat % GRID_W == 0

    cs_l, sn_l, cs_c, sn_c = _rope_tables(n_lat, n_ctx)
    dft_l = _dft_tables(n_lat)
    dft_c = _dft_tables(n_ctx)
    feats_l = _filter_features(n_lat)
    feats_c = _filter_features(n_ctx)
    max_decay = math.log(HY_DECAY_TARGET) / HY_FAST_DECAY_PCT
    min_decay = math.log(HY_DECAY_TARGET) / HY_SLOW_DECAY_PCT
    deltas_abs = jnp.abs(jnp.linspace(min_decay, max_decay, HY_WIDTH, dtype=F32))[None, :]
    cap_l = max(1, EC_CAPACITY * n_lat // N_EXPERTS)
    cap_c = max(1, EC_CAPACITY * n_ctx // N_EXPERTS)

    cvec = jnp.zeros((MOD_ROWS, d), F32).at[:bsz].set(c).at[MOD_ROWS - 8].set(c_ctx)
    mod = _mod_call(cvec, mod_w, mod_b)
    mod = mod.reshape(depth, MOD_ROWS, 6, 1, d).transpose(0, 2, 1, 3, 4)

    h, hc = x, ctx
    for l in range(depth):
        last = l == depth - 1
        sh1, sc1, g1, sh2, sc2, g2 = (mod[l, j] for j in range(6))
        w_in_b = w_in[l].astype(BF16)
        n1 = norm1_g[l][None, :]
        n2 = norm2_g[l][None, :]
        qg = q_norm_g[l][None, :]
        kg = k_norm_g[l][None, :]

        q, k, v, hy, rx, rg, gate = _inproj_call(h, sh1, sc1, n1, w_in_b, cs_l, sn_l, qg, kg, False)
        qc, kc, vc, hyc, rxc, rgc, gatec = _inproj_call(hc, sh1, sc1, n1, w_in_b, cs_c, sn_c, qg, kg, True)

        attn = _attn_call(q, [(k, v), (kc, vc)])

        fw1 = _pad_to(hy_fw1[l], (LANES, LANES))
        fb1 = _pad_to(hy_fb1[l][None, :], (1, LANES))
        fw2 = _pad_to(hy_fw2[l], (LANES, LANES))
        fb2 = _pad_to(hy_fb2[l][None, :], (1, LANES))
        fw3 = _pad_to(hy_fw3[l], (LANES, 2 * HY_WIDTH))
        freq = _pad_to(hy_freq[l], (2, LANES))
        hbias = hy_bias[l][None, :]

        def hyena(hy_in, feats, tabs):
            tabs_fwd, tabs_inv = tabs
            hsd = _hy_filter_call(feats, fw1, fb1, fw2, fb2, fw3, freq, deltas_abs)
            kspec = _hy_kspec_call(tabs_fwd, hsd)
            x0, w, we, wo = _hy_pre_call(hy_in, hy_conv_w[l], hy_conv_b[l][None, :])
            us = _hy_fwd_call(tabs_fwd, we, wo, kspec)
            return _hy_inv_call(tabs_inv, us, x0, w, hbias)

        hyo = hyena(hy, feats_l, dft_l)

        lruo, lruoc = _lru_call(rx, rxc, rg, rgc, lru_conv_w[l], lru_conv_b[l][None, :],
                                _block_diag_chunks(lru_wa[l]).astype(BF16), _block_diag_chunks(lru_wx[l]).astype(BF16),
                                lru_ba[l][:, None, :], lru_bx[l][:, None, :], lru_lambda[l][:, None, :])

        wa = w_attn_out[l].astype(BF16)
        wh = w_hy_out[l].astype(BF16)
        wl = w_lru_out[l].astype(BF16)
        wo = w_out[l].astype(BF16)
        rw_t = router_w[l].T.astype(BF16)
        e1 = exp_w1[l].astype(BF16)
        e3 = exp_w3[l].astype(BF16)
        e2 = exp_w2[l].astype(BF16)

        def channel_mix(h_in, attn_in, hy_in, lru_in, gate_in, cap, ctx_rows):
            hn, u2, lt = _merge_call(attn_in, hy_in, lru_in, gate_in, h_in, g1, wa, wh, wl, wo, n2, sh2, sc2, rw_t,
                                     ctx_rows)
            slot_row, slot_col, g_row, coff = _router_call(lt, cap)
            if ctx_rows:
                y = _expert_call_folded(u2, slot_row, g_row, coff, e1, e3, e2, cap)
            else:
                y = _expert_call(u2, slot_row, g_row, coff, e1, e3, e2, cap)
            return _scatter_call(hn, y, slot_col, coff, g2, cap, ctx_rows)

        h = channel_mix(h, attn, hyo, lruo, gate, cap_l, False)
        if not last:
            attn_c = _attn_call(qc, [(kc, vc)])
            hyo_c = hyena(hyc, feats_c, dft_c)
            hc = channel_mix(hc, attn_c, hyo_c, lruoc, gatec, cap_c, True)

    return _final_norm_call(h, final_norm_g[None, :])
```

```python
import functools
import math

import jax
import jax.numpy as jnp
from jax import lax
from jax.experimental import pallas as pl
from jax.experimental.pallas import tpu as pltpu

F32 = jnp.float32
BF16 = jnp.bfloat16
I32 = jnp.int32

D_MODEL = 1024
GRID_W = 64
NORM_EPS = 1e-6
N_HEADS = 8
N_KV_HEADS = 2
HEAD_DIM = 128
GROUP = N_HEADS // N_KV_HEADS
ROPE_PAIRS_AXIS = HEAD_DIM // 4
ROPE_THETA = 10000.0
ATTN_SCALE = HEAD_DIM ** -0.5
Q_PRESCALE = ATTN_SCALE * math.log2(math.e)
Q_WIDTH = N_HEADS * HEAD_DIM
KV_WIDTH = N_KV_HEADS * HEAD_DIM
HY_WIDTH = D_MODEL // 2
HY_BANDS = 16
HY_EMB = 1 + 2 * HY_BANDS
HY_FILTER_HIDDEN = 64
HY_FAST_DECAY_PCT = 0.3
HY_SLOW_DECAY_PCT = 1.5
HY_DECAY_TARGET = 1e-2
LRU_WIDTH = D_MODEL // 2
LRU_BLOCKS = 8
LRU_BLOCK = LRU_WIDTH // LRU_BLOCKS
LRU_C = 8.0
N_BRANCH = 3
N_EXPERTS = 16
EC_CAPACITY = 2
GATE_WIDTH = N_BRANCH * D_MODEL
OFF_Q = 0
OFF_K = OFF_Q + Q_WIDTH
OFF_V = OFF_K + KV_WIDTH
OFF_HY = OFF_V + KV_WIDTH
OFF_RX = OFF_HY + 3 * HY_WIDTH
OFF_RG = OFF_RX + LRU_WIDTH
OFF_GATE = OFF_RG + LRU_WIDTH
IN_TOTAL = OFF_GATE + GATE_WIDTH

LANES = 128
SUBLANES = 8
V7X_VMEM_LIMIT_BYTES = 56 * 1024 * 1024
MOD_ROWS = 24


def _params(n_axes, vmem=V7X_VMEM_LIMIT_BYTES):
    return pltpu.CompilerParams(dimension_semantics=("arbitrary",) * n_axes, vmem_limit_bytes=vmem)


def _dot(a, b):
    return jnp.dot(a, b, preferred_element_type=F32)


def _dot_nt(a, b):
    return lax.dot_general(a, b, (((1,), (1,)), ((), ())), preferred_element_type=F32)


def _split2(x):
    hi = x.astype(BF16)
    lo = (x - hi.astype(F32)).astype(BF16)
    return hi, lo


def _dot3(a, b):
    ah, al = _split2(a)
    bh, bl = _split2(b)
    return _dot(ah, bh) + (_dot(ah, bl) + _dot(al, bh))


def _rms(x):
    return x * lax.rsqrt(jnp.mean(x * x, axis=-1, keepdims=True) + NORM_EPS)


def _sigmoid(x):
    return 0.5 * jnp.tanh(0.5 * x) + 0.5


def _silu(x):
    return x * _sigmoid(x)


def _gelu_tanh(x):
    return 0.5 * x * (1.0 + jnp.tanh(math.sqrt(2.0 / math.pi) * (x + 0.044715 * (x * x * x))))


def _shift_down(x, k):
    row = lax.broadcasted_iota(I32, x.shape, 0)
    return jnp.where(row >= k, pltpu.roll(x, k, 0), 0.0)


def _shift_up(x, k):
    n = x.shape[0]
    row = lax.broadcasted_iota(I32, x.shape, 0)
    return jnp.where(row < n - k, pltpu.roll(x, n - k, 0), 0.0)


def _mod_kernel(c_ref, w_ref, b_ref, o_ref):
    o_ref[0] = _dot3(_silu(c_ref[...]), w_ref[0]) + b_ref[0]


def _mod_call(cvec, mod_w, mod_b):
    depth, d, six_d = mod_w.shape
    tn = 1536
    return pl.pallas_call(
        _mod_kernel,
        out_shape=jax.ShapeDtypeStruct((depth, MOD_ROWS, six_d), F32),
        grid=(depth, six_d // tn),
        in_specs=[
            pl.BlockSpec((MOD_ROWS, d), lambda l, j: (0, 0)),
            pl.BlockSpec((1, d, tn), lambda l, j: (l, 0, j)),
            pl.BlockSpec((1, 1, tn), lambda l, j: (l, 0, j)),
        ],
        out_specs=pl.BlockSpec((1, MOD_ROWS, tn), lambda l, j: (l, 0, j)),
        compiler_params=_params(2),
        name="mod",
    )(cvec, mod_w, mod_b.reshape(depth, 1, six_d))


def _inproj_kernel(h_ref, sh_ref, sc_ref, g_ref, w_ref, cs_ref, sn_ref, qg_ref, kg_ref,
                   q_ref, k_ref, v_ref, hy_ref, rx_ref, rg_ref, gate_ref):
    u = (_rms(h_ref[0]) * g_ref[...] * (1.0 + sc_ref[0]) + sh_ref[0]).astype(BF16)
    cs = cs_ref[...]
    sn = sn_ref[...]

    def normed_rope(z, g):
        r = _rms(z) * g
        return r * cs + pltpu.roll(r, HEAD_DIM // 2, 1) * sn

    zq = _dot(u, w_ref[:, OFF_Q:OFF_Q + Q_WIDTH])
    for hd in range(N_HEADS):
        sl = slice(hd * HEAD_DIM, (hd + 1) * HEAD_DIM)
        q_ref[0, hd] = (normed_rope(zq[:, sl], qg_ref[...]) * Q_PRESCALE).astype(BF16)
    zkv = _dot(u, w_ref[:, OFF_K:OFF_K + 2 * KV_WIDTH])
    for hd in range(N_KV_HEADS):
        sl = slice(hd * HEAD_DIM, (hd + 1) * HEAD_DIM)
        k_ref[0, :, sl] = normed_rope(zkv[:, sl], kg_ref[...]).astype(BF16)
    v_ref[0] = zkv[:, KV_WIDTH:].astype(BF16)
    for j in range(3):
        sl = slice(j * HY_WIDTH, (j + 1) * HY_WIDTH)
        hy_ref[0, :, sl] = _dot(u, w_ref[:, OFF_HY + j * HY_WIDTH:OFF_HY + (j + 1) * HY_WIDTH]).astype(BF16)
    rr = _dot(u, w_ref[:, OFF_RX:OFF_RX + 2 * LRU_WIDTH])
    rx_ref[0] = rr[:, :LRU_WIDTH]
    rg_ref[0] = rr[:, LRU_WIDTH:].astype(BF16)
    for j in range(N_BRANCH):
        sl = slice(j * D_MODEL, (j + 1) * D_MODEL)
        gate_ref[0, :, sl] = _dot(u, w_ref[:, OFF_GATE + j * D_MODEL:OFF_GATE + (j + 1) * D_MODEL]).astype(BF16)


def _inproj_call(h, sh, sc, g, w_bf, cs, sn, qg, kg, ctx_rows):
    bsz, t, d = h.shape
    tm = min(512, t)
    row = (lambda b: MOD_ROWS - 8) if ctx_rows else (lambda b: b)
    tok = lambda w: pl.BlockSpec((1, tm, w), lambda b, i: (b, i, 0))
    modspec = pl.BlockSpec((1, 1, d), lambda b, i: (row(b), 0, 0))
    const = lambda shape: pl.BlockSpec(shape, lambda b, i: (0,) * len(shape))
    outs = [(KV_WIDTH, BF16), (KV_WIDTH, BF16), (3 * HY_WIDTH, BF16), (LRU_WIDTH, F32), (LRU_WIDTH, BF16),
            (GATE_WIDTH, BF16)]
    return pl.pallas_call(
        _inproj_kernel,
        out_shape=[jax.ShapeDtypeStruct((bsz, N_HEADS, t, HEAD_DIM), BF16)]
        + [jax.ShapeDtypeStruct((bsz, t, w), dt) for w, dt in outs],
        grid=(bsz, t // tm),
        in_specs=[
            tok(d), modspec, modspec, const((1, d)),
            pl.BlockSpec((d, IN_TOTAL), lambda b, i: (0, 0), pipeline_mode=pl.Buffered(1)),
            pl.BlockSpec((tm, HEAD_DIM), lambda b, i: (i, 0)),
            pl.BlockSpec((tm, HEAD_DIM), lambda b, i: (i, 0)),
            const((1, HEAD_DIM)), const((1, HEAD_DIM)),
        ],
        out_specs=[pl.BlockSpec((1, N_HEADS, tm, HEAD_DIM), lambda b, i: (b, 0, i, 0))] + [tok(w) for w, _ in outs],
        compiler_params=_params(2),
        name="inproj",
    )(h, sh, sc, g, w_bf, cs, sn, qg, kg)


ATTN_TQ = 512
ATTN_TK = 512
ATTN_STREAMS = 4
NEG_BIG = -1e30


def _attn_kernel(*refs, n_src, tq):
    q_ref = refs[0]
    kv = refs[1:1 + 2 * n_src]
    o_ref, kcat, vext = refs[1 + 2 * n_src:]
    n_keys = kcat.shape[0]

    @pl.when(pl.program_id(2) == 0)
    def _():
        off = 0
        for i in range(n_src):
            n = kv[2 * i].shape[1]
            kcat[off:off + n, :] = kv[2 * i][0]
            vext[off:off + n, 0:HEAD_DIM] = kv[2 * i + 1][0]
            off += n
        vext[:, HEAD_DIM:] = jnp.ones((n_keys, HEAD_DIM), BF16)

    rows = GROUP * tq
    q = q_ref[0].reshape(rows, HEAD_DIM)
    per = rows // ATTN_STREAMS
    qs = [q[i * per:(i + 1) * per] for i in range(ATTN_STREAMS)]
    ms = [jnp.full((per, 1), NEG_BIG, F32) for _ in qs]
    accs = [jnp.zeros((per, 2 * HEAD_DIM), F32) for _ in qs]
    for off in range(0, n_keys, ATTN_TK):
        size = min(ATTN_TK, n_keys - off)
        kk = kcat[off:off + size, :]
        vv = vext[off:off + size, :]
        for i in range(ATTN_STREAMS):
            s = _dot_nt(qs[i], kk)
            m_new = jnp.maximum(ms[i], s.max(axis=-1, keepdims=True))
            p = jnp.exp2(s - m_new).astype(BF16)
            accs[i] = jnp.exp2(ms[i] - m_new) * accs[i] + _dot(p, vv)
            ms[i] = m_new
    out = jnp.concatenate([a[:, :HEAD_DIM] / a[:, HEAD_DIM:] for a in accs], axis=0).astype(BF16)
    for g in range(GROUP):
        o_ref[0, :, g * HEAD_DIM:(g + 1) * HEAD_DIM] = out[g * tq:(g + 1) * tq]


def _attn_call(q, kvs):
    bsz, _, tq_all, _ = q.shape
    tq = min(ATTN_TQ, tq_all)
    gw = GROUP * HEAD_DIM
    in_specs = [pl.BlockSpec((1, GROUP, tq, HEAD_DIM), lambda b, h, i: (b, h, i, 0))]
    args = [q]
    n_keys = 0
    for k, v in kvs:
        tk = k.shape[1]
        n_keys += tk
        spec = pl.BlockSpec((1, tk, HEAD_DIM), lambda b, h, i: (b, 0, h))
        in_specs += [spec, spec]
        args += [k, v]
    return pl.pallas_call(
        functools.partial(_attn_kernel, n_src=len(kvs), tq=tq),
        out_shape=jax.ShapeDtypeStruct((bsz, tq_all, Q_WIDTH), BF16),
        grid=(bsz, N_KV_HEADS, tq_all // tq),
        in_specs=in_specs,
        out_specs=pl.BlockSpec((1, tq, gw), lambda b, h, i: (b, i, h)),
        scratch_shapes=[pltpu.VMEM((n_keys, HEAD_DIM), BF16), pltpu.VMEM((n_keys, 2 * HEAD_DIM), BF16)],
        compiler_params=_params(3),
        name="attn",
    )(*args)


def _hy_pre_kernel(h0_ref, h1_ref, h2_ref, w0_ref, w1_ref, w2_ref, b0_ref, b1_ref, b2_ref,
                   x0_ref, we_ref, wo_ref, w_scr):
    def conv(x_ref, cw_ref, cb_ref):
        x = x_ref[0].astype(F32)
        cw = cw_ref[...]
        return cb_ref[...] + _shift_down(x, 1) * cw[0:1] + x * cw[1:2] + _shift_up(x, 1) * cw[2:3]

    x0_ref[0] = conv(h0_ref, w0_ref, b0_ref).astype(BF16)
    w_scr[...] = conv(h2_ref, w2_ref, b2_ref) * conv(h1_ref, w1_ref, b1_ref)
    half = w_scr.shape[0] // 2
    we_ref[0] = w_scr[pl.ds(0, half, stride=2), :].astype(BF16)
    wo_ref[0] = w_scr[pl.ds(1, half, stride=2), :].astype(BF16)


def _hy_pre_call(hy, conv_w, conv_b):
    bsz, t, _ = hy.shape
    nch = HY_WIDTH // LANES
    xs = lambda part: pl.BlockSpec((1, t, LANES), lambda b, j: (b, 0, part * nch + j))
    ws = lambda part: pl.BlockSpec((3, LANES), lambda b, j: (0, part * nch + j))
    bs = lambda part: pl.BlockSpec((1, LANES), lambda b, j: (0, part * nch + j))
    out = pl.BlockSpec((1, t, LANES), lambda b, j: (b, 0, j))
    outh = pl.BlockSpec((1, t // 2, LANES), lambda b, j: (b, 0, j))
    return pl.pallas_call(
        _hy_pre_kernel,
        out_shape=[jax.ShapeDtypeStruct((bsz, t, HY_WIDTH), BF16),
                   jax.ShapeDtypeStruct((bsz, t // 2, HY_WIDTH), BF16),
                   jax.ShapeDtypeStruct((bsz, t // 2, HY_WIDTH), BF16)],
        grid=(bsz, nch),
        in_specs=[xs(0), xs(1), xs(2), ws(0), ws(1), ws(2), bs(0), bs(1), bs(2)],
        out_specs=[out, outh, outh],
        scratch_shapes=[pltpu.VMEM((t, LANES), F32)],
        compiler_params=_params(2),
        name="hy_pre",
    )(hy, hy, hy, conv_w, conv_w, conv_w, conv_b, conv_b, conv_b)


def _hy_filter_kernel(feat_ref, w1_ref, b1_ref, w2_ref, b2_ref, w3_ref, fr_ref, dl_ref, o_ref):
    feats = feat_ref[...]
    hid = jnp.sin(fr_ref[0:1] * (_dot3(feats, w1_ref[...]) + b1_ref[...]))
    hid = jnp.sin(fr_ref[1:2] * (_dot3(hid, w2_ref[...]) + b2_ref[...]))
    filt = _dot3(hid, w3_ref[...])
    decay = jnp.exp(-feats[:, 0:1] * dl_ref[...])
    h_fwd = filt[:, :HY_WIDTH] * decay
    h_bwd = filt[:, HY_WIDTH:] * decay
    o_ref[:, :HY_WIDTH] = h_fwd + h_bwd
    o_ref[:, HY_WIDTH:] = h_bwd - h_fwd


def _hy_filter_call(feats, fw1, fb1, fw2, fb2, fw3, freq, deltas_abs):
    n = feats.shape[0]
    tn = min(512, n)
    const = lambda shape: pl.BlockSpec(shape, lambda i: (0,) * len(shape))
    return pl.pallas_call(
        _hy_filter_kernel,
        out_shape=jax.ShapeDtypeStruct((n, 2 * HY_WIDTH), F32),
        grid=(n // tn,),
        in_specs=[pl.BlockSpec((tn, LANES), lambda i: (i, 0)), const((LANES, LANES)), const((1, LANES)),
                  const((LANES, LANES)), const((1, LANES)), const((LANES, 2 * HY_WIDTH)), const((2, LANES)),
                  const((1, HY_WIDTH))],
        out_specs=pl.BlockSpec((tn, 2 * HY_WIDTH), lambda i: (i, 0)),
        compiler_params=_params(1),
        name="hy_filter",
    )(feats, fw1, fb1, fw2, fb2, fw3, freq, deltas_abs)


def _hy_kspec_kernel(ce_ref, co_ref, se_ref, so_ref, he_ref, ho_ref, krl_ref, kil_ref, krh_ref, kih_ref, *, scale):
    def branch(tab_e, tab_o, cols):
        he_hi, he_lo = _split2(he_ref[:, cols])
        ho_hi, ho_lo = _split2(ho_ref[:, cols])
        pe = _dot(tab_e[...], he_hi) + _dot(tab_e[...], he_lo)
        po = _dot(tab_o[...], ho_hi) + _dot(tab_o[...], ho_lo)
        return pe, po

    pc, qc = branch(ce_ref, co_ref, slice(0, HY_WIDTH))
    ps, qs = branch(se_ref, so_ref, slice(HY_WIDTH, 2 * HY_WIDTH))
    krl_ref[...] = (pc + qc) * scale
    krh_ref[...] = (pc - qc) * scale
    kil_ref[...] = (ps + qs) * scale
    kih_ref[...] = (qs - ps) * scale


def _hy_kspec_call(tabs, hsd):
    n = hsd.shape[0]
    half = n // 2
    tf = min(512, half)
    tab = pl.BlockSpec((tf, half), lambda i: (i, 0))
    out = pl.BlockSpec((tf, HY_WIDTH), lambda i: (i, 0))
    return pl.pallas_call(
        functools.partial(_hy_kspec_kernel, scale=1.0 / n),
        out_shape=[jax.ShapeDtypeStruct((half, HY_WIDTH), F32)] * 4,
        grid=(half // tf,),
        in_specs=[tab, tab, tab, tab, pl.BlockSpec((half, 2 * HY_WIDTH), lambda i: (0, 0)),
                  pl.BlockSpec((half, 2 * HY_WIDTH), lambda i: (0, 0))],
        out_specs=[out] * 4,
        compiler_params=_params(1),
        name="hy_kspec",
    )(*tabs, hsd[0::2], hsd[1::2])


def _hy_fwd_kernel(ce_ref, co_ref, se_ref, so_ref, we_ref, wo_ref, krl_ref, kil_ref, krh_ref, kih_ref,
                   urp_ref, urm_ref, uim_ref, uip_ref):
    we = we_ref[0]
    wo = wo_ref[0]
    pc = _dot(ce_ref[...], we)
    qc = _dot(co_ref[...], wo)
    ps = _dot(se_ref[...], we)
    qs = _dot(so_ref[...], wo)
    a_lo, a_hi = pc + qc, pc - qc
    b_lo, b_hi = ps + qs, qs - ps
    krl, kil, krh, kih = krl_ref[...], kil_ref[...], krh_ref[...], kih_ref[...]
    yre_lo = a_lo * krl + b_lo * kil
    yim_lo = a_lo * kil - b_lo * krl
    yre_hi = a_hi * krh + b_hi * kih
    yim_hi = a_hi * kih - b_hi * krh
    urp_ref[0] = (yre_lo + yre_hi).astype(BF16)
    urm_ref[0] = (yre_lo - yre_hi).astype(BF16)
    uim_ref[0] = (yim_lo - yim_hi).astype(BF16)
    uip_ref[0] = (yim_lo + yim_hi).astype(BF16)


def _hy_fwd_call(tabs, we, wo, kspec):
    bsz, half, _ = we.shape
    tf = min(512, half)
    tab = pl.BlockSpec((tf, half), lambda i, b: (i, 0))
    sig = pl.BlockSpec((1, half, HY_WIDTH), lambda i, b: (b, 0, 0))
    kk = pl.BlockSpec((tf, HY_WIDTH), lambda i, b: (i, 0))
    out = pl.BlockSpec((1, tf, HY_WIDTH), lambda i, b: (b, i, 0))
    return pl.pallas_call(
        _hy_fwd_kernel,
        out_shape=[jax.ShapeDtypeStruct((bsz, half, HY_WIDTH), BF16)] * 4,
        grid=(half // tf, bsz),
        in_specs=[tab, tab, tab, tab, sig, sig, kk, kk, kk, kk],
        out_specs=[out] * 4,
        compiler_params=_params(2),
        name="hy_fwd",
    )(*tabs, we, wo, *kspec)


def _hy_inv_kernel(cte_ref, cto_ref, ste_ref, sto_ref, urp_ref, urm_ref, uim_ref, uip_ref, x0_ref, we_ref,
                   wo_ref, bias_ref, o_ref, y_scr):
    tt = cte_ref.shape[0]
    bias = bias_ref[...]
    y_even = _dot(cte_ref[...], urp_ref[0]) + _dot(ste_ref[...], uim_ref[0]) + we_ref[0].astype(F32) * bias
    y_odd = _dot(cto_ref[...], urm_ref[0]) + _dot(sto_ref[...], uip_ref[0]) + wo_ref[0].astype(F32) * bias
    for j in range(HY_WIDTH // LANES):
        sl = slice(j * LANES, (j + 1) * LANES)
        y_scr[j, pl.ds(0, tt, stride=2), :] = y_even[:, sl]
        y_scr[j, pl.ds(1, tt, stride=2), :] = y_odd[:, sl]
        o_ref[0, :, sl] = (x0_ref[0, :, sl].astype(F32) * y_scr[j]).astype(BF16)


def _hy_inv_call(tabs_t, us, x0, we, wo, bias):
    bsz, n, _ = x0.shape
    half = n // 2
    tt = min(256, half)
    tab = pl.BlockSpec((tt, half), lambda i, b: (i, 0))
    full = pl.BlockSpec((1, half, HY_WIDTH), lambda i, b: (b, 0, 0))
    tile = pl.BlockSpec((1, 2 * tt, HY_WIDTH), lambda i, b: (b, i, 0))
    htile = pl.BlockSpec((1, tt, HY_WIDTH), lambda i, b: (b, i, 0))
    return pl.pallas_call(
        _hy_inv_kernel,
        out_shape=jax.ShapeDtypeStruct((bsz, n, HY_WIDTH), BF16),
        grid=(half // tt, bsz),
        in_specs=[tab, tab, tab, tab, full, full, full, full, tile, htile, htile,
                  pl.BlockSpec((1, HY_WIDTH), lambda i, b: (0, 0))],
        out_specs=tile,
        scratch_shapes=[pltpu.VMEM((HY_WIDTH // LANES, 2 * tt, LANES), F32)],
        compiler_params=_params(2),
        name="hy_inv",
    )(*tabs_t, *us, x0, we, wo, bias)


LRU_SCAN_UNROLL = 4


def _lru_kernel(rxl_ref, rxc_ref, rgl_ref, rgc_ref, cw_ref, cb_ref, wa_ref, wx_ref, ba_ref, bx_ref, lam_ref,
                ol_ref, oc_ref, a_scr, b_scr, h_scr):
    cw = cw_ref[...]
    cb = cb_ref[...]
    row8 = lax.broadcasted_iota(I32, (SUBLANES, LANES), 0)

    def coeffs(x_ref, n):
        x = x_ref[0]
        xc = (cb + _shift_down(x, 2) * cw[0:1] + _shift_down(x, 1) * cw[1:2] + x * cw[2:3]
              + _shift_up(x, 1) * cw[3:4])
        xb = xc.astype(BF16)
        for d in range(2):
            r = _sigmoid(_dot(xb, wa_ref[d, 0]) + ba_ref[d])
            i = _sigmoid(_dot(xb, wx_ref[d, 0]) + bx_ref[d])
            nl = -lam_ref[d]
            softplus = jnp.maximum(nl, 0.0) + jnp.log(1.0 + jnp.exp(-jnp.abs(nl)))
            log_a = (-LRU_C) * r * softplus
            a = jnp.exp(log_a)
            a_scr[d, 0:n, :] = a
            b_scr[d, 0:n, :] = jnp.sqrt(1.0 - a * a) * i * xc

    def scan(n, carry_f, carry_b):
        ng = n // SUBLANES

        def body(g, carry):
            cf, cbk = carry
            of = pl.multiple_of(g * SUBLANES, SUBLANES)
            ob = pl.multiple_of((ng - 1 - g) * SUBLANES, SUBLANES)
            a = a_scr[0, pl.ds(of, SUBLANES), :]
            b = b_scr[0, pl.ds(of, SUBLANES), :]
            a2 = a_scr[1, pl.ds(ob, SUBLANES), :]
            b2 = b_scr[1, pl.ds(ob, SUBLANES), :]
            for k in (1, 2, 4):
                keep = row8 >= k
                b = a * jnp.where(keep, pltpu.roll(b, k, 0), 0.0) + b
                a = a * jnp.where(keep, pltpu.roll(a, k, 0), 1.0)
                keep2 = row8 < SUBLANES - k
                b2 = a2 * jnp.where(keep2, pltpu.roll(b2, SUBLANES - k, 0), 0.0) + b2
                a2 = a2 * jnp.where(keep2, pltpu.roll(a2, SUBLANES - k, 0), 1.0)
            hf = a * cf + b
            hb = a2 * cbk + b2
            h_scr[0, pl.ds(of, SUBLANES), :] = hf
            h_scr[1, pl.ds(ob, SUBLANES), :] = hb
            return hf[SUBLANES - 1:SUBLANES, :], hb[0:1, :]

        return lax.fori_loop(0, ng, body, (carry_f, carry_b), unroll=LRU_SCAN_UNROLL)

    n_ctx = rxc_ref.shape[1]
    n_lat = rxl_ref.shape[1]
    zero = jnp.zeros((1, LANES), F32)
    coeffs(rxc_ref, n_ctx)
    cf, cbk = scan(n_ctx, zero, zero)
    oc_ref[0] = ((h_scr[0, 0:n_ctx, :] + h_scr[1, 0:n_ctx, :]) * _gelu_tanh(rgc_ref[0].astype(F32))).astype(BF16)
    coeffs(rxl_ref, n_lat)
    scan(n_lat, cf, cbk)
    ol_ref[0] = ((h_scr[0] + h_scr[1]) * _gelu_tanh(rgl_ref[0].astype(F32))).astype(BF16)


def _lru_call(rx, rx_c, rg, rg_c, conv_w, conv_b, wa_bd, wx_bd, ba, bx, lam):
    bsz, t, _ = rx.shape
    tc = rx_c.shape[1]
    nch = LRU_WIDTH // LANES
    seq = lambda n: pl.BlockSpec((1, n, LANES), lambda b, j: (b, 0, j))
    vec = lambda r: pl.BlockSpec((r, 1, LANES), lambda b, j: (0, 0, j))
    wsp = pl.BlockSpec((2, 1, LANES, LANES), lambda b, j: (0, j, 0, 0))
    return pl.pallas_call(
        _lru_kernel,
        out_shape=[jax.ShapeDtypeStruct((bsz, t, LRU_WIDTH), BF16), jax.ShapeDtypeStruct((bsz, tc, LRU_WIDTH), BF16)],
        grid=(bsz, nch),
        in_specs=[seq(t), seq(tc), seq(t), seq(tc),
                  pl.BlockSpec((4, LANES), lambda b, j: (0, j)), pl.BlockSpec((1, LANES), lambda b, j: (0, j)),
                  wsp, wsp, vec(2), vec(2), vec(2)],
        out_specs=[seq(t), seq(tc)],
        scratch_shapes=[pltpu.VMEM((2, t, LANES), F32)] * 3,
        compiler_params=_params(2),
        name="lru",
    )(rx, rx_c, rg, rg_c, conv_w, conv_b, wa_bd, wx_bd, ba, bx, lam)


def _merge_kernel(attn_ref, hy_ref, lru_ref, gate_ref, h_ref, g1_ref, wa_ref, wh_ref, wl_ref, wo_ref,
                  n2_ref, sh2_ref, sc2_ref, rw_ref, hn_ref, u2_ref, lt_ref):
    def gate(j):
        return _sigmoid(gate_ref[0, :, j * D_MODEL:(j + 1) * D_MODEL].astype(F32))

    y = gate(0) * _dot(attn_ref[0], wa_ref[...])
    y = y + gate(1) * _dot(hy_ref[0], wh_ref[...])
    y = y + gate(2) * _dot(lru_ref[0], wl_ref[...])
    hn = h_ref[0] + g1_ref[0] * _dot(y.astype(BF16), wo_ref[...])
    hn_ref[0] = hn
    u2 = (_rms(hn) * n2_ref[...] * (1.0 + sc2_ref[0]) + sh2_ref[0]).astype(BF16)
    u2_ref[0] = u2
    lt_ref[0] = _dot_nt(rw_ref[...], u2)


def _merge_call(attn, hyo, lruo, gate, h, g1, wa, wh, wl, wo, n2g, sh2, sc2, rw_t, ctx_rows):
    bsz, t, d = h.shape
    tm = min(512, t)
    row = (lambda b: MOD_ROWS - 8) if ctx_rows else (lambda b: b)
    tok = lambda w: pl.BlockSpec((1, tm, w), lambda b, i: (b, i, 0))
    modspec = pl.BlockSpec((1, 1, d), lambda b, i: (row(b), 0, 0))
    const = lambda shape: pl.BlockSpec(shape, lambda b, i: (0,) * len(shape))
    return pl.pallas_call(
        _merge_kernel,
        out_shape=[jax.ShapeDtypeStruct((bsz, t, d), F32), jax.ShapeDtypeStruct((bsz, t, d), BF16),
                   jax.ShapeDtypeStruct((bsz, N_EXPERTS, t), F32)],
        grid=(bsz, t // tm),
        in_specs=[tok(Q_WIDTH), tok(HY_WIDTH), tok(LRU_WIDTH), tok(GATE_WIDTH), tok(d), modspec,
                  const((Q_WIDTH, d)), const((HY_WIDTH, d)), const((LRU_WIDTH, d)), const((d, d)),
                  const((1, d)), modspec, modspec, const((N_EXPERTS, d))],
        out_specs=[tok(d), tok(d), pl.BlockSpec((1, N_EXPERTS, tm), lambda b, i: (b, 0, i))],
        compiler_params=_params(2),
        name="merge",
    )(attn, hyo, lruo, gate, h, g1, wa, wh, wl, wo, n2g, sh2, sc2, rw_t)


def _router_kernel(lt_ref, slot_ref, slotc_ref, g_ref, coff_ref, *, cap):
    lg = lt_ref[0]
    n_e, t = lg.shape
    ex = jnp.exp(lg - lg.max(axis=0, keepdims=True))
    aff = ex / ex.sum(axis=0, keepdims=True)
    key = pltpu.bitcast(aff, I32)
    capf = float(cap)

    def count(mask):
        return jnp.where(mask, 1.0, 0.0).sum(axis=1, keepdims=True)

    def vbody(i, thr):
        cand = thr | lax.shift_left(jnp.int32(1), 29 - i)
        return jnp.where(count(key >= cand) >= capf, cand, thr)

    thr = lax.fori_loop(0, 30, vbody, jnp.zeros((n_e, 1), I32))
    gt = key > thr
    eq = key == thr
    need = capf - count(gt)
    idx = lax.broadcasted_iota(I32, (n_e, t), 1)
    nbits = t.bit_length() - 1

    def ibody(i, lo):
        cand = lo | lax.shift_left(jnp.int32(1), nbits - 1 - i)
        return jnp.where(count(eq & (idx < cand)) < need, cand, lo)

    last = lax.fori_loop(0, nbits, ibody, jnp.zeros((n_e, 1), I32))
    sel = gt | (eq & (idx <= last))
    self32 = jnp.where(sel, 1.0, 0.0)
    g_ref[0] = jnp.where(sel, aff, 0.0)

    r_i = lax.broadcasted_iota(I32, (LANES, LANES), 0)
    c_i = lax.broadcasted_iota(I32, (LANES, LANES), 1)
    tri = jnp.where(r_i <= c_i, 1.0, 0.0).astype(BF16)
    eye = jnp.where(r_i == c_i, 1.0, 0.0).astype(BF16)
    off = jnp.zeros((n_e, 1), F32)
    n_units = t // LANES
    coff_ref[0] = jnp.zeros((n_e, LANES), I32)
    for c in range(n_units):
        sl = slice(c * LANES, (c + 1) * LANES)
        coff_ref[0, :, c:c + 1] = off.astype(I32)
        xs = self32[:, sl]
        inc = _dot(xs.astype(BF16), tri)
        slot1 = jnp.where(sel[:, sl], inc - xs + off + 1.0, 0.0)
        off = off + inc[:, LANES - 1:LANES]
        slot_ref[0, :, sl] = slot1.astype(I32) - 1
        hi = jnp.floor(slot1 * (1.0 / 16.0))
        lo = slot1 - 16.0 * hi
        col = 16.0 * _dot_nt(eye, hi.astype(BF16)) + _dot_nt(eye, lo.astype(BF16))
        slotc_ref[0, sl, :] = col.astype(I32) - 1
    coff_ref[0, :, n_units:n_units + 1] = off.astype(I32)


def _router_call(logits_t, cap):
    bsz, n_e, t = logits_t.shape
    assert t % LANES == 0 and t // LANES < LANES
    row = pl.BlockSpec((1, n_e, t), lambda b: (b, 0, 0))
    slot_row, slot_col, g_row, coff = pl.pallas_call(
        functools.partial(_router_kernel, cap=cap),
        out_shape=[jax.ShapeDtypeStruct((bsz, n_e, t), I32), jax.ShapeDtypeStruct((bsz, t, n_e), I32),
                   jax.ShapeDtypeStruct((bsz, n_e, t), F32), jax.ShapeDtypeStruct((bsz, n_e, LANES), I32)],
        grid=(bsz,),
        in_specs=[row],
        out_specs=[row, pl.BlockSpec((1, t, n_e), lambda b: (b, 0, 0)), row,
                   pl.BlockSpec((1, n_e, LANES), lambda b: (b, 0, 0))],
        compiler_params=_params(1),
        name="router",
    )(logits_t)
    return slot_row, slot_col, g_row, coff[:, :, :t // LANES + 1].reshape(-1)


MOE_GATHER_TOKENS = 256
MOE_SLOT_ROWS = 128
MOE_SCATTER_WINDOW = 128
MOE_SCATTER_SLOTS = 256


def _log2(n):
    assert n & (n - 1) == 0
    return n.bit_length() - 1


def _expert_kernel(coff_ref, u_ref, slot_ref, g_ref, w1_ref, w3_ref, w2_ref, y_ref, xg_scr, gs_scr, *, tkg, sb,
                   n_units):
    cap = xg_scr.shape[0]
    n_chunks = u_ref.shape[1] // tkg
    base = (pl.program_id(0) * pl.num_programs(1) + pl.program_id(1)) * (n_units + 1)
    upc = tkg // LANES
    rid = lax.broadcasted_iota(I32, (sb, tkg), 0)
    xg_scr[...] = jnp.zeros(xg_scr.shape, F32)
    gs_scr[...] = jnp.zeros(gs_scr.shape, F32)

    def gather(match, r0, c_tokens, g_row):
        xg_scr[pl.ds(r0, sb), :] += _dot(jnp.where(match, 1.0, 0.0).astype(BF16), u_ref[0, c_tokens, :])
        gs_scr[pl.ds(r0, sb), :] += jnp.where(match, g_row, 0.0).sum(axis=1, keepdims=True)

    def window_start(c):
        lo = coff_ref[base + c * upc]
        return jnp.minimum(lo & (-SUBLANES), cap - sb)

    for c in range(n_chunks):
        r0 = pl.multiple_of(window_start(c), SUBLANES)
        gather(slot_ref[0, 0, c:c + 1, :] - r0 == rid, r0, slice(c * tkg, (c + 1) * tkg), g_ref[0, 0, c:c + 1, :])

    def overflow(c, carry):
        covered = window_start(c) + sb
        hi = coff_ref[base + (c + 1) * upc]
        first = lax.shift_right_logical(covered, _log2(sb))
        stop = jnp.where(hi > covered, lax.shift_right_logical(hi + (sb - 1), _log2(sb)), first)

        def block(j, carry2):
            b0 = pl.multiple_of(j * sb, sb)
            srow = slot_ref[0, 0, pl.ds(c, 1), :]
            gather((srow - b0 == rid) & (srow >= covered), b0, pl.ds(pl.multiple_of(c * tkg, tkg), tkg),
                   g_ref[0, 0, pl.ds(c, 1), :])
            return carry2

        return lax.fori_loop(first, stop, block, carry)

    lax.fori_loop(0, n_chunks, overflow, 0)
    xb = xg_scr[...].astype(BF16)
    hid = (_silu(_dot(xb, w1_ref[0])) * _dot(xb, w3_ref[0])).astype(BF16)
    y_ref[0, 0] = (_dot(hid, w2_ref[0]) * gs_scr[...]).astype(BF16)


def _expert_call(u2, slot_row, g_row, coff, w1, w3, w2, cap):
    bsz, t, d = u2.shape
    n_e = w1.shape[0]
    tkg = min(MOE_GATHER_TOKENS, t)
    sb = min(MOE_SLOT_ROWS, cap)
    wspec = pl.BlockSpec((1, d, d), lambda b, e, co: (e, 0, 0))
    chunks = pl.BlockSpec((1, 1, t // tkg, tkg), lambda b, e, co: (b, e, 0, 0))
    grid_spec = pltpu.PrefetchScalarGridSpec(
        num_scalar_prefetch=1,
        grid=(bsz, n_e),
        in_specs=[pl.BlockSpec((1, t, d), lambda b, e, co: (b, 0, 0)),
                  chunks, chunks, wspec, wspec, wspec],
        out_specs=pl.BlockSpec((1, 1, cap, d), lambda b, e, co: (b, e, 0, 0)),
        scratch_shapes=[pltpu.VMEM((cap, d), F32), pltpu.VMEM((cap, 1), F32)],
    )
    return pl.pallas_call(
        functools.partial(_expert_kernel, tkg=tkg, sb=sb, n_units=t // LANES),
        out_shape=jax.ShapeDtypeStruct((bsz, n_e, cap, d), BF16),
        grid_spec=grid_spec,
        compiler_params=_params(2),
        name="expert",
    )(coff, u2, slot_row.reshape(bsz, n_e, t // tkg, tkg), g_row.reshape(bsz, n_e, t // tkg, tkg), w1, w3, w2)


def _expert_call_folded(u2, slot_row, g_row, coff, w1, w3, w2, cap):
    bsz, t, d = u2.shape
    n_e = slot_row.shape[1]
    n_units = t // LANES
    first = (jnp.arange(bsz, dtype=I32) * cap)[:, None, None]
    slot_all = jnp.where(slot_row >= 0, slot_row + first, -1).transpose(1, 0, 2).reshape(1, n_e, bsz * t)
    coff_all = (coff.reshape(bsz, n_e, n_units + 1)[:, :, :n_units] + first).transpose(1, 0, 2)
    coff_all = jnp.concatenate([coff_all.reshape(n_e, bsz * n_units), jnp.full((n_e, 1), bsz * cap, I32)], axis=1)
    g_all = g_row.transpose(1, 0, 2).reshape(1, n_e, bsz * t)
    y = _expert_call(u2.reshape(1, bsz * t, d), slot_all, g_all, coff_all.reshape(-1), w1, w3, w2, bsz * cap)
    return y.reshape(n_e, bsz, cap, d).transpose(1, 0, 2, 3)


def _scatter_kernel(coff_ref, h_ref, y_ref, slotc_ref, g2_ref, fg_ref, o_ref, acc_scr, ystack, *, win, group, kb,
                    n_units, final_norm):
    tk = h_ref.shape[1]
    cap = y_ref.shape[2]
    upt = tk // LANES
    i = pl.program_id(1)
    lane = lax.broadcasted_iota(I32, (tk, win), 1)
    bf16_rows = 2 * SUBLANES

    def span(e):
        base = (pl.program_id(0) * N_EXPERTS + e) * (n_units + 1)
        lo = coff_ref[base + i * upt]
        hi = coff_ref[base + (i + 1) * upt]
        return jnp.minimum(lo & (-bf16_rows), cap - win), hi

    acc = jnp.zeros((tk, D_MODEL), F32)
    for p in range(N_EXPERTS // group):
        hots = []
        for j in range(group):
            e = p * group + j
            r0 = pl.multiple_of(span(e)[0], bf16_rows)
            ystack[p, j * win:(j + 1) * win, :] = y_ref[0, e, pl.ds(r0, win), :]
            hots.append(jnp.where(slotc_ref[0, :, e:e + 1] - r0 == lane, 1.0, 0.0).astype(BF16))
        onehot = hots[0] if group == 1 else jnp.concatenate(hots, axis=1)
        acc = acc + _dot(onehot, ystack[p])
    acc_scr[...] = acc

    lane_kb = lax.broadcasted_iota(I32, (tk, kb), 1)
    for e in range(N_EXPERTS):
        r0, hi = span(e)
        covered = r0 + win
        first = lax.shift_right_logical(covered, _log2(kb))
        stop = jnp.where(hi > covered, lax.shift_right_logical(hi + (kb - 1), _log2(kb)), first)

        def block(j, carry, e=e, covered=covered):
            b0 = pl.multiple_of(j * kb, kb)
            scol = slotc_ref[0, :, e:e + 1]
            onehot = jnp.where((scol - b0 == lane_kb) & (scol >= covered), 1.0, 0.0).astype(BF16)
            acc_scr[...] += _dot(onehot, y_ref[0, e, pl.ds(b0, kb), :])
            return carry

        lax.fori_loop(first, stop, block, 0)
    out = h_ref[0] + g2_ref[0] * acc_scr[...]
    o_ref[0] = _rms(out) * fg_ref[...] if final_norm else out


def _scatter_call(h, y, slot_col, coff, g2, cap, ctx_rows, final_g=None):
    bsz, t, d = h.shape
    n_e = y.shape[1]
    tk = min(512, t)
    win = min(MOE_SCATTER_WINDOW, cap)
    group = 2 if win == LANES else 1
    kb = min(MOE_SCATTER_SLOTS, cap)
    row = (lambda b: MOD_ROWS - 8) if ctx_rows else (lambda b: b)
    tile = lambda w: pl.BlockSpec((1, tk, w), lambda b, i, co: (b, i, 0))
    grid_spec = pltpu.PrefetchScalarGridSpec(
        num_scalar_prefetch=1,
        grid=(bsz, t // tk),
        in_specs=[tile(d),
                  pl.BlockSpec((1, n_e, cap, d), lambda b, i, co: (b, 0, 0, 0), pipeline_mode=pl.Buffered(1)),
                  tile(n_e), pl.BlockSpec((1, 1, d), lambda b, i, co: (row(b), 0, 0)),
                  pl.BlockSpec((1, d), lambda b, i, co: (0, 0))],
        out_specs=tile(d),
        scratch_shapes=[pltpu.VMEM((tk, d), F32), pltpu.VMEM((n_e // group, group * win, d), BF16)],
    )
    return pl.pallas_call(
        functools.partial(_scatter_kernel, win=win, group=group, kb=kb, n_units=t // LANES,
                          final_norm=final_g is not None),
        out_shape=jax.ShapeDtypeStruct((bsz, t, d), F32),
        grid_spec=grid_spec,
        compiler_params=_params(2),
        name="moe_scatter",
    )(coff, h, y, slot_col, g2, jnp.ones((1, d), F32) if final_g is None else final_g)


def _rope_tables(n_lat, n_ctx):
    rows = n_lat // GRID_W
    row = jnp.repeat(jnp.arange(rows, dtype=F32), GRID_W)
    col = jnp.tile(jnp.arange(GRID_W, dtype=F32), rows)
    inv = jnp.power(ROPE_THETA, -jnp.arange(ROPE_PAIRS_AXIS, dtype=F32) / ROPE_PAIRS_AXIS)
    ang = jnp.concatenate([row[:, None] * inv, col[:, None] * inv], axis=-1)
    cos, sin = jnp.cos(ang), jnp.sin(ang)
    cs = jnp.concatenate([cos, cos], axis=-1)
    sn = jnp.concatenate([-sin, sin], axis=-1)
    return cs, sn, jnp.ones((n_ctx, HEAD_DIM), F32), jnp.zeros((n_ctx, HEAD_DIM), F32)


def _dft_tables(n):
    k = jnp.arange(n // 2, dtype=I32)

    def tables(first):
        m = ((2 * k[:, None] + 1) * (2 * k[None, :] + first)) % (4 * n)
        ang = m.astype(F32) * (2.0 * math.pi / (4 * n))
        return jnp.cos(ang), jnp.sin(ang)

    ce, se = tables(0)
    co, so = tables(1)
    fwd = tuple(x.astype(BF16) for x in (ce, co, se, so))
    inv = tuple(x.T.astype(BF16) for x in (ce, co, -se, -so))
    return fwd, inv


def _filter_features(n):
    t = jnp.linspace(0.0, 1.0, n, dtype=F32)[:, None]
    w = (2.0 * math.pi / n) * jnp.arange(n, dtype=F32)[:, None]
    f = jnp.linspace(1e-4, HY_BANDS - 1, HY_BANDS, dtype=F32)[None, :]
    feats = jnp.concatenate([t, jnp.cos(f * w), -jnp.sin(f * w)], axis=-1)
    return jnp.pad(feats, ((0, 0), (0, LANES - HY_EMB)))


def _pad_to(x, shape):
    return jnp.pad(x, [(0, s - d) for d, s in zip(x.shape, shape)])


def _block_diag_chunks(w):
    per = LANES // LRU_BLOCK
    w = w.reshape(2, LRU_BLOCKS // per, per, LRU_BLOCK, LRU_BLOCK)
    eye = jnp.eye(per, dtype=w.dtype)
    return jnp.einsum('dcpkj,pq->dcpkqj', w, eye).reshape(2, LRU_BLOCKS // per, LANES, LANES)


def kernel(x, c, ctx, c_ctx, mod_w, mod_b, norm1_g, norm2_g, w_in, q_norm_g, k_norm_g, hy_conv_w, hy_conv_b,
           hy_fw1, hy_fb1, hy_fw2, hy_fb2, hy_fw3, hy_freq, hy_bias, lru_conv_w, lru_conv_b, lru_wa, lru_ba,
           lru_wx, lru_bx, lru_lambda, w_attn_out, w_hy_out, w_lru_out, w_out, router_w, exp_w1, exp_w3, exp_w2,
           final_norm_g):
    bsz, n_lat, d = x.shape
    n_ctx = ctx.shape[1]
    depth = mod_w.shape[0]
    assert d == D_MODEL and bsz <= MOD_ROWS - 8 and n_lat % GRID_W == 0

    cs_l, sn_l, cs_c, sn_c = _rope_tables(n_lat, n_ctx)
    dft_l = _dft_tables(n_lat)
    dft_c = _dft_tables(n_ctx)
    feats_l = _filter_features(n_lat)
    feats_c = _filter_features(n_ctx)
    max_decay = math.log(HY_DECAY_TARGET) / HY_FAST_DECAY_PCT
    min_decay = math.log(HY_DECAY_TARGET) / HY_SLOW_DECAY_PCT
    deltas_abs = jnp.abs(jnp.linspace(min_decay, max_decay, HY_WIDTH, dtype=F32))[None, :]
    cap_l = max(1, EC_CAPACITY * n_lat // N_EXPERTS)
    cap_c = max(1, EC_CAPACITY * n_ctx // N_EXPERTS)

    cvec = jnp.zeros((MOD_ROWS, d), F32).at[:bsz].set(c).at[MOD_ROWS - 8].set(c_ctx)
    mod = _mod_call(cvec, mod_w, mod_b)
    mod = mod.reshape(depth, MOD_ROWS, 6, 1, d).transpose(0, 2, 1, 3, 4)

    h, hc = x, ctx
    for l in range(depth):
        last = l == depth - 1
        sh1, sc1, g1, sh2, sc2, g2 = (mod[l, j] for j in range(6))
        w_in_b = w_in[l].astype(BF16)
        n1 = norm1_g[l][None, :]
        n2 = norm2_g[l][None, :]
        qg = q_norm_g[l][None, :]
        kg = k_norm_g[l][None, :]

        q, k, v, hy, rx, rg, gate = _inproj_call(h, sh1, sc1, n1, w_in_b, cs_l, sn_l, qg, kg, False)
        qc, kc, vc, hyc, rxc, rgc, gatec = _inproj_call(hc, sh1, sc1, n1, w_in_b, cs_c, sn_c, qg, kg, True)

        attn = _attn_call(q, [(k, v), (kc, vc)])

        fw1 = _pad_to(hy_fw1[l], (LANES, LANES))
        fb1 = _pad_to(hy_fb1[l][None, :], (1, LANES))
        fw2 = _pad_to(hy_fw2[l], (LANES, LANES))
        fb2 = _pad_to(hy_fb2[l][None, :], (1, LANES))
        fw3 = _pad_to(hy_fw3[l], (LANES, 2 * HY_WIDTH))
        freq = _pad_to(hy_freq[l], (2, LANES))
        hbias = hy_bias[l][None, :]

        def hyena(hy_in, feats, tabs):
            tabs_fwd, tabs_inv = tabs
            hsd = _hy_filter_call(feats, fw1, fb1, fw2, fb2, fw3, freq, deltas_abs)
            kspec = _hy_kspec_call(tabs_fwd, hsd)
            x0, we, wo = _hy_pre_call(hy_in, hy_conv_w[l], hy_conv_b[l][None, :])
            us = _hy_fwd_call(tabs_fwd, we, wo, kspec)
            return _hy_inv_call(tabs_inv, us, x0, we, wo, hbias)

        hyo = hyena(hy, feats_l, dft_l)

        lruo, lruoc = _lru_call(rx, rxc, rg, rgc, lru_conv_w[l], lru_conv_b[l][None, :],
                                _block_diag_chunks(lru_wa[l]).astype(BF16), _block_diag_chunks(lru_wx[l]).astype(BF16),
                                lru_ba[l][:, None, :], lru_bx[l][:, None, :], lru_lambda[l][:, None, :])

        wa = w_attn_out[l].astype(BF16)
        wh = w_hy_out[l].astype(BF16)
        wl = w_lru_out[l].astype(BF16)
        wo = w_out[l].astype(BF16)
        rw_t = router_w[l].T.astype(BF16)
        e1 = exp_w1[l].astype(BF16)
        e3 = exp_w3[l].astype(BF16)
        e2 = exp_w2[l].astype(BF16)

        def channel_mix(h_in, attn_in, hy_in, lru_in, gate_in, cap, ctx_rows, final_g=None):
            hn, u2, lt = _merge_call(attn_in, hy_in, lru_in, gate_in, h_in, g1, wa, wh, wl, wo, n2, sh2, sc2, rw_t,
                                     ctx_rows)
            slot_row, slot_col, g_row, coff = _router_call(lt, cap)
            if ctx_rows:
                y = _expert_call_folded(u2, slot_row, g_row, coff, e1, e3, e2, cap)
            else:
                y = _expert_call(u2, slot_row, g_row, coff, e1, e3, e2, cap)
            return _scatter_call(hn, y, slot_col, coff, g2, cap, ctx_rows, final_g)

        h = channel_mix(h, attn, hyo, lruo, gate, cap_l, False, final_norm_g[None, :] if last else None)
        if not last:
            attn_c = _attn_call(qc, [(kc, vc)])
            hyo_c = hyena(hyc, feats_c, dft_c)
            hc = channel_mix(hc, attn_c, hyo_c, lruoc, gatec, cap_c, True)

    return h
```

```python
import functools
import math

import jax
import jax.numpy as jnp
from jax import lax
from jax.experimental import pallas as pl
from jax.experimental.pallas import tpu as pltpu

F32 = jnp.float32
BF16 = jnp.bfloat16
I32 = jnp.int32

D_MODEL = 1024
GRID_W = 64
NORM_EPS = 1e-6
N_HEADS = 8
N_KV_HEADS = 2
HEAD_DIM = 128
GROUP = N_HEADS // N_KV_HEADS
ROPE_PAIRS_AXIS = HEAD_DIM // 4
ROPE_THETA = 10000.0
ATTN_SCALE = HEAD_DIM ** -0.5
Q_PRESCALE = ATTN_SCALE * math.log2(math.e)
Q_WIDTH = N_HEADS * HEAD_DIM
KV_WIDTH = N_KV_HEADS * HEAD_DIM
HY_WIDTH = D_MODEL // 2
HY_BANDS = 16
HY_EMB = 1 + 2 * HY_BANDS
HY_FILTER_HIDDEN = 64
HY_FAST_DECAY_PCT = 0.3
HY_SLOW_DECAY_PCT = 1.5
HY_DECAY_TARGET = 1e-2
LRU_WIDTH = D_MODEL // 2
LRU_BLOCKS = 8
LRU_BLOCK = LRU_WIDTH // LRU_BLOCKS
LRU_C = 8.0
N_BRANCH = 3
N_EXPERTS = 16
EC_CAPACITY = 2
GATE_WIDTH = N_BRANCH * D_MODEL
OFF_Q = 0
OFF_K = OFF_Q + Q_WIDTH
OFF_V = OFF_K + KV_WIDTH
OFF_HY = OFF_V + KV_WIDTH
OFF_RX = OFF_HY + 3 * HY_WIDTH
OFF_RG = OFF_RX + LRU_WIDTH
OFF_GATE = OFF_RG + LRU_WIDTH
IN_TOTAL = OFF_GATE + GATE_WIDTH

LANES = 128
SUBLANES = 8
V7X_VMEM_LIMIT_BYTES = 56 * 1024 * 1024
MOD_ROWS = 24


def _params(n_axes, vmem=V7X_VMEM_LIMIT_BYTES):
    return pltpu.CompilerParams(dimension_semantics=("arbitrary",) * n_axes, vmem_limit_bytes=vmem)


def _dot(a, b):
    return jnp.dot(a, b, preferred_element_type=F32)


def _dot_nt(a, b):
    return lax.dot_general(a, b, (((1,), (1,)), ((), ())), preferred_element_type=F32)


def _split2(x):
    hi = x.astype(BF16)
    lo = (x - hi.astype(F32)).astype(BF16)
    return hi, lo


def _dot3(a, b):
    ah, al = _split2(a)
    bh, bl = _split2(b)
    return _dot(ah, bh) + (_dot(ah, bl) + _dot(al, bh))


def _rms(x):
    return x * lax.rsqrt(jnp.mean(x * x, axis=-1, keepdims=True) + NORM_EPS)


def _sigmoid(x):
    return 0.5 * jnp.tanh(0.5 * x) + 0.5


def _silu(x):
    return x * _sigmoid(x)


def _gelu_tanh(x):
    return 0.5 * x * (1.0 + jnp.tanh(math.sqrt(2.0 / math.pi) * (x + 0.044715 * (x * x * x))))


def _shift_down(x, k):
    row = lax.broadcasted_iota(I32, x.shape, 0)
    return jnp.where(row >= k, pltpu.roll(x, k, 0), 0.0)


def _shift_up(x, k):
    n = x.shape[0]
    row = lax.broadcasted_iota(I32, x.shape, 0)
    return jnp.where(row < n - k, pltpu.roll(x, n - k, 0), 0.0)


def _mod_kernel(c_ref, w_ref, b_ref, o_ref):
    o_ref[0] = _dot3(_silu(c_ref[...]), w_ref[0]) + b_ref[0]


def _mod_call(cvec, mod_w, mod_b):
    depth, d, six_d = mod_w.shape
    tn = 1536
    return pl.pallas_call(
        _mod_kernel,
        out_shape=jax.ShapeDtypeStruct((depth, MOD_ROWS, six_d), F32),
        grid=(depth, six_d // tn),
        in_specs=[
            pl.BlockSpec((MOD_ROWS, d), lambda l, j: (0, 0)),
            pl.BlockSpec((1, d, tn), lambda l, j: (l, 0, j)),
            pl.BlockSpec((1, 1, tn), lambda l, j: (l, 0, j)),
        ],
        out_specs=pl.BlockSpec((1, MOD_ROWS, tn), lambda l, j: (l, 0, j)),
        compiler_params=_params(2),
        name="mod",
    )(cvec, mod_w, mod_b.reshape(depth, 1, six_d))


def _inproj_kernel(h_ref, sh_ref, sc_ref, g_ref, w_ref, cs_ref, sn_ref, qg_ref, kg_ref,
                   q_ref, k_ref, v_ref, hy_ref, rx_ref, rg_ref, gate_ref):
    u = (_rms(h_ref[0]) * g_ref[...] * (1.0 + sc_ref[0]) + sh_ref[0]).astype(BF16)
    cs = cs_ref[...]
    sn = sn_ref[...]

    def normed_rope(z, g):
        r = _rms(z) * g
        return r * cs + pltpu.roll(r, HEAD_DIM // 2, 1) * sn

    zq = _dot(u, w_ref[0, :, OFF_Q:OFF_Q + Q_WIDTH])
    for hd in range(N_HEADS):
        sl = slice(hd * HEAD_DIM, (hd + 1) * HEAD_DIM)
        q_ref[0, hd] = (normed_rope(zq[:, sl], qg_ref[...]) * Q_PRESCALE).astype(BF16)
    zkv = _dot(u, w_ref[0, :, OFF_K:OFF_K + 2 * KV_WIDTH])
    for hd in range(N_KV_HEADS):
        sl = slice(hd * HEAD_DIM, (hd + 1) * HEAD_DIM)
        k_ref[0, :, sl] = normed_rope(zkv[:, sl], kg_ref[...]).astype(BF16)
    v_ref[0] = zkv[:, KV_WIDTH:].astype(BF16)
    for j in range(3):
        sl = slice(j * HY_WIDTH, (j + 1) * HY_WIDTH)
        hy_ref[0, :, sl] = _dot(u, w_ref[0, :, OFF_HY + j * HY_WIDTH:OFF_HY + (j + 1) * HY_WIDTH]).astype(BF16)
    rr = _dot(u, w_ref[0, :, OFF_RX:OFF_RX + 2 * LRU_WIDTH])
    rx_ref[0] = rr[:, :LRU_WIDTH]
    rg_ref[0] = rr[:, LRU_WIDTH:].astype(BF16)
    for j in range(N_BRANCH):
        sl = slice(j * D_MODEL, (j + 1) * D_MODEL)
        gate_ref[0, :, sl] = _dot(u, w_ref[0, :, OFF_GATE + j * D_MODEL:OFF_GATE + (j + 1) * D_MODEL]).astype(BF16)


def _inproj_call(h, sh, sc, g, w_bf, layer, cs, sn, qg, kg, ctx_rows):
    bsz, t, d = h.shape
    tm = min(512, t)
    row = (lambda b: MOD_ROWS - 8) if ctx_rows else (lambda b: b)
    tok = lambda w: pl.BlockSpec((1, tm, w), lambda b, i: (b, i, 0))
    modspec = pl.BlockSpec((1, 1, d), lambda b, i: (row(b), 0, 0))
    const = lambda shape: pl.BlockSpec(shape, lambda b, i: (0,) * len(shape))
    outs = [(KV_WIDTH, BF16), (KV_WIDTH, BF16), (3 * HY_WIDTH, BF16), (LRU_WIDTH, F32), (LRU_WIDTH, BF16),
            (GATE_WIDTH, BF16)]
    return pl.pallas_call(
        _inproj_kernel,
        out_shape=[jax.ShapeDtypeStruct((bsz, N_HEADS, t, HEAD_DIM), BF16)]
        + [jax.ShapeDtypeStruct((bsz, t, w), dt) for w, dt in outs],
        grid=(bsz, t // tm),
        in_specs=[
            tok(d), modspec, modspec, const((1, d)),
            pl.BlockSpec((1, d, IN_TOTAL), lambda b, i: (layer, 0, 0), pipeline_mode=pl.Buffered(1)),
            pl.BlockSpec((tm, HEAD_DIM), lambda b, i: (i, 0)),
            pl.BlockSpec((tm, HEAD_DIM), lambda b, i: (i, 0)),
            const((1, HEAD_DIM)), const((1, HEAD_DIM)),
        ],
        out_specs=[pl.BlockSpec((1, N_HEADS, tm, HEAD_DIM), lambda b, i: (b, 0, i, 0))] + [tok(w) for w, _ in outs],
        compiler_params=_params(2),
        name="inproj",
    )(h, sh, sc, g, w_bf, cs, sn, qg, kg)


ATTN_TQ = 512
ATTN_TK = 512
ATTN_STREAMS = 4
NEG_BIG = -1e30


def _attn_kernel(*refs, n_src, tq):
    q_ref = refs[0]
    kv = refs[1:1 + 2 * n_src]
    o_ref, kcat, vext = refs[1 + 2 * n_src:]
    n_keys = kcat.shape[0]

    @pl.when(pl.program_id(2) == 0)
    def _():
        off = 0
        for i in range(n_src):
            n = kv[2 * i].shape[1]
            kcat[off:off + n, :] = kv[2 * i][0]
            vext[off:off + n, 0:HEAD_DIM] = kv[2 * i + 1][0]
            off += n
        vext[:, HEAD_DIM:] = jnp.ones((n_keys, HEAD_DIM), BF16)

    rows = GROUP * tq
    q = q_ref[0].reshape(rows, HEAD_DIM)
    per = rows // ATTN_STREAMS
    qs = [q[i * per:(i + 1) * per] for i in range(ATTN_STREAMS)]
    ms = [jnp.full((per, 1), NEG_BIG, F32) for _ in qs]
    accs = [jnp.zeros((per, 2 * HEAD_DIM), F32) for _ in qs]
    for off in range(0, n_keys, ATTN_TK):
        size = min(ATTN_TK, n_keys - off)
        kk = kcat[off:off + size, :]
        vv = vext[off:off + size, :]
        for i in range(ATTN_STREAMS):
            s = _dot_nt(qs[i], kk)
            m_new = jnp.maximum(ms[i], s.max(axis=-1, keepdims=True))
            p = jnp.exp2(s - m_new).astype(BF16)
            accs[i] = jnp.exp2(ms[i] - m_new) * accs[i] + _dot(p, vv)
            ms[i] = m_new
    out = jnp.concatenate([a[:, :HEAD_DIM] / a[:, HEAD_DIM:] for a in accs], axis=0).astype(BF16)
    for g in range(GROUP):
        o_ref[0, :, g * HEAD_DIM:(g + 1) * HEAD_DIM] = out[g * tq:(g + 1) * tq]


def _attn_call(q, kvs):
    bsz, _, tq_all, _ = q.shape
    tq = min(ATTN_TQ, tq_all)
    gw = GROUP * HEAD_DIM
    in_specs = [pl.BlockSpec((1, GROUP, tq, HEAD_DIM), lambda b, h, i: (b, h, i, 0))]
    args = [q]
    n_keys = 0
    for k, v in kvs:
        tk = k.shape[1]
        n_keys += tk
        spec = pl.BlockSpec((1, tk, HEAD_DIM), lambda b, h, i: (b, 0, h))
        in_specs += [spec, spec]
        args += [k, v]
    return pl.pallas_call(
        functools.partial(_attn_kernel, n_src=len(kvs), tq=tq),
        out_shape=jax.ShapeDtypeStruct((bsz, tq_all, Q_WIDTH), BF16),
        grid=(bsz, N_KV_HEADS, tq_all // tq),
        in_specs=in_specs,
        out_specs=pl.BlockSpec((1, tq, gw), lambda b, h, i: (b, i, h)),
        scratch_shapes=[pltpu.VMEM((n_keys, HEAD_DIM), BF16), pltpu.VMEM((n_keys, 2 * HEAD_DIM), BF16)],
        compiler_params=_params(3),
        name="attn",
    )(*args)


def _hy_pre_kernel(h0_ref, h1_ref, h2_ref, w0_ref, w1_ref, w2_ref, b0_ref, b1_ref, b2_ref,
                   x0_ref, we_ref, wo_ref, w_scr):
    def conv(x_ref, cw_ref, cb_ref):
        x = x_ref[0].astype(F32)
        cw = cw_ref[...]
        return cb_ref[...] + _shift_down(x, 1) * cw[0:1] + x * cw[1:2] + _shift_up(x, 1) * cw[2:3]

    x0_ref[0] = conv(h0_ref, w0_ref, b0_ref).astype(BF16)
    w_scr[...] = conv(h2_ref, w2_ref, b2_ref) * conv(h1_ref, w1_ref, b1_ref)
    half = w_scr.shape[0] // 2
    we_ref[0] = w_scr[pl.ds(0, half, stride=2), :].astype(BF16)
    wo_ref[0] = w_scr[pl.ds(1, half, stride=2), :].astype(BF16)


def _hy_pre_call(hy, conv_w, conv_b):
    bsz, t, _ = hy.shape
    nch = HY_WIDTH // LANES
    xs = lambda part: pl.BlockSpec((1, t, LANES), lambda b, j: (b, 0, part * nch + j))
    ws = lambda part: pl.BlockSpec((3, LANES), lambda b, j: (0, part * nch + j))
    bs = lambda part: pl.BlockSpec((1, LANES), lambda b, j: (0, part * nch + j))
    out = pl.BlockSpec((1, t, LANES), lambda b, j: (b, 0, j))
    outh = pl.BlockSpec((1, t // 2, LANES), lambda b, j: (b, 0, j))
    return pl.pallas_call(
        _hy_pre_kernel,
        out_shape=[jax.ShapeDtypeStruct((bsz, t, HY_WIDTH), BF16),
                   jax.ShapeDtypeStruct((bsz, t // 2, HY_WIDTH), BF16),
                   jax.ShapeDtypeStruct((bsz, t // 2, HY_WIDTH), BF16)],
        grid=(bsz, nch),
        in_specs=[xs(0), xs(1), xs(2), ws(0), ws(1), ws(2), bs(0), bs(1), bs(2)],
        out_specs=[out, outh, outh],
        scratch_shapes=[pltpu.VMEM((t, LANES), F32)],
        compiler_params=_params(2),
        name="hy_pre",
    )(hy, hy, hy, conv_w, conv_w, conv_w, conv_b, conv_b, conv_b)


def _hy_filter_kernel(feat_ref, w1_ref, b1_ref, w2_ref, b2_ref, w3_ref, fr_ref, dl_ref, o_ref):
    feats = feat_ref[...]
    hid = jnp.sin(fr_ref[0:1] * (_dot3(feats, w1_ref[...]) + b1_ref[...]))
    hid = jnp.sin(fr_ref[1:2] * (_dot3(hid, w2_ref[...]) + b2_ref[...]))
    filt = _dot3(hid, w3_ref[...])
    decay = jnp.exp(-feats[:, 0:1] * dl_ref[...])
    h_fwd = filt[:, :HY_WIDTH] * decay
    h_bwd = filt[:, HY_WIDTH:] * decay
    o_ref[:, :HY_WIDTH] = h_fwd + h_bwd
    o_ref[:, HY_WIDTH:] = h_bwd - h_fwd


def _hy_filter_call(feats, fw1, fb1, fw2, fb2, fw3, freq, deltas_abs):
    n = feats.shape[0]
    tn = min(512, n)
    const = lambda shape: pl.BlockSpec(shape, lambda i: (0,) * len(shape))
    return pl.pallas_call(
        _hy_filter_kernel,
        out_shape=jax.ShapeDtypeStruct((n, 2 * HY_WIDTH), F32),
        grid=(n // tn,),
        in_specs=[pl.BlockSpec((tn, LANES), lambda i: (i, 0)), const((LANES, LANES)), const((1, LANES)),
                  const((LANES, LANES)), const((1, LANES)), const((LANES, 2 * HY_WIDTH)), const((2, LANES)),
                  const((1, HY_WIDTH))],
        out_specs=pl.BlockSpec((tn, 2 * HY_WIDTH), lambda i: (i, 0)),
        compiler_params=_params(1),
        name="hy_filter",
    )(feats, fw1, fb1, fw2, fb2, fw3, freq, deltas_abs)


def _hy_kspec_kernel(ce_ref, co_ref, se_ref, so_ref, he_ref, ho_ref, krl_ref, kil_ref, krh_ref, kih_ref, *, scale):
    def branch(tab_e, tab_o, cols):
        he_hi, he_lo = _split2(he_ref[:, cols])
        ho_hi, ho_lo = _split2(ho_ref[:, cols])
        pe = _dot(tab_e[...], he_hi) + _dot(tab_e[...], he_lo)
        po = _dot(tab_o[...], ho_hi) + _dot(tab_o[...], ho_lo)
        return pe, po

    pc, qc = branch(ce_ref, co_ref, slice(0, HY_WIDTH))
    ps, qs = branch(se_ref, so_ref, slice(HY_WIDTH, 2 * HY_WIDTH))
    krl_ref[...] = (pc + qc) * scale
    krh_ref[...] = (pc - qc) * scale
    kil_ref[...] = (ps + qs) * scale
    kih_ref[...] = (qs - ps) * scale


def _hy_kspec_call(tabs, hsd):
    n = hsd.shape[0]
    half = n // 2
    tf = min(512, half)
    tab = pl.BlockSpec((tf, half), lambda i: (i, 0))
    out = pl.BlockSpec((tf, HY_WIDTH), lambda i: (i, 0))
    return pl.pallas_call(
        functools.partial(_hy_kspec_kernel, scale=1.0 / n),
        out_shape=[jax.ShapeDtypeStruct((half, HY_WIDTH), F32)] * 4,
        grid=(half // tf,),
        in_specs=[tab, tab, tab, tab, pl.BlockSpec((half, 2 * HY_WIDTH), lambda i: (0, 0)),
                  pl.BlockSpec((half, 2 * HY_WIDTH), lambda i: (1, 0))],
        out_specs=[out] * 4,
        compiler_params=_params(1),
        name="hy_kspec",
    )(*tabs, hsd, hsd)


def _hy_fwd_kernel(ce_ref, co_ref, se_ref, so_ref, we_ref, wo_ref, krl_ref, kil_ref, krh_ref, kih_ref,
                   urp_ref, urm_ref, uim_ref, uip_ref):
    we = we_ref[0]
    wo = wo_ref[0]
    pc = _dot(ce_ref[...], we)
    qc = _dot(co_ref[...], wo)
    ps = _dot(se_ref[...], we)
    qs = _dot(so_ref[...], wo)
    a_lo, a_hi = pc + qc, pc - qc
    b_lo, b_hi = ps + qs, qs - ps
    krl, kil, krh, kih = krl_ref[...], kil_ref[...], krh_ref[...], kih_ref[...]
    yre_lo = a_lo * krl + b_lo * kil
    yim_lo = a_lo * kil - b_lo * krl
    yre_hi = a_hi * krh + b_hi * kih
    yim_hi = a_hi * kih - b_hi * krh
    urp_ref[0] = (yre_lo + yre_hi).astype(BF16)
    urm_ref[0] = (yre_lo - yre_hi).astype(BF16)
    uim_ref[0] = (yim_lo - yim_hi).astype(BF16)
    uip_ref[0] = (yim_lo + yim_hi).astype(BF16)


def _hy_fwd_call(tabs, we, wo, kspec):
    bsz, half, _ = we.shape
    tf = min(512, half)
    tab = pl.BlockSpec((tf, half), lambda i, b: (i, 0))
    sig = pl.BlockSpec((1, half, HY_WIDTH), lambda i, b: (b, 0, 0))
    kk = pl.BlockSpec((tf, HY_WIDTH), lambda i, b: (i, 0))
    out = pl.BlockSpec((1, tf, HY_WIDTH), lambda i, b: (b, i, 0))
    return pl.pallas_call(
        _hy_fwd_kernel,
        out_shape=[jax.ShapeDtypeStruct((bsz, half, HY_WIDTH), BF16)] * 4,
        grid=(half // tf, bsz),
        in_specs=[tab, tab, tab, tab, sig, sig, kk, kk, kk, kk],
        out_specs=[out] * 4,
        compiler_params=_params(2),
        name="hy_fwd",
    )(*tabs, we, wo, *kspec)


def _hy_inv_kernel(cte_ref, cto_ref, ste_ref, sto_ref, urp_ref, urm_ref, uim_ref, uip_ref, x0_ref, we_ref,
                   wo_ref, bias_ref, o_ref, y_scr):
    tt = cte_ref.shape[0]
    bias = bias_ref[...]
    y_even = _dot(cte_ref[...], urp_ref[0]) + _dot(ste_ref[...], uim_ref[0]) + we_ref[0].astype(F32) * bias
    y_odd = _dot(cto_ref[...], urm_ref[0]) + _dot(sto_ref[...], uip_ref[0]) + wo_ref[0].astype(F32) * bias
    for j in range(HY_WIDTH // LANES):
        sl = slice(j * LANES, (j + 1) * LANES)
        y_scr[j, pl.ds(0, tt, stride=2), :] = y_even[:, sl]
        y_scr[j, pl.ds(1, tt, stride=2), :] = y_odd[:, sl]
        o_ref[0, :, sl] = (x0_ref[0, :, sl].astype(F32) * y_scr[j]).astype(BF16)


def _hy_inv_call(tabs_t, us, x0, we, wo, bias):
    bsz, n, _ = x0.shape
    half = n // 2
    tt = min(256, half)
    tab = pl.BlockSpec((tt, half), lambda i, b: (i, 0))
    full = pl.BlockSpec((1, half, HY_WIDTH), lambda i, b: (b, 0, 0))
    tile = pl.BlockSpec((1, 2 * tt, HY_WIDTH), lambda i, b: (b, i, 0))
    htile = pl.BlockSpec((1, tt, HY_WIDTH), lambda i, b: (b, i, 0))
    return pl.pallas_call(
        _hy_inv_kernel,
        out_shape=jax.ShapeDtypeStruct((bsz, n, HY_WIDTH), BF16),
        grid=(half // tt, bsz),
        in_specs=[tab, tab, tab, tab, full, full, full, full, tile, htile, htile,
                  pl.BlockSpec((1, HY_WIDTH), lambda i, b: (0, 0))],
        out_specs=tile,
        scratch_shapes=[pltpu.VMEM((HY_WIDTH // LANES, 2 * tt, LANES), F32)],
        compiler_params=_params(2),
        name="hy_inv",
    )(*tabs_t, *us, x0, we, wo, bias)


LRU_SCAN_UNROLL = 4


def _lru_kernel(rxl_ref, rxc_ref, rgl_ref, rgc_ref, cw_ref, cb_ref, wa_ref, wx_ref, ba_ref, bx_ref, lam_ref,
                ol_ref, oc_ref, a_scr, b_scr, h_scr):
    cw = cw_ref[...]
    cb = cb_ref[...]
    row8 = lax.broadcasted_iota(I32, (SUBLANES, LANES), 0)

    def coeffs(x_ref, n):
        x = x_ref[0]
        xc = (cb + _shift_down(x, 2) * cw[0:1] + _shift_down(x, 1) * cw[1:2] + x * cw[2:3]
              + _shift_up(x, 1) * cw[3:4])
        xb = xc.astype(BF16)
        for d in range(2):
            r = _sigmoid(_dot(xb, wa_ref[d, 0]) + ba_ref[d])
            i = _sigmoid(_dot(xb, wx_ref[d, 0]) + bx_ref[d])
            nl = -lam_ref[d]
            softplus = jnp.maximum(nl, 0.0) + jnp.log(1.0 + jnp.exp(-jnp.abs(nl)))
            log_a = (-LRU_C) * r * softplus
            a = jnp.exp(log_a)
            a_scr[d, 0:n, :] = a
            b_scr[d, 0:n, :] = jnp.sqrt(1.0 - a * a) * i * xc

    def scan(n, carry_f, carry_b):
        ng = n // SUBLANES

        def body(g, carry):
            cf, cbk = carry
            of = pl.multiple_of(g * SUBLANES, SUBLANES)
            ob = pl.multiple_of((ng - 1 - g) * SUBLANES, SUBLANES)
            a = a_scr[0, pl.ds(of, SUBLANES), :]
            b = b_scr[0, pl.ds(of, SUBLANES), :]
            a2 = a_scr[1, pl.ds(ob, SUBLANES), :]
            b2 = b_scr[1, pl.ds(ob, SUBLANES), :]
            for k in (1, 2, 4):
                keep = row8 >= k
                b = a * jnp.where(keep, pltpu.roll(b, k, 0), 0.0) + b
                a = a * jnp.where(keep, pltpu.roll(a, k, 0), 1.0)
                keep2 = row8 < SUBLANES - k
                b2 = a2 * jnp.where(keep2, pltpu.roll(b2, SUBLANES - k, 0), 0.0) + b2
                a2 = a2 * jnp.where(keep2, pltpu.roll(a2, SUBLANES - k, 0), 1.0)
            hf = a * cf + b
            hb = a2 * cbk + b2
            h_scr[0, pl.ds(of, SUBLANES), :] = hf
            h_scr[1, pl.ds(ob, SUBLANES), :] = hb
            return hf[SUBLANES - 1:SUBLANES, :], hb[0:1, :]

        return lax.fori_loop(0, ng, body, (carry_f, carry_b), unroll=LRU_SCAN_UNROLL)

    n_ctx = rxc_ref.shape[1]
    n_lat = rxl_ref.shape[1]
    zero = jnp.zeros((1, LANES), F32)
    coeffs(rxc_ref, n_ctx)
    cf, cbk = scan(n_ctx, zero, zero)
    oc_ref[0] = ((h_scr[0, 0:n_ctx, :] + h_scr[1, 0:n_ctx, :]) * _gelu_tanh(rgc_ref[0].astype(F32))).astype(BF16)
    coeffs(rxl_ref, n_lat)
    scan(n_lat, cf, cbk)
    ol_ref[0] = ((h_scr[0] + h_scr[1]) * _gelu_tanh(rgl_ref[0].astype(F32))).astype(BF16)


def _lru_call(rx, rx_c, rg, rg_c, conv_w, conv_b, wa_bd, wx_bd, ba, bx, lam):
    bsz, t, _ = rx.shape
    tc = rx_c.shape[1]
    nch = LRU_WIDTH // LANES
    seq = lambda n: pl.BlockSpec((1, n, LANES), lambda b, j: (b, 0, j))
    vec = lambda r: pl.BlockSpec((r, 1, LANES), lambda b, j: (0, 0, j))
    wsp = pl.BlockSpec((2, 1, LANES, LANES), lambda b, j: (0, j, 0, 0))
    return pl.pallas_call(
        _lru_kernel,
        out_shape=[jax.ShapeDtypeStruct((bsz, t, LRU_WIDTH), BF16), jax.ShapeDtypeStruct((bsz, tc, LRU_WIDTH), BF16)],
        grid=(bsz, nch),
        in_specs=[seq(t), seq(tc), seq(t), seq(tc),
                  pl.BlockSpec((4, LANES), lambda b, j: (0, j)), pl.BlockSpec((1, LANES), lambda b, j: (0, j)),
                  wsp, wsp, vec(2), vec(2), vec(2)],
        out_specs=[seq(t), seq(tc)],
        scratch_shapes=[pltpu.VMEM((2, t, LANES), F32)] * 3,
        compiler_params=_params(2),
        name="lru",
    )(rx, rx_c, rg, rg_c, conv_w, conv_b, wa_bd, wx_bd, ba, bx, lam)


def _merge_kernel(attn_ref, hy_ref, lru_ref, gate_ref, h_ref, g1_ref, wa_ref, wh_ref, wl_ref, wo_ref,
                  n2_ref, sh2_ref, sc2_ref, rw_ref, hn_ref, u2_ref, lt_ref):
    def gate(j):
        return _sigmoid(gate_ref[0, :, j * D_MODEL:(j + 1) * D_MODEL].astype(F32))

    y = gate(0) * _dot(attn_ref[0], wa_ref[...])
    y = y + gate(1) * _dot(hy_ref[0], wh_ref[...])
    y = y + gate(2) * _dot(lru_ref[0], wl_ref[...])
    hn = h_ref[0] + g1_ref[0] * _dot(y.astype(BF16), wo_ref[...])
    hn_ref[0] = hn
    u2 = (_rms(hn) * n2_ref[...] * (1.0 + sc2_ref[0]) + sh2_ref[0]).astype(BF16)
    u2_ref[0] = u2
    lt_ref[0] = _dot_nt(rw_ref[...], u2)


def _merge_call(attn, hyo, lruo, gate, h, g1, wa, wh, wl, wo, n2g, sh2, sc2, rw_t, ctx_rows):
    bsz, t, d = h.shape
    tm = min(512, t)
    row = (lambda b: MOD_ROWS - 8) if ctx_rows else (lambda b: b)
    tok = lambda w: pl.BlockSpec((1, tm, w), lambda b, i: (b, i, 0))
    modspec = pl.BlockSpec((1, 1, d), lambda b, i: (row(b), 0, 0))
    const = lambda shape: pl.BlockSpec(shape, lambda b, i: (0,) * len(shape))
    return pl.pallas_call(
        _merge_kernel,
        out_shape=[jax.ShapeDtypeStruct((bsz, t, d), F32), jax.ShapeDtypeStruct((bsz, t, d), BF16),
                   jax.ShapeDtypeStruct((bsz, N_EXPERTS, t), F32)],
        grid=(bsz, t // tm),
        in_specs=[tok(Q_WIDTH), tok(HY_WIDTH), tok(LRU_WIDTH), tok(GATE_WIDTH), tok(d), modspec,
                  const((Q_WIDTH, d)), const((HY_WIDTH, d)), const((LRU_WIDTH, d)), const((d, d)),
                  const((1, d)), modspec, modspec, const((N_EXPERTS, d))],
        out_specs=[tok(d), tok(d), pl.BlockSpec((1, N_EXPERTS, tm), lambda b, i: (b, 0, i))],
        compiler_params=_params(2),
        name="merge",
    )(attn, hyo, lruo, gate, h, g1, wa, wh, wl, wo, n2g, sh2, sc2, rw_t)


def _router_kernel(lt_ref, slot_ref, slotc_ref, g_ref, coff_ref, *, cap):
    lg = lt_ref[0]
    n_e, t = lg.shape
    ex = jnp.exp(lg - lg.max(axis=0, keepdims=True))
    aff = ex / ex.sum(axis=0, keepdims=True)
    key = pltpu.bitcast(aff, I32)
    capf = float(cap)

    def count(mask):
        return jnp.where(mask, 1.0, 0.0).sum(axis=1, keepdims=True)

    def vbody(i, thr):
        cand = thr | lax.shift_left(jnp.int32(1), 29 - i)
        return jnp.where(count(key >= cand) >= capf, cand, thr)

    thr = lax.fori_loop(0, 30, vbody, jnp.zeros((n_e, 1), I32))
    gt = key > thr
    eq = key == thr
    need = capf - count(gt)
    idx = lax.broadcasted_iota(I32, (n_e, t), 1)
    nbits = t.bit_length() - 1

    def ibody(i, lo):
        cand = lo | lax.shift_left(jnp.int32(1), nbits - 1 - i)
        return jnp.where(count(eq & (idx < cand)) < need, cand, lo)

    last = lax.fori_loop(0, nbits, ibody, jnp.zeros((n_e, 1), I32))
    sel = gt | (eq & (idx <= last))
    self32 = jnp.where(sel, 1.0, 0.0)
    g_ref[0] = jnp.where(sel, aff, 0.0)

    r_i = lax.broadcasted_iota(I32, (LANES, LANES), 0)
    c_i = lax.broadcasted_iota(I32, (LANES, LANES), 1)
    tri = jnp.where(r_i <= c_i, 1.0, 0.0).astype(BF16)
    eye = jnp.where(r_i == c_i, 1.0, 0.0).astype(BF16)
    off = jnp.zeros((n_e, 1), F32)
    n_units = t // LANES
    coff_ref[0] = jnp.zeros((n_e, LANES), I32)
    for c in range(n_units):
        sl = slice(c * LANES, (c + 1) * LANES)
        coff_ref[0, :, c:c + 1] = off.astype(I32)
        xs = self32[:, sl]
        inc = _dot(xs.astype(BF16), tri)
        slot1 = jnp.where(sel[:, sl], inc - xs + off + 1.0, 0.0)
        off = off + inc[:, LANES - 1:LANES]
        slot_ref[0, :, sl] = slot1.astype(I32) - 1
        hi = jnp.floor(slot1 * (1.0 / 16.0))
        lo = slot1 - 16.0 * hi
        col = 16.0 * _dot_nt(eye, hi.astype(BF16)) + _dot_nt(eye, lo.astype(BF16))
        slotc_ref[0, sl, :] = col.astype(I32) - 1
    coff_ref[0, :, n_units:n_units + 1] = off.astype(I32)


def _router_call(logits_t, cap):
    bsz, n_e, t = logits_t.shape
    assert t % LANES == 0 and t // LANES < LANES
    row = pl.BlockSpec((1, n_e, t), lambda b: (b, 0, 0))
    slot_row, slot_col, g_row, coff = pl.pallas_call(
        functools.partial(_router_kernel, cap=cap),
        out_shape=[jax.ShapeDtypeStruct((bsz, n_e, t), I32), jax.ShapeDtypeStruct((bsz, t, n_e), I32),
                   jax.ShapeDtypeStruct((bsz, n_e, t), F32), jax.ShapeDtypeStruct((bsz, n_e, LANES), I32)],
        grid=(bsz,),
        in_specs=[row],
        out_specs=[row, pl.BlockSpec((1, t, n_e), lambda b: (b, 0, 0)), row,
                   pl.BlockSpec((1, n_e, LANES), lambda b: (b, 0, 0))],
        compiler_params=_params(1),
        name="router",
    )(logits_t)
    return slot_row, slot_col, g_row, coff[:, :, :t // LANES + 1].reshape(-1)


MOE_GATHER_TOKENS = 256
MOE_SLOT_ROWS = 128
MOE_SCATTER_WINDOW = 128
MOE_SCATTER_SLOTS = 256


def _log2(n):
    assert n & (n - 1) == 0
    return n.bit_length() - 1


def _expert_kernel(coff_ref, u_ref, slot_ref, g_ref, w1_ref, w3_ref, w2_ref, y_ref, xg_scr, gs_scr, *, tkg, sb,
                   n_units):
    cap = xg_scr.shape[0]
    n_chunks = u_ref.shape[1] // tkg
    base = (pl.program_id(0) * pl.num_programs(1) + pl.program_id(1)) * (n_units + 1)
    upc = tkg // LANES
    rid = lax.broadcasted_iota(I32, (sb, tkg), 0)
    xg_scr[...] = jnp.zeros(xg_scr.shape, F32)
    gs_scr[...] = jnp.zeros(gs_scr.shape, F32)

    def gather(match, r0, c_tokens, g_row):
        xg_scr[pl.ds(r0, sb), :] += _dot(jnp.where(match, 1.0, 0.0).astype(BF16), u_ref[0, c_tokens, :])
        gs_scr[pl.ds(r0, sb), :] += jnp.where(match, g_row, 0.0).sum(axis=1, keepdims=True)

    def window_start(c):
        lo = coff_ref[base + c * upc]
        return jnp.minimum(lo & (-SUBLANES), cap - sb)

    for c in range(n_chunks):
        r0 = pl.multiple_of(window_start(c), SUBLANES)
        gather(slot_ref[0, 0, c:c + 1, :] - r0 == rid, r0, slice(c * tkg, (c + 1) * tkg), g_ref[0, 0, c:c + 1, :])

    def overflow(c, carry):
        covered = window_start(c) + sb
        hi = coff_ref[base + (c + 1) * upc]
        first = lax.shift_right_logical(covered, _log2(sb))
        stop = jnp.where(hi > covered, lax.shift_right_logical(hi + (sb - 1), _log2(sb)), first)

        def block(j, carry2):
            b0 = pl.multiple_of(j * sb, sb)
            srow = slot_ref[0, 0, pl.ds(c, 1), :]
            gather((srow - b0 == rid) & (srow >= covered), b0, pl.ds(pl.multiple_of(c * tkg, tkg), tkg),
                   g_ref[0, 0, pl.ds(c, 1), :])
            return carry2

        return lax.fori_loop(first, stop, block, carry)

    lax.fori_loop(0, n_chunks, overflow, 0)
    xb = xg_scr[...].astype(BF16)
    hid = (_silu(_dot(xb, w1_ref[0, 0])) * _dot(xb, w3_ref[0, 0])).astype(BF16)
    y_ref[0, 0] = (_dot(hid, w2_ref[0, 0]) * gs_scr[...]).astype(BF16)


def _expert_call(u2, slot_row, g_row, coff, w1, w3, w2, layer, cap):
    bsz, t, d = u2.shape
    n_e = w1.shape[1]
    tkg = min(MOE_GATHER_TOKENS, t)
    sb = min(MOE_SLOT_ROWS, cap)
    wspec = pl.BlockSpec((1, 1, d, d), lambda b, e, co: (layer, e, 0, 0))
    chunks = pl.BlockSpec((1, 1, t // tkg, tkg), lambda b, e, co: (b, e, 0, 0))
    grid_spec = pltpu.PrefetchScalarGridSpec(
        num_scalar_prefetch=1,
        grid=(bsz, n_e),
        in_specs=[pl.BlockSpec((1, t, d), lambda b, e, co: (b, 0, 0)),
                  chunks, chunks, wspec, wspec, wspec],
        out_specs=pl.BlockSpec((1, 1, cap, d), lambda b, e, co: (b, e, 0, 0)),
        scratch_shapes=[pltpu.VMEM((cap, d), F32), pltpu.VMEM((cap, 1), F32)],
    )
    return pl.pallas_call(
        functools.partial(_expert_kernel, tkg=tkg, sb=sb, n_units=t // LANES),
        out_shape=jax.ShapeDtypeStruct((bsz, n_e, cap, d), BF16),
        grid_spec=grid_spec,
        compiler_params=_params(2),
        name="expert",
    )(coff, u2, slot_row.reshape(bsz, n_e, t // tkg, tkg), g_row.reshape(bsz, n_e, t // tkg, tkg), w1, w3, w2)


def _expert_call_folded(u2, slot_row, g_row, coff, w1, w3, w2, layer, cap):
    bsz, t, d = u2.shape
    n_e = slot_row.shape[1]
    n_units = t // LANES
    first = (jnp.arange(bsz, dtype=I32) * cap)[:, None, None]
    slot_all = jnp.where(slot_row >= 0, slot_row + first, -1).transpose(1, 0, 2).reshape(1, n_e, bsz * t)
    coff_all = (coff.reshape(bsz, n_e, n_units + 1)[:, :, :n_units] + first).transpose(1, 0, 2)
    coff_all = jnp.concatenate([coff_all.reshape(n_e, bsz * n_units), jnp.full((n_e, 1), bsz * cap, I32)], axis=1)
    g_all = g_row.transpose(1, 0, 2).reshape(1, n_e, bsz * t)
    y = _expert_call(u2.reshape(1, bsz * t, d), slot_all, g_all, coff_all.reshape(-1), w1, w3, w2, layer, bsz * cap)
    return y.reshape(n_e, bsz, cap, d).transpose(1, 0, 2, 3)


def _scatter_kernel(coff_ref, h_ref, y_ref, slotc_ref, g2_ref, fg_ref, o_ref, acc_scr, ystack, *, win, group, kb,
                    n_units, final_norm):
    tk = h_ref.shape[1]
    cap = y_ref.shape[2]
    upt = tk // LANES
    i = pl.program_id(1)
    lane = lax.broadcasted_iota(I32, (tk, win), 1)
    bf16_rows = 2 * SUBLANES

    def span(e):
        base = (pl.program_id(0) * N_EXPERTS + e) * (n_units + 1)
        lo = coff_ref[base + i * upt]
        hi = coff_ref[base + (i + 1) * upt]
        return jnp.minimum(lo & (-bf16_rows), cap - win), hi

    acc = jnp.zeros((tk, D_MODEL), F32)
    for p in range(N_EXPERTS // group):
        hots = []
        for j in range(group):
            e = p * group + j
            r0 = pl.multiple_of(span(e)[0], bf16_rows)
            ystack[p, j * win:(j + 1) * win, :] = y_ref[0, e, pl.ds(r0, win), :]
            hots.append(jnp.where(slotc_ref[0, :, e:e + 1] - r0 == lane, 1.0, 0.0).astype(BF16))
        onehot = hots[0] if group == 1 else jnp.concatenate(hots, axis=1)
        acc = acc + _dot(onehot, ystack[p])
    acc_scr[...] = acc

    lane_kb = lax.broadcasted_iota(I32, (tk, kb), 1)
    for e in range(N_EXPERTS):
        r0, hi = span(e)
        covered = r0 + win
        first = lax.shift_right_logical(covered, _log2(kb))
        stop = jnp.where(hi > covered, lax.shift_right_logical(hi + (kb - 1), _log2(kb)), first)

        def block(j, carry, e=e, covered=covered):
            b0 = pl.multiple_of(j * kb, kb)
            scol = slotc_ref[0, :, e:e + 1]
            onehot = jnp.where((scol - b0 == lane_kb) & (scol >= covered), 1.0, 0.0).astype(BF16)
            acc_scr[...] += _dot(onehot, y_ref[0, e, pl.ds(b0, kb), :])
            return carry

        lax.fori_loop(first, stop, block, 0)
    out = h_ref[0] + g2_ref[0] * acc_scr[...]
    o_ref[0] = _rms(out) * fg_ref[...] if final_norm else out


def _scatter_call(h, y, slot_col, coff, g2, cap, ctx_rows, final_g=None):
    bsz, t, d = h.shape
    n_e = y.shape[1]
    tk = min(512, t)
    win = min(MOE_SCATTER_WINDOW, cap)
    group = 2 if win == LANES else 1
    kb = min(MOE_SCATTER_SLOTS, cap)
    row = (lambda b: MOD_ROWS - 8) if ctx_rows else (lambda b: b)
    tile = lambda w: pl.BlockSpec((1, tk, w), lambda b, i, co: (b, i, 0))
    grid_spec = pltpu.PrefetchScalarGridSpec(
        num_scalar_prefetch=1,
        grid=(bsz, t // tk),
        in_specs=[tile(d),
                  pl.BlockSpec((1, n_e, cap, d), lambda b, i, co: (b, 0, 0, 0)),
                  tile(n_e), pl.BlockSpec((1, 1, d), lambda b, i, co: (row(b), 0, 0)),
                  pl.BlockSpec((1, d), lambda b, i, co: (0, 0))],
        out_specs=tile(d),
        scratch_shapes=[pltpu.VMEM((tk, d), F32), pltpu.VMEM((n_e // group, group * win, d), BF16)],
    )
    return pl.pallas_call(
        functools.partial(_scatter_kernel, win=win, group=group, kb=kb, n_units=t // LANES,
                          final_norm=final_g is not None),
        out_shape=jax.ShapeDtypeStruct((bsz, t, d), F32),
        grid_spec=grid_spec,
        compiler_params=_params(2),
        name="moe_scatter",
    )(coff, h, y, slot_col, g2, jnp.ones((1, d), F32) if final_g is None else final_g)


def _rope_tables(n_lat, n_ctx):
    rows = n_lat // GRID_W
    row = jnp.repeat(jnp.arange(rows, dtype=F32), GRID_W)
    col = jnp.tile(jnp.arange(GRID_W, dtype=F32), rows)
    inv = jnp.power(ROPE_THETA, -jnp.arange(ROPE_PAIRS_AXIS, dtype=F32) / ROPE_PAIRS_AXIS)
    ang = jnp.concatenate([row[:, None] * inv, col[:, None] * inv], axis=-1)
    cos, sin = jnp.cos(ang), jnp.sin(ang)
    cs = jnp.concatenate([cos, cos], axis=-1)
    sn = jnp.concatenate([-sin, sin], axis=-1)
    return cs, sn, jnp.ones((n_ctx, HEAD_DIM), F32), jnp.zeros((n_ctx, HEAD_DIM), F32)


def _dft_tables(n):
    k = jnp.arange(n // 2, dtype=I32)

    def tables(first):
        m = ((2 * k[:, None] + 1) * (2 * k[None, :] + first)) % (4 * n)
        ang = m.astype(F32) * (2.0 * math.pi / (4 * n))
        return jnp.cos(ang), jnp.sin(ang)

    ce, se = tables(0)
    co, so = tables(1)
    fwd = tuple(x.astype(BF16) for x in (ce, co, se, so))
    inv = tuple(x.astype(BF16) for x in (ce.T, co, -se.T, -so))
    return fwd, inv


def _filter_features(n):
    t = jnp.linspace(0.0, 1.0, n, dtype=F32)[:, None]
    w = (2.0 * math.pi / n) * jnp.arange(n, dtype=F32)[:, None]
    f = jnp.linspace(1e-4, HY_BANDS - 1, HY_BANDS, dtype=F32)[None, :]
    feats = jnp.concatenate([t, jnp.cos(f * w), -jnp.sin(f * w)], axis=-1)
    feats = jnp.concatenate([feats[0::2], feats[1::2]], axis=0)
    return jnp.pad(feats, ((0, 0), (0, LANES - HY_EMB)))


def _pad_to(x, shape):
    return jnp.pad(x, [(0, s - d) for d, s in zip(x.shape, shape)])


def _block_diag_chunks(w):
    per = LANES // LRU_BLOCK
    w = w.reshape(2, LRU_BLOCKS // per, per, LRU_BLOCK, LRU_BLOCK)
    eye = jnp.eye(per, dtype=w.dtype)
    return jnp.einsum('dcpkj,pq->dcpkqj', w, eye).reshape(2, LRU_BLOCKS // per, LANES, LANES)


def kernel(x, c, ctx, c_ctx, mod_w, mod_b, norm1_g, norm2_g, w_in, q_norm_g, k_norm_g, hy_conv_w, hy_conv_b,
           hy_fw1, hy_fb1, hy_fw2, hy_fb2, hy_fw3, hy_freq, hy_bias, lru_conv_w, lru_conv_b, lru_wa, lru_ba,
           lru_wx, lru_bx, lru_lambda, w_attn_out, w_hy_out, w_lru_out, w_out, router_w, exp_w1, exp_w3, exp_w2,
           final_norm_g):
    bsz, n_lat, d = x.shape
    n_ctx = ctx.shape[1]
    depth = mod_w.shape[0]
    assert d == D_MODEL and bsz <= MOD_ROWS - 8 and n_lat % GRID_W == 0

    cs_l, sn_l, cs_c, sn_c = _rope_tables(n_lat, n_ctx)
    dft_l = _dft_tables(n_lat)
    dft_c = _dft_tables(n_ctx)
    feats_l = _filter_features(n_lat)
    feats_c = _filter_features(n_ctx)
    max_decay = math.log(HY_DECAY_TARGET) / HY_FAST_DECAY_PCT
    min_decay = math.log(HY_DECAY_TARGET) / HY_SLOW_DECAY_PCT
    deltas_abs = jnp.abs(jnp.linspace(min_decay, max_decay, HY_WIDTH, dtype=F32))[None, :]
    cap_l = max(1, EC_CAPACITY * n_lat // N_EXPERTS)
    cap_c = max(1, EC_CAPACITY * n_ctx // N_EXPERTS)

    cvec = jnp.zeros((MOD_ROWS, d), F32).at[:bsz].set(c).at[MOD_ROWS - 8].set(c_ctx)
    mod = _mod_call(cvec, mod_w, mod_b)
    mod = mod.reshape(depth, MOD_ROWS, 6, 1, d).transpose(0, 2, 1, 3, 4)

    w_in_b = w_in.astype(BF16)
    e1 = exp_w1.astype(BF16)
    e3 = exp_w3.astype(BF16)
    e2 = exp_w2.astype(BF16)

    h, hc = x, ctx
    for l in range(depth):
        last = l == depth - 1
        sh1, sc1, g1, sh2, sc2, g2 = (mod[l, j] for j in range(6))
        n1 = norm1_g[l][None, :]
        n2 = norm2_g[l][None, :]
        qg = q_norm_g[l][None, :]
        kg = k_norm_g[l][None, :]

        q, k, v, hy, rx, rg, gate = _inproj_call(h, sh1, sc1, n1, w_in_b, l, cs_l, sn_l, qg, kg, False)
        qc, kc, vc, hyc, rxc, rgc, gatec = _inproj_call(hc, sh1, sc1, n1, w_in_b, l, cs_c, sn_c, qg, kg, True)

        attn = _attn_call(q, [(k, v), (kc, vc)])

        fw1 = _pad_to(hy_fw1[l], (LANES, LANES))
        fb1 = _pad_to(hy_fb1[l][None, :], (1, LANES))
        fw2 = _pad_to(hy_fw2[l], (LANES, LANES))
        fb2 = _pad_to(hy_fb2[l][None, :], (1, LANES))
        fw3 = _pad_to(hy_fw3[l], (LANES, 2 * HY_WIDTH))
        freq = _pad_to(hy_freq[l], (2, LANES))
        hbias = hy_bias[l][None, :]

        def hyena(hy_in, feats, tabs):
            tabs_fwd, tabs_inv = tabs
            hsd = _hy_filter_call(feats, fw1, fb1, fw2, fb2, fw3, freq, deltas_abs)
            kspec = _hy_kspec_call(tabs_fwd, hsd)
            x0, we, wo = _hy_pre_call(hy_in, hy_conv_w[l], hy_conv_b[l][None, :])
            us = _hy_fwd_call(tabs_fwd, we, wo, kspec)
            return _hy_inv_call(tabs_inv, us, x0, we, wo, hbias)

        hyo = hyena(hy, feats_l, dft_l)

        lruo, lruoc = _lru_call(rx, rxc, rg, rgc, lru_conv_w[l], lru_conv_b[l][None, :],
                                _block_diag_chunks(lru_wa[l]).astype(BF16), _block_diag_chunks(lru_wx[l]).astype(BF16),
                                lru_ba[l][:, None, :], lru_bx[l][:, None, :], lru_lambda[l][:, None, :])

        wa = w_attn_out[l].astype(BF16)
        wh = w_hy_out[l].astype(BF16)
        wl = w_lru_out[l].astype(BF16)
        wo = w_out[l].astype(BF16)
        rw_t = router_w[l].T.astype(BF16)

        def channel_mix(h_in, attn_in, hy_in, lru_in, gate_in, cap, ctx_rows, final_g=None):
            hn, u2, lt = _merge_call(attn_in, hy_in, lru_in, gate_in, h_in, g1, wa, wh, wl, wo, n2, sh2, sc2, rw_t,
                                     ctx_rows)
            slot_row, slot_col, g_row, coff = _router_call(lt, cap)
            if ctx_rows:
                y = _expert_call_folded(u2, slot_row, g_row, coff, e1, e3, e2, l, cap)
            else:
                y = _expert_call(u2, slot_row, g_row, coff, e1, e3, e2, l, cap)
            return _scatter_call(hn, y, slot_col, coff, g2, cap, ctx_rows, final_g)

        h = channel_mix(h, attn, hyo, lruo, gate, cap_l, False, final_norm_g[None, :] if last else None)
        if not last:
            attn_c = _attn_call(qc, [(kc, vc)])
            hyo_c = hyena(hyc, feats_c, dft_c)
            hc = channel_mix(hc, attn_c, hyo_c, lruoc, gatec, cap_c, True)

    return h
```

```python
import functools
import math

import jax
import jax.numpy as jnp
from jax import lax
from jax.experimental import pallas as pl
from jax.experimental.pallas import tpu as pltpu

F32 = jnp.float32
BF16 = jnp.bfloat16
I32 = jnp.int32

D_MODEL = 1024
GRID_W = 64
NORM_EPS = 1e-6
N_HEADS = 8
N_KV_HEADS = 2
HEAD_DIM = 128
GROUP = N_HEADS // N_KV_HEADS
ROPE_PAIRS_AXIS = HEAD_DIM // 4
ROPE_THETA = 10000.0
ATTN_SCALE = HEAD_DIM ** -0.5
Q_PRESCALE = ATTN_SCALE * math.log2(math.e)
Q_WIDTH = N_HEADS * HEAD_DIM
KV_WIDTH = N_KV_HEADS * HEAD_DIM
HY_WIDTH = D_MODEL // 2
HY_BANDS = 16
HY_EMB = 1 + 2 * HY_BANDS
HY_FILTER_HIDDEN = 64
HY_FAST_DECAY_PCT = 0.3
HY_SLOW_DECAY_PCT = 1.5
HY_DECAY_TARGET = 1e-2
LRU_WIDTH = D_MODEL // 2
LRU_BLOCKS = 8
LRU_BLOCK = LRU_WIDTH // LRU_BLOCKS
LRU_C = 8.0
N_BRANCH = 3
N_EXPERTS = 16
EC_CAPACITY = 2
GATE_WIDTH = N_BRANCH * D_MODEL
OFF_Q = 0
OFF_K = OFF_Q + Q_WIDTH
OFF_V = OFF_K + KV_WIDTH
OFF_HY = OFF_V + KV_WIDTH
OFF_RX = OFF_HY + 3 * HY_WIDTH
OFF_RG = OFF_RX + LRU_WIDTH
OFF_GATE = OFF_RG + LRU_WIDTH
IN_TOTAL = OFF_GATE + GATE_WIDTH

LANES = 128
SUBLANES = 8
V7X_VMEM_LIMIT_BYTES = 56 * 1024 * 1024
MOD_ROWS = 24


def _params(n_axes, vmem=V7X_VMEM_LIMIT_BYTES):
    return pltpu.CompilerParams(dimension_semantics=("arbitrary",) * n_axes, vmem_limit_bytes=vmem)


def _dot(a, b):
    return jnp.dot(a, b, preferred_element_type=F32)


def _dot_nt(a, b):
    return lax.dot_general(a, b, (((1,), (1,)), ((), ())), preferred_element_type=F32)


def _split2(x):
    hi = x.astype(BF16)
    lo = (x - hi.astype(F32)).astype(BF16)
    return hi, lo


def _dot3(a, b):
    ah, al = _split2(a)
    bh, bl = _split2(b)
    return _dot(ah, bh) + (_dot(ah, bl) + _dot(al, bh))


def _rms(x):
    return x * lax.rsqrt(jnp.mean(x * x, axis=-1, keepdims=True) + NORM_EPS)


def _sigmoid(x):
    return 0.5 * jnp.tanh(0.5 * x) + 0.5


def _silu(x):
    return x * _sigmoid(x)


def _gelu_tanh(x):
    return 0.5 * x * (1.0 + jnp.tanh(math.sqrt(2.0 / math.pi) * (x + 0.044715 * (x * x * x))))


def _shift_down(x, k):
    row = lax.broadcasted_iota(I32, x.shape, 0)
    return jnp.where(row >= k, pltpu.roll(x, k, 0), 0.0)


def _shift_up(x, k):
    n = x.shape[0]
    row = lax.broadcasted_iota(I32, x.shape, 0)
    return jnp.where(row < n - k, pltpu.roll(x, n - k, 0), 0.0)


def _mod_kernel(c_ref, w_ref, b_ref, o_ref):
    o_ref[0] = _dot3(_silu(c_ref[...]), w_ref[0]) + b_ref[0]


def _mod_call(cvec, mod_w, mod_b):
    depth, d, six_d = mod_w.shape
    tn = 1536
    return pl.pallas_call(
        _mod_kernel,
        out_shape=jax.ShapeDtypeStruct((depth, MOD_ROWS, six_d), F32),
        grid=(depth, six_d // tn),
        in_specs=[
            pl.BlockSpec((MOD_ROWS, d), lambda l, j: (0, 0)),
            pl.BlockSpec((1, d, tn), lambda l, j: (l, 0, j)),
            pl.BlockSpec((1, 1, tn), lambda l, j: (l, 0, j)),
        ],
        out_specs=pl.BlockSpec((1, MOD_ROWS, tn), lambda l, j: (l, 0, j)),
        compiler_params=_params(2),
        name="mod",
    )(cvec, mod_w, mod_b.reshape(depth, 1, six_d))


def _inproj_kernel(h_ref, sh_ref, sc_ref, g_ref, w_ref, cs_ref, sn_ref, qg_ref, kg_ref,
                   q_ref, k_ref, v_ref, hy_ref, rx_ref, rg_ref, gate_ref):
    u = (_rms(h_ref[0]) * g_ref[...] * (1.0 + sc_ref[0]) + sh_ref[0]).astype(BF16)
    cs = cs_ref[...]
    sn = sn_ref[...]

    def normed_rope(z, g):
        r = _rms(z) * g
        return r * cs + pltpu.roll(r, HEAD_DIM // 2, 1) * sn

    zq = _dot(u, w_ref[0, :, OFF_Q:OFF_Q + Q_WIDTH])
    for hd in range(N_HEADS):
        sl = slice(hd * HEAD_DIM, (hd + 1) * HEAD_DIM)
        q_ref[0, hd] = (normed_rope(zq[:, sl], qg_ref[...]) * Q_PRESCALE).astype(BF16)
    zkv = _dot(u, w_ref[0, :, OFF_K:OFF_K + 2 * KV_WIDTH])
    for hd in range(N_KV_HEADS):
        sl = slice(hd * HEAD_DIM, (hd + 1) * HEAD_DIM)
        k_ref[0, :, sl] = normed_rope(zkv[:, sl], kg_ref[...]).astype(BF16)
    v_ref[0] = zkv[:, KV_WIDTH:].astype(BF16)
    for j in range(3):
        sl = slice(j * HY_WIDTH, (j + 1) * HY_WIDTH)
        hy_ref[0, :, sl] = _dot(u, w_ref[0, :, OFF_HY + j * HY_WIDTH:OFF_HY + (j + 1) * HY_WIDTH]).astype(BF16)
    rr = _dot(u, w_ref[0, :, OFF_RX:OFF_RX + 2 * LRU_WIDTH])
    rx_ref[0] = rr[:, :LRU_WIDTH]
    rg_ref[0] = _gelu_tanh(rr[:, LRU_WIDTH:]).astype(BF16)
    for j in range(N_BRANCH):
        sl = slice(j * D_MODEL, (j + 1) * D_MODEL)
        gate_ref[0, :, sl] = _dot(u, w_ref[0, :, OFF_GATE + j * D_MODEL:OFF_GATE + (j + 1) * D_MODEL]).astype(BF16)


def _inproj_call(h, sh, sc, g, w_bf, layer, cs, sn, qg, kg, ctx_rows):
    bsz, t, d = h.shape
    tm = min(512, t)
    row = (lambda b: MOD_ROWS - 8) if ctx_rows else (lambda b: b)
    tok = lambda w: pl.BlockSpec((1, tm, w), lambda b, i: (b, i, 0))
    modspec = pl.BlockSpec((1, 1, d), lambda b, i: (row(b), 0, 0))
    const = lambda shape: pl.BlockSpec(shape, lambda b, i: (0,) * len(shape))
    outs = [(KV_WIDTH, BF16), (KV_WIDTH, BF16), (3 * HY_WIDTH, BF16), (LRU_WIDTH, F32), (LRU_WIDTH, BF16),
            (GATE_WIDTH, BF16)]
    return pl.pallas_call(
        _inproj_kernel,
        out_shape=[jax.ShapeDtypeStruct((bsz, N_HEADS, t, HEAD_DIM), BF16)]
        + [jax.ShapeDtypeStruct((bsz, t, w), dt) for w, dt in outs],
        grid=(bsz, t // tm),
        in_specs=[
            tok(d), modspec, modspec, const((1, d)),
            pl.BlockSpec((1, d, IN_TOTAL), lambda b, i: (layer, 0, 0), pipeline_mode=pl.Buffered(1)),
            pl.BlockSpec((tm, HEAD_DIM), lambda b, i: (i, 0)),
            pl.BlockSpec((tm, HEAD_DIM), lambda b, i: (i, 0)),
            const((1, HEAD_DIM)), const((1, HEAD_DIM)),
        ],
        out_specs=[pl.BlockSpec((1, N_HEADS, tm, HEAD_DIM), lambda b, i: (b, 0, i, 0))] + [tok(w) for w, _ in outs],
        compiler_params=_params(2),
        name="inproj",
    )(h, sh, sc, g, w_bf, cs, sn, qg, kg)


ATTN_TQ = 1024
ATTN_TK = 512
ATTN_STREAMS = 4
NEG_BIG = -1e30


def _attn_kernel(*refs, n_src, tq):
    q_ref = refs[0]
    kv = refs[1:1 + 2 * n_src]
    o_ref, kcat, vext = refs[1 + 2 * n_src:]
    n_keys = kcat.shape[0]

    @pl.when(pl.program_id(2) == 0)
    def _():
        off = 0
        for i in range(n_src):
            n = kv[2 * i].shape[1]
            kcat[off:off + n, :] = kv[2 * i][0]
            vext[off:off + n, 0:HEAD_DIM] = kv[2 * i + 1][0]
            off += n
        vext[:, HEAD_DIM:] = jnp.ones((n_keys, HEAD_DIM), BF16)

    rows = GROUP * tq
    q = q_ref[0].reshape(rows, HEAD_DIM)
    per = rows // ATTN_STREAMS
    qs = [q[i * per:(i + 1) * per] for i in range(ATTN_STREAMS)]
    ms = [jnp.full((per, 1), NEG_BIG, F32) for _ in qs]
    accs = [jnp.zeros((per, 2 * HEAD_DIM), F32) for _ in qs]
    for off in range(0, n_keys, ATTN_TK):
        size = min(ATTN_TK, n_keys - off)
        kk = kcat[off:off + size, :]
        vv = vext[off:off + size, :]
        for i in range(ATTN_STREAMS):
            s = _dot_nt(qs[i], kk)
            m_new = jnp.maximum(ms[i], s.max(axis=-1, keepdims=True))
            p = jnp.exp2(s - m_new).astype(BF16)
            accs[i] = jnp.exp2(ms[i] - m_new) * accs[i] + _dot(p, vv)
            ms[i] = m_new
    out = jnp.concatenate([a[:, :HEAD_DIM] / a[:, HEAD_DIM:] for a in accs], axis=0).astype(BF16)
    for g in range(GROUP):
        o_ref[0, :, g * HEAD_DIM:(g + 1) * HEAD_DIM] = out[g * tq:(g + 1) * tq]


def _attn_call(q, kvs):
    bsz, _, tq_all, _ = q.shape
    tq = min(ATTN_TQ, tq_all)
    gw = GROUP * HEAD_DIM
    in_specs = [pl.BlockSpec((1, GROUP, tq, HEAD_DIM), lambda b, h, i: (b, h, i, 0))]
    args = [q]
    n_keys = 0
    for k, v in kvs:
        tk = k.shape[1]
        n_keys += tk
        spec = pl.BlockSpec((1, tk, HEAD_DIM), lambda b, h, i: (b, 0, h))
        in_specs += [spec, spec]
        args += [k, v]
    return pl.pallas_call(
        functools.partial(_attn_kernel, n_src=len(kvs), tq=tq),
        out_shape=jax.ShapeDtypeStruct((bsz, tq_all, Q_WIDTH), BF16),
        grid=(bsz, N_KV_HEADS, tq_all // tq),
        in_specs=in_specs,
        out_specs=pl.BlockSpec((1, tq, gw), lambda b, h, i: (b, i, h)),
        scratch_shapes=[pltpu.VMEM((n_keys, HEAD_DIM), BF16), pltpu.VMEM((n_keys, 2 * HEAD_DIM), BF16)],
        compiler_params=_params(3),
        name="attn",
    )(*args)


def _hy_pre_kernel(h0_ref, h1_ref, h2_ref, w0_ref, w1_ref, w2_ref, b0_ref, b1_ref, b2_ref,
                   x0_ref, we_ref, wo_ref, w_scr):
    def conv(x_ref, cw_ref, cb_ref):
        x = x_ref[0].astype(F32)
        cw = cw_ref[...]
        return cb_ref[...] + _shift_down(x, 1) * cw[0:1] + x * cw[1:2] + _shift_up(x, 1) * cw[2:3]

    x0_ref[0] = conv(h0_ref, w0_ref, b0_ref).astype(BF16)
    w_scr[...] = conv(h2_ref, w2_ref, b2_ref) * conv(h1_ref, w1_ref, b1_ref)
    half = w_scr.shape[0] // 2
    we_ref[0] = w_scr[pl.ds(0, half, stride=2), :].astype(BF16)
    wo_ref[0] = w_scr[pl.ds(1, half, stride=2), :].astype(BF16)


def _hy_pre_call(hy, conv_w, conv_b):
    bsz, t, _ = hy.shape
    nch = HY_WIDTH // LANES
    xs = lambda part: pl.BlockSpec((1, t, LANES), lambda b, j: (b, 0, part * nch + j))
    ws = lambda part: pl.BlockSpec((3, LANES), lambda b, j: (0, part * nch + j))
    bs = lambda part: pl.BlockSpec((1, LANES), lambda b, j: (0, part * nch + j))
    out = pl.BlockSpec((1, t, LANES), lambda b, j: (b, 0, j))
    outh = pl.BlockSpec((1, t // 2, LANES), lambda b, j: (b, 0, j))
    return pl.pallas_call(
        _hy_pre_kernel,
        out_shape=[jax.ShapeDtypeStruct((bsz, t, HY_WIDTH), BF16),
                   jax.ShapeDtypeStruct((bsz, t // 2, HY_WIDTH), BF16),
                   jax.ShapeDtypeStruct((bsz, t // 2, HY_WIDTH), BF16)],
        grid=(bsz, nch),
        in_specs=[xs(0), xs(1), xs(2), ws(0), ws(1), ws(2), bs(0), bs(1), bs(2)],
        out_specs=[out, outh, outh],
        scratch_shapes=[pltpu.VMEM((t, LANES), F32)],
        compiler_params=_params(2),
        name="hy_pre",
    )(hy, hy, hy, conv_w, conv_w, conv_w, conv_b, conv_b, conv_b)


def _hy_filter_kernel(feat_ref, w1_ref, b1_ref, w2_ref, b2_ref, w3_ref, fr_ref, dl_ref, o_ref):
    feats = feat_ref[...]
    hid = jnp.sin(fr_ref[0:1] * (_dot3(feats, w1_ref[...]) + b1_ref[...]))
    hid = jnp.sin(fr_ref[1:2] * (_dot3(hid, w2_ref[...]) + b2_ref[...]))
    filt = _dot3(hid, w3_ref[...])
    decay = jnp.exp(-feats[:, 0:1] * dl_ref[...])
    h_fwd = filt[:, :HY_WIDTH] * decay
    h_bwd = filt[:, HY_WIDTH:] * decay
    o_ref[:, :HY_WIDTH] = h_fwd + h_bwd
    o_ref[:, HY_WIDTH:] = h_bwd - h_fwd


def _hy_filter_call(feats, fw1, fb1, fw2, fb2, fw3, freq, deltas_abs):
    n = feats.shape[0]
    tn = min(512, n)
    const = lambda shape: pl.BlockSpec(shape, lambda i: (0,) * len(shape))
    return pl.pallas_call(
        _hy_filter_kernel,
        out_shape=jax.ShapeDtypeStruct((n, 2 * HY_WIDTH), F32),
        grid=(n // tn,),
        in_specs=[pl.BlockSpec((tn, LANES), lambda i: (i, 0)), const((LANES, LANES)), const((1, LANES)),
                  const((LANES, LANES)), const((1, LANES)), const((LANES, 2 * HY_WIDTH)), const((2, LANES)),
                  const((1, HY_WIDTH))],
        out_specs=pl.BlockSpec((tn, 2 * HY_WIDTH), lambda i: (i, 0)),
        compiler_params=_params(1),
        name="hy_filter",
    )(feats, fw1, fb1, fw2, fb2, fw3, freq, deltas_abs)


def _hy_kspec_kernel(ce_ref, co_ref, se_ref, so_ref, he_ref, ho_ref, krl_ref, kil_ref, krh_ref, kih_ref, *, scale):
    def branch(tab_e, tab_o, cols):
        he_hi, he_lo = _split2(he_ref[:, cols])
        ho_hi, ho_lo = _split2(ho_ref[:, cols])
        pe = _dot(tab_e[...], he_hi) + _dot(tab_e[...], he_lo)
        po = _dot(tab_o[...], ho_hi) + _dot(tab_o[...], ho_lo)
        return pe, po

    pc, qc = branch(ce_ref, co_ref, slice(0, HY_WIDTH))
    ps, qs = branch(se_ref, so_ref, slice(HY_WIDTH, 2 * HY_WIDTH))
    krl_ref[...] = (pc + qc) * scale
    krh_ref[...] = (pc - qc) * scale
    kil_ref[...] = (ps + qs) * scale
    kih_ref[...] = (qs - ps) * scale


def _hy_kspec_call(tabs, hsd):
    n = hsd.shape[0]
    half = n // 2
    tf = min(512, half)
    tab = pl.BlockSpec((tf, half), lambda i: (i, 0))
    out = pl.BlockSpec((tf, HY_WIDTH), lambda i: (i, 0))
    return pl.pallas_call(
        functools.partial(_hy_kspec_kernel, scale=1.0 / n),
        out_shape=[jax.ShapeDtypeStruct((half, HY_WIDTH), F32)] * 4,
        grid=(half // tf,),
        in_specs=[tab, tab, tab, tab, pl.BlockSpec((half, 2 * HY_WIDTH), lambda i: (0, 0)),
                  pl.BlockSpec((half, 2 * HY_WIDTH), lambda i: (1, 0))],
        out_specs=[out] * 4,
        compiler_params=_params(1),
        name="hy_kspec",
    )(*tabs, hsd, hsd)


def _hy_fwd_kernel(ce_ref, co_ref, se_ref, so_ref, we_ref, wo_ref, krl_ref, kil_ref, krh_ref, kih_ref,
                   urp_ref, urm_ref, uim_ref, uip_ref):
    we = we_ref[0]
    wo = wo_ref[0]
    pc = _dot(ce_ref[...], we)
    qc = _dot(co_ref[...], wo)
    ps = _dot(se_ref[...], we)
    qs = _dot(so_ref[...], wo)
    a_lo, a_hi = pc + qc, pc - qc
    b_lo, b_hi = ps + qs, qs - ps
    krl, kil, krh, kih = krl_ref[...], kil_ref[...], krh_ref[...], kih_ref[...]
    yre_lo = a_lo * krl + b_lo * kil
    yim_lo = a_lo * kil - b_lo * krl
    yre_hi = a_hi * krh + b_hi * kih
    yim_hi = a_hi * kih - b_hi * krh
    urp_ref[0] = (yre_lo + yre_hi).astype(BF16)
    urm_ref[0] = (yre_lo - yre_hi).astype(BF16)
    uim_ref[0] = (yim_lo - yim_hi).astype(BF16)
    uip_ref[0] = (yim_lo + yim_hi).astype(BF16)


def _hy_fwd_call(tabs, we, wo, kspec):
    bsz, half, _ = we.shape
    tf = min(512, half)
    tab = pl.BlockSpec((tf, half), lambda i, b: (i, 0))
    sig = pl.BlockSpec((1, half, HY_WIDTH), lambda i, b: (b, 0, 0))
    kk = pl.BlockSpec((tf, HY_WIDTH), lambda i, b: (i, 0))
    out = pl.BlockSpec((1, tf, HY_WIDTH), lambda i, b: (b, i, 0))
    return pl.pallas_call(
        _hy_fwd_kernel,
        out_shape=[jax.ShapeDtypeStruct((bsz, half, HY_WIDTH), BF16)] * 4,
        grid=(half // tf, bsz),
        in_specs=[tab, tab, tab, tab, sig, sig, kk, kk, kk, kk],
        out_specs=[out] * 4,
        compiler_params=_params(2),
        name="hy_fwd",
    )(*tabs, we, wo, *kspec)


def _hy_inv_kernel(cte_ref, cto_ref, ste_ref, sto_ref, urp_ref, urm_ref, uim_ref, uip_ref, x0_ref, we_ref,
                   wo_ref, bias_ref, o_ref, y_scr):
    tt = cte_ref.shape[0]
    bias = bias_ref[...]
    y_even = _dot(cte_ref[...], urp_ref[0]) + _dot(ste_ref[...], uim_ref[0]) + we_ref[0].astype(F32) * bias
    y_odd = _dot(cto_ref[...], urm_ref[0]) + _dot(sto_ref[...], uip_ref[0]) + wo_ref[0].astype(F32) * bias
    for j in range(HY_WIDTH // LANES):
        sl = slice(j * LANES, (j + 1) * LANES)
        y_scr[j, pl.ds(0, tt, stride=2), :] = y_even[:, sl]
        y_scr[j, pl.ds(1, tt, stride=2), :] = y_odd[:, sl]
        o_ref[0, :, sl] = (x0_ref[0, :, sl].astype(F32) * y_scr[j]).astype(BF16)


def _hy_inv_call(tabs_t, us, x0, we, wo, bias):
    bsz, n, _ = x0.shape
    half = n // 2
    tt = min(512, half)
    tab = pl.BlockSpec((tt, half), lambda i, b: (i, 0))
    full = pl.BlockSpec((1, half, HY_WIDTH), lambda i, b: (b, 0, 0))
    tile = pl.BlockSpec((1, 2 * tt, HY_WIDTH), lambda i, b: (b, i, 0))
    htile = pl.BlockSpec((1, tt, HY_WIDTH), lambda i, b: (b, i, 0))
    return pl.pallas_call(
        _hy_inv_kernel,
        out_shape=jax.ShapeDtypeStruct((bsz, n, HY_WIDTH), BF16),
        grid=(half // tt, bsz),
        in_specs=[tab, tab, tab, tab, full, full, full, full, tile, htile, htile,
                  pl.BlockSpec((1, HY_WIDTH), lambda i, b: (0, 0))],
        out_specs=tile,
        scratch_shapes=[pltpu.VMEM((HY_WIDTH // LANES, 2 * tt, LANES), F32)],
        compiler_params=_params(2),
        name="hy_inv",
    )(*tabs_t, *us, x0, we, wo, bias)


LRU_SCAN_UNROLL = 4


def _lru_kernel(rxl_ref, rxc_ref, rgl_ref, rgc_ref, cw_ref, cb_ref, wa_ref, wx_ref, ba_ref, bx_ref, lam_ref,
                ol_ref, oc_ref, a_scr, b_scr, h_scr):
    cw = cw_ref[...]
    cb = cb_ref[...]
    row8 = lax.broadcasted_iota(I32, (SUBLANES, LANES), 0)

    def coeffs(x_ref, n):
        x = x_ref[0]
        xc = (cb + _shift_down(x, 2) * cw[0:1] + _shift_down(x, 1) * cw[1:2] + x * cw[2:3]
              + _shift_up(x, 1) * cw[3:4])
        xb = xc.astype(BF16)
        for d in range(2):
            r = _sigmoid(_dot(xb, wa_ref[d, 0]) + ba_ref[d])
            i = _sigmoid(_dot(xb, wx_ref[d, 0]) + bx_ref[d])
            nl = -lam_ref[d]
            softplus = jnp.maximum(nl, 0.0) + jnp.log(1.0 + jnp.exp(-jnp.abs(nl)))
            log_a = (-LRU_C) * r * softplus
            a = jnp.exp(log_a)
            a_scr[d, 0:n, :] = a
            b_scr[d, 0:n, :] = jnp.sqrt(1.0 - a * a) * i * xc

    def scan(n, carry_f, carry_b):
        ng = n // SUBLANES

        def body(g, carry):
            cf, cbk = carry
            of = pl.multiple_of(g * SUBLANES, SUBLANES)
            ob = pl.multiple_of((ng - 1 - g) * SUBLANES, SUBLANES)
            a = a_scr[0, pl.ds(of, SUBLANES), :]
            b = b_scr[0, pl.ds(of, SUBLANES), :]
            a2 = a_scr[1, pl.ds(ob, SUBLANES), :]
            b2 = b_scr[1, pl.ds(ob, SUBLANES), :]
            for k in (1, 2, 4):
                keep = row8 >= k
                b = a * jnp.where(keep, pltpu.roll(b, k, 0), 0.0) + b
                a = a * jnp.where(keep, pltpu.roll(a, k, 0), 1.0)
                keep2 = row8 < SUBLANES - k
                b2 = a2 * jnp.where(keep2, pltpu.roll(b2, SUBLANES - k, 0), 0.0) + b2
                a2 = a2 * jnp.where(keep2, pltpu.roll(a2, SUBLANES - k, 0), 1.0)
            hf = a * cf + b
            hb = a2 * cbk + b2
            h_scr[0, pl.ds(of, SUBLANES), :] = hf
            h_scr[1, pl.ds(ob, SUBLANES), :] = hb
            return hf[SUBLANES - 1:SUBLANES, :], hb[0:1, :]

        return lax.fori_loop(0, ng, body, (carry_f, carry_b), unroll=LRU_SCAN_UNROLL)

    n_ctx = rxc_ref.shape[1]
    n_lat = rxl_ref.shape[1]
    zero = jnp.zeros((1, LANES), F32)
    coeffs(rxc_ref, n_ctx)
    cf, cbk = scan(n_ctx, zero, zero)
    oc_ref[0] = ((h_scr[0, 0:n_ctx, :] + h_scr[1, 0:n_ctx, :]) * rgc_ref[0].astype(F32)).astype(BF16)
    coeffs(rxl_ref, n_lat)
    scan(n_lat, cf, cbk)
    ol_ref[0] = ((h_scr[0] + h_scr[1]) * rgl_ref[0].astype(F32)).astype(BF16)


def _lru_call(rx, rx_c, rg, rg_c, conv_w, conv_b, wa_bd, wx_bd, ba, bx, lam):
    bsz, t, _ = rx.shape
    tc = rx_c.shape[1]
    nch = LRU_WIDTH // LANES
    seq = lambda n: pl.BlockSpec((1, n, LANES), lambda b, j: (b, 0, j))
    vec = lambda r: pl.BlockSpec((r, 1, LANES), lambda b, j: (0, 0, j))
    wsp = pl.BlockSpec((2, 1, LANES, LANES), lambda b, j: (0, j, 0, 0))
    return pl.pallas_call(
        _lru_kernel,
        out_shape=[jax.ShapeDtypeStruct((bsz, t, LRU_WIDTH), BF16), jax.ShapeDtypeStruct((bsz, tc, LRU_WIDTH), BF16)],
        grid=(bsz, nch),
        in_specs=[seq(t), seq(tc), seq(t), seq(tc),
                  pl.BlockSpec((4, LANES), lambda b, j: (0, j)), pl.BlockSpec((1, LANES), lambda b, j: (0, j)),
                  wsp, wsp, vec(2), vec(2), vec(2)],
        out_specs=[seq(t), seq(tc)],
        scratch_shapes=[pltpu.VMEM((2, t, LANES), F32)] * 3,
        compiler_params=_params(2),
        name="lru",
    )(rx, rx_c, rg, rg_c, conv_w, conv_b, wa_bd, wx_bd, ba, bx, lam)


def _merge_kernel(attn_ref, hy_ref, lru_ref, gate_ref, h_ref, g1_ref, wa_ref, wh_ref, wl_ref, wo_ref,
                  n2_ref, sh2_ref, sc2_ref, rw_ref, hn_ref, u2_ref, lt_ref):
    def gate(j):
        return _sigmoid(gate_ref[0, :, j * D_MODEL:(j + 1) * D_MODEL].astype(F32))

    y = gate(0) * _dot(attn_ref[0], wa_ref[...])
    y = y + gate(1) * _dot(hy_ref[0], wh_ref[...])
    y = y + gate(2) * _dot(lru_ref[0], wl_ref[...])
    hn = h_ref[0] + g1_ref[0] * _dot(y.astype(BF16), wo_ref[...])
    hn_ref[0] = hn
    u2 = (_rms(hn) * n2_ref[...] * (1.0 + sc2_ref[0]) + sh2_ref[0]).astype(BF16)
    u2_ref[0] = u2
    lt_ref[0] = _dot_nt(rw_ref[...], u2)


def _merge_call(attn, hyo, lruo, gate, h, g1, wa, wh, wl, wo, n2g, sh2, sc2, rw_t, ctx_rows):
    bsz, t, d = h.shape
    tm = min(512, t)
    row = (lambda b: MOD_ROWS - 8) if ctx_rows else (lambda b: b)
    tok = lambda w: pl.BlockSpec((1, tm, w), lambda b, i: (b, i, 0))
    modspec = pl.BlockSpec((1, 1, d), lambda b, i: (row(b), 0, 0))
    const = lambda shape: pl.BlockSpec(shape, lambda b, i: (0,) * len(shape))
    return pl.pallas_call(
        _merge_kernel,
        out_shape=[jax.ShapeDtypeStruct((bsz, t, d), F32), jax.ShapeDtypeStruct((bsz, t, d), BF16),
                   jax.ShapeDtypeStruct((bsz, N_EXPERTS, t), F32)],
        grid=(bsz, t // tm),
        in_specs=[tok(Q_WIDTH), tok(HY_WIDTH), tok(LRU_WIDTH), tok(GATE_WIDTH), tok(d), modspec,
                  const((Q_WIDTH, d)), const((HY_WIDTH, d)), const((LRU_WIDTH, d)), const((d, d)),
                  const((1, d)), modspec, modspec, const((N_EXPERTS, d))],
        out_specs=[tok(d), tok(d), pl.BlockSpec((1, N_EXPERTS, tm), lambda b, i: (b, 0, i))],
        compiler_params=_params(2),
        name="merge",
    )(attn, hyo, lruo, gate, h, g1, wa, wh, wl, wo, n2g, sh2, sc2, rw_t)


def _router_kernel(lt_ref, slot_ref, slotc_ref, g_ref, coff_ref, *, cap):
    lg = lt_ref[0]
    n_e, t = lg.shape
    ex = jnp.exp(lg - lg.max(axis=0, keepdims=True))
    aff = ex / ex.sum(axis=0, keepdims=True)
    key = pltpu.bitcast(aff, I32)
    capf = float(cap)

    def count(mask):
        return jnp.where(mask, 1.0, 0.0).sum(axis=1, keepdims=True)

    def vbody(i, thr):
        sh = 28 - 2 * i
        out = thr
        for j in (1, 2, 3):
            cand = thr | lax.shift_left(jnp.int32(j), sh)
            out = jnp.where(count(key >= cand) >= capf, cand, out)
        return out

    thr = lax.fori_loop(0, 15, vbody, jnp.zeros((n_e, 1), I32))
    gt = key > thr
    eq = key == thr
    need = capf - count(gt)
    idx = lax.broadcasted_iota(I32, (n_e, t), 1)
    nbits = t.bit_length() - 1

    def ibody(i, lo):
        sh = nbits - 2 - 2 * i
        out = lo
        for j in (1, 2, 3):
            cand = lo | lax.shift_left(jnp.int32(j), sh)
            out = jnp.where(count(eq & (idx < cand)) < need, cand, out)
        return out

    last = lax.fori_loop(0, nbits // 2, ibody, jnp.zeros((n_e, 1), I32))
    if nbits % 2:
        last = jnp.where(count(eq & (idx < (last | 1))) < need, last | 1, last)
    sel = gt | (eq & (idx <= last))
    self32 = jnp.where(sel, 1.0, 0.0)
    g_ref[0] = jnp.where(sel, aff, 0.0)

    r_i = lax.broadcasted_iota(I32, (LANES, LANES), 0)
    c_i = lax.broadcasted_iota(I32, (LANES, LANES), 1)
    tri = jnp.where(r_i <= c_i, 1.0, 0.0).astype(BF16)
    eye = jnp.where(r_i == c_i, 1.0, 0.0).astype(BF16)
    off = jnp.zeros((n_e, 1), F32)
    n_units = t // LANES
    coff_ref[0] = jnp.zeros((n_e, LANES), I32)
    for c in range(n_units):
        sl = slice(c * LANES, (c + 1) * LANES)
        coff_ref[0, :, c:c + 1] = off.astype(I32)
        xs = self32[:, sl]
        inc = _dot(xs.astype(BF16), tri)
        slot1 = jnp.where(sel[:, sl], inc - xs + off + 1.0, 0.0)
        off = off + inc[:, LANES - 1:LANES]
        slot_ref[0, :, sl] = slot1.astype(I32) - 1
        hi = jnp.floor(slot1 * (1.0 / 16.0))
        lo = slot1 - 16.0 * hi
        col = 16.0 * _dot_nt(eye, hi.astype(BF16)) + _dot_nt(eye, lo.astype(BF16))
        slotc_ref[0, sl, :] = col.astype(I32) - 1
    coff_ref[0, :, n_units:n_units + 1] = off.astype(I32)


def _router_call(logits_t, cap):
    bsz, n_e, t = logits_t.shape
    assert t % LANES == 0 and t // LANES < LANES
    row = pl.BlockSpec((1, n_e, t), lambda b: (b, 0, 0))
    slot_row, slot_col, g_row, coff = pl.pallas_call(
        functools.partial(_router_kernel, cap=cap),
        out_shape=[jax.ShapeDtypeStruct((bsz, n_e, t), I32), jax.ShapeDtypeStruct((bsz, t, n_e), I32),
                   jax.ShapeDtypeStruct((bsz, n_e, t), F32), jax.ShapeDtypeStruct((bsz, n_e, LANES), I32)],
        grid=(bsz,),
        in_specs=[row],
        out_specs=[row, pl.BlockSpec((1, t, n_e), lambda b: (b, 0, 0)), row,
                   pl.BlockSpec((1, n_e, LANES), lambda b: (b, 0, 0))],
        compiler_params=_params(1),
        name="router",
    )(logits_t)
    return slot_row, slot_col, g_row, coff[:, :, :t // LANES + 1].reshape(-1)


MOE_GATHER_TOKENS = 256
MOE_SLOT_ROWS = 128
MOE_SCATTER_WINDOW = 128
MOE_SCATTER_SLOTS = 256


def _log2(n):
    assert n & (n - 1) == 0
    return n.bit_length() - 1


def _expert_kernel(coff_ref, u_ref, slot_ref, g_ref, w1_ref, w3_ref, w2_ref, y_ref, xg_scr, gs_scr, *, tkg, sb,
                   n_units):
    cap = xg_scr.shape[0]
    n_chunks = u_ref.shape[1] // tkg
    base = (pl.program_id(0) * pl.num_programs(1) + pl.program_id(1)) * (n_units + 1)
    upc = tkg // LANES
    rid = lax.broadcasted_iota(I32, (sb, tkg), 0)
    xg_scr[...] = jnp.zeros(xg_scr.shape, F32)
    gs_scr[...] = jnp.zeros(gs_scr.shape, F32)

    def gather(match, r0, c_tokens, g_row):
        xg_scr[pl.ds(r0, sb), :] += _dot(jnp.where(match, 1.0, 0.0).astype(BF16), u_ref[0, c_tokens, :])
        gs_scr[pl.ds(r0, sb), :] += jnp.where(match, g_row, 0.0).sum(axis=1, keepdims=True)

    def window_start(c):
        lo = coff_ref[base + c * upc]
        return jnp.minimum(lo & (-SUBLANES), cap - sb)

    for c in range(n_chunks):
        r0 = pl.multiple_of(window_start(c), SUBLANES)
        gather(slot_ref[0, 0, c:c + 1, :] - r0 == rid, r0, slice(c * tkg, (c + 1) * tkg), g_ref[0, 0, c:c + 1, :])

    def overflow(c, carry):
        covered = window_start(c) + sb
        hi = coff_ref[base + (c + 1) * upc]
        first = lax.shift_right_logical(covered, _log2(sb))
        stop = jnp.where(hi > covered, lax.shift_right_logical(hi + (sb - 1), _log2(sb)), first)

        def block(j, carry2):
            b0 = pl.multiple_of(j * sb, sb)
            srow = slot_ref[0, 0, pl.ds(c, 1), :]
            gather((srow - b0 == rid) & (srow >= covered), b0, pl.ds(pl.multiple_of(c * tkg, tkg), tkg),
                   g_ref[0, 0, pl.ds(c, 1), :])
            return carry2

        return lax.fori_loop(first, stop, block, carry)

    lax.fori_loop(0, n_chunks, overflow, 0)
    xb = xg_scr[...].astype(BF16)
    hid = (_silu(_dot(xb, w1_ref[0, 0])) * _dot(xb, w3_ref[0, 0])).astype(BF16)
    y_ref[0, 0] = (_dot(hid, w2_ref[0, 0]) * gs_scr[...]).astype(BF16)


def _expert_call(u2, slot_row, g_row, coff, w1, w3, w2, layer, cap):
    bsz, t, d = u2.shape
    n_e = w1.shape[1]
    tkg = min(MOE_GATHER_TOKENS, t)
    sb = min(MOE_SLOT_ROWS, cap)
    wspec = pl.BlockSpec((1, 1, d, d), lambda b, e, co: (layer, e, 0, 0))
    chunks = pl.BlockSpec((1, 1, t // tkg, tkg), lambda b, e, co: (b, e, 0, 0))
    grid_spec = pltpu.PrefetchScalarGridSpec(
        num_scalar_prefetch=1,
        grid=(bsz, n_e),
        in_specs=[pl.BlockSpec((1, t, d), lambda b, e, co: (b, 0, 0)),
                  chunks, chunks, wspec, wspec, wspec],
        out_specs=pl.BlockSpec((1, 1, cap, d), lambda b, e, co: (b, e, 0, 0)),
        scratch_shapes=[pltpu.VMEM((cap, d), F32), pltpu.VMEM((cap, 1), F32)],
    )
    return pl.pallas_call(
        functools.partial(_expert_kernel, tkg=tkg, sb=sb, n_units=t // LANES),
        out_shape=jax.ShapeDtypeStruct((bsz, n_e, cap, d), BF16),
        grid_spec=grid_spec,
        compiler_params=_params(2),
        name="expert",
    )(coff, u2, slot_row.reshape(bsz, n_e, t // tkg, tkg), g_row.reshape(bsz, n_e, t // tkg, tkg), w1, w3, w2)


def _expert_call_folded(u2, slot_row, g_row, coff, w1, w3, w2, layer, cap):
    bsz, t, d = u2.shape
    n_e = slot_row.shape[1]
    n_units = t // LANES
    first = (jnp.arange(bsz, dtype=I32) * cap)[:, None, None]
    slot_all = jnp.where(slot_row >= 0, slot_row + first, -1).transpose(1, 0, 2).reshape(1, n_e, bsz * t)
    coff_all = (coff.reshape(bsz, n_e, n_units + 1)[:, :, :n_units] + first).transpose(1, 0, 2)
    coff_all = jnp.concatenate([coff_all.reshape(n_e, bsz * n_units), jnp.full((n_e, 1), bsz * cap, I32)], axis=1)
    g_all = g_row.transpose(1, 0, 2).reshape(1, n_e, bsz * t)
    y = _expert_call(u2.reshape(1, bsz * t, d), slot_all, g_all, coff_all.reshape(-1), w1, w3, w2, layer, bsz * cap)
    return y.reshape(n_e, bsz, cap, d).transpose(1, 0, 2, 3)


def _scatter_kernel(coff_ref, h_ref, y_ref, slotc_ref, g2_ref, fg_ref, o_ref, acc_scr, ystack, *, win, group, kb,
                    n_units, final_norm):
    tk = h_ref.shape[1]
    cap = y_ref.shape[2]
    upt = tk // LANES
    i = pl.program_id(1)
    lane = lax.broadcasted_iota(I32, (tk, win), 1)
    bf16_rows = 2 * SUBLANES

    def span(e):
        base = (pl.program_id(0) * N_EXPERTS + e) * (n_units + 1)
        lo = coff_ref[base + i * upt]
        hi = coff_ref[base + (i + 1) * upt]
        return jnp.minimum(lo & (-bf16_rows), cap - win), hi

    acc = jnp.zeros((tk, D_MODEL), F32)
    for p in range(N_EXPERTS // group):
        hots = []
        for j in range(group):
            e = p * group + j
            r0 = pl.multiple_of(span(e)[0], bf16_rows)
            ystack[p, j * win:(j + 1) * win, :] = y_ref[0, e, pl.ds(r0, win), :]
            hots.append(jnp.where(slotc_ref[0, :, e:e + 1] - r0 == lane, 1.0, 0.0).astype(BF16))
        onehot = hots[0] if group == 1 else jnp.concatenate(hots, axis=1)
        acc = acc + _dot(onehot, ystack[p])
    acc_scr[...] = acc

    lane_kb = lax.broadcasted_iota(I32, (tk, kb), 1)
    for e in range(N_EXPERTS):
        r0, hi = span(e)
        covered = r0 + win
        first = lax.shift_right_logical(covered, _log2(kb))
        stop = jnp.where(hi > covered, lax.shift_right_logical(hi + (kb - 1), _log2(kb)), first)

        def block(j, carry, e=e, covered=covered):
            b0 = pl.multiple_of(j * kb, kb)
            scol = slotc_ref[0, :, e:e + 1]
            onehot = jnp.where((scol - b0 == lane_kb) & (scol >= covered), 1.0, 0.0).astype(BF16)
            acc_scr[...] += _dot(onehot, y_ref[0, e, pl.ds(b0, kb), :])
            return carry

        lax.fori_loop(first, stop, block, 0)
    out = h_ref[0] + g2_ref[0] * acc_scr[...]
    o_ref[0] = _rms(out) * fg_ref[...] if final_norm else out


def _scatter_call(h, y, slot_col, coff, g2, cap, ctx_rows, final_g=None):
    bsz, t, d = h.shape
    n_e = y.shape[1]
    tk = min(512, t)
    win = min(MOE_SCATTER_WINDOW, cap)
    group = 2 if win == LANES else 1
    kb = min(MOE_SCATTER_SLOTS, cap)
    row = (lambda b: MOD_ROWS - 8) if ctx_rows else (lambda b: b)
    tile = lambda w: pl.BlockSpec((1, tk, w), lambda b, i, co: (b, i, 0))
    grid_spec = pltpu.PrefetchScalarGridSpec(
        num_scalar_prefetch=1,
        grid=(bsz, t // tk),
        in_specs=[tile(d),
                  pl.BlockSpec((1, n_e, cap, d), lambda b, i, co: (b, 0, 0, 0)),
                  tile(n_e), pl.BlockSpec((1, 1, d), lambda b, i, co: (row(b), 0, 0)),
                  pl.BlockSpec((1, d), lambda b, i, co: (0, 0))],
        out_specs=tile(d),
        scratch_shapes=[pltpu.VMEM((tk, d), F32), pltpu.VMEM((n_e // group, group * win, d), BF16)],
    )
    return pl.pallas_call(
        functools.partial(_scatter_kernel, win=win, group=group, kb=kb, n_units=t // LANES,
                          final_norm=final_g is not None),
        out_shape=jax.ShapeDtypeStruct((bsz, t, d), F32),
        grid_spec=grid_spec,
        compiler_params=_params(2),
        name="moe_scatter",
    )(coff, h, y, slot_col, g2, jnp.ones((1, d), F32) if final_g is None else final_g)


def _rope_tables(n_lat, n_ctx):
    rows = n_lat // GRID_W
    row = jnp.repeat(jnp.arange(rows, dtype=F32), GRID_W)
    col = jnp.tile(jnp.arange(GRID_W, dtype=F32), rows)
    inv = jnp.power(ROPE_THETA, -jnp.arange(ROPE_PAIRS_AXIS, dtype=F32) / ROPE_PAIRS_AXIS)
    ang = jnp.concatenate([row[:, None] * inv, col[:, None] * inv], axis=-1)
    cos, sin = jnp.cos(ang), jnp.sin(ang)
    cs = jnp.concatenate([cos, cos], axis=-1)
    sn = jnp.concatenate([-sin, sin], axis=-1)
    return cs, sn, jnp.ones((n_ctx, HEAD_DIM), F32), jnp.zeros((n_ctx, HEAD_DIM), F32)


def _dft_tables(n):
    k = jnp.arange(n // 2, dtype=I32)

    def tables(first):
        m = ((2 * k[:, None] + 1) * (2 * k[None, :] + first)) % (4 * n)
        ang = m.astype(F32) * (2.0 * math.pi / (4 * n))
        return jnp.cos(ang), jnp.sin(ang)

    ce, se = tables(0)
    co, so = tables(1)
    fwd = tuple(x.astype(BF16) for x in (ce, co, se, so))
    inv = tuple(x.astype(BF16) for x in (ce.T, co, -se.T, -so))
    return fwd, inv


def _filter_features(n):
    t = jnp.linspace(0.0, 1.0, n, dtype=F32)[:, None]
    w = (2.0 * math.pi / n) * jnp.arange(n, dtype=F32)[:, None]
    f = jnp.linspace(1e-4, HY_BANDS - 1, HY_BANDS, dtype=F32)[None, :]
    feats = jnp.concatenate([t, jnp.cos(f * w), -jnp.sin(f * w)], axis=-1)
    feats = jnp.concatenate([feats[0::2], feats[1::2]], axis=0)
    return jnp.pad(feats, ((0, 0), (0, LANES - HY_EMB)))


def _pad_to(x, shape):
    return jnp.pad(x, [(0, s - d) for d, s in zip(x.shape, shape)])


def _block_diag_chunks(w):
    per = LANES // LRU_BLOCK
    w = w.reshape(2, LRU_BLOCKS // per, per, LRU_BLOCK, LRU_BLOCK)
    eye = jnp.eye(per, dtype=w.dtype)
    return jnp.einsum('dcpkj,pq->dcpkqj', w, eye).reshape(2, LRU_BLOCKS // per, LANES, LANES)


def kernel(x, c, ctx, c_ctx, mod_w, mod_b, norm1_g, norm2_g, w_in, q_norm_g, k_norm_g, hy_conv_w, hy_conv_b,
           hy_fw1, hy_fb1, hy_fw2, hy_fb2, hy_fw3, hy_freq, hy_bias, lru_conv_w, lru_conv_b, lru_wa, lru_ba,
           lru_wx, lru_bx, lru_lambda, w_attn_out, w_hy_out, w_lru_out, w_out, router_w, exp_w1, exp_w3, exp_w2,
           final_norm_g):
    bsz, n_lat, d = x.shape
    n_ctx = ctx.shape[1]
    depth = mod_w.shape[0]
    assert d == D_MODEL and bsz <= MOD_ROWS - 8 and n_lat % GRID_W == 0

    cs_l, sn_l, cs_c, sn_c = _rope_tables(n_lat, n_ctx)
    dft_l = _dft_tables(n_lat)
    dft_c = _dft_tables(n_ctx)
    feats_l = _filter_features(n_lat)
    feats_c = _filter_features(n_ctx)
    max_decay = math.log(HY_DECAY_TARGET) / HY_FAST_DECAY_PCT
    min_decay = math.log(HY_DECAY_TARGET) / HY_SLOW_DECAY_PCT
    deltas_abs = jnp.abs(jnp.linspace(min_decay, max_decay, HY_WIDTH, dtype=F32))[None, :]
    cap_l = max(1, EC_CAPACITY * n_lat // N_EXPERTS)
    cap_c = max(1, EC_CAPACITY * n_ctx // N_EXPERTS)

    cvec = jnp.zeros((MOD_ROWS, d), F32).at[:bsz].set(c).at[MOD_ROWS - 8].set(c_ctx)
    mod = _mod_call(cvec, mod_w, mod_b)
    mod = mod.reshape(depth, MOD_ROWS, 6, 1, d).transpose(0, 2, 1, 3, 4)

    w_in_b = w_in.astype(BF16)
    e1 = exp_w1.astype(BF16)
    e3 = exp_w3.astype(BF16)
    e2 = exp_w2.astype(BF16)

    h, hc = x, ctx
    for l in range(depth):
        last = l == depth - 1
        sh1, sc1, g1, sh2, sc2, g2 = (mod[l, j] for j in range(6))
        n1 = norm1_g[l][None, :]
        n2 = norm2_g[l][None, :]
        qg = q_norm_g[l][None, :]
        kg = k_norm_g[l][None, :]

        q, k, v, hy, rx, rg, gate = _inproj_call(h, sh1, sc1, n1, w_in_b, l, cs_l, sn_l, qg, kg, False)
        qc, kc, vc, hyc, rxc, rgc, gatec = _inproj_call(hc, sh1, sc1, n1, w_in_b, l, cs_c, sn_c, qg, kg, True)

        attn = _attn_call(q, [(k, v), (kc, vc)])

        fw1 = _pad_to(hy_fw1[l], (LANES, LANES))
        fb1 = _pad_to(hy_fb1[l][None, :], (1, LANES))
        fw2 = _pad_to(hy_fw2[l], (LANES, LANES))
        fb2 = _pad_to(hy_fb2[l][None, :], (1, LANES))
        fw3 = _pad_to(hy_fw3[l], (LANES, 2 * HY_WIDTH))
        freq = _pad_to(hy_freq[l], (2, LANES))
        hbias = hy_bias[l][None, :]

        def hyena(hy_in, feats, tabs):
            tabs_fwd, tabs_inv = tabs
            hsd = _hy_filter_call(feats, fw1, fb1, fw2, fb2, fw3, freq, deltas_abs)
            kspec = _hy_kspec_call(tabs_fwd, hsd)
            x0, we, wo = _hy_pre_call(hy_in, hy_conv_w[l], hy_conv_b[l][None, :])
            us = _hy_fwd_call(tabs_fwd, we, wo, kspec)
            return _hy_inv_call(tabs_inv, us, x0, we, wo, hbias)

        hyo = hyena(hy, feats_l, dft_l)

        lruo, lruoc = _lru_call(rx, rxc, rg, rgc, lru_conv_w[l], lru_conv_b[l][None, :],
                                _block_diag_chunks(lru_wa[l]).astype(BF16), _block_diag_chunks(lru_wx[l]).astype(BF16),
                                lru_ba[l][:, None, :], lru_bx[l][:, None, :], lru_lambda[l][:, None, :])

        wa = w_attn_out[l].astype(BF16)
        wh = w_hy_out[l].astype(BF16)
        wl = w_lru_out[l].astype(BF16)
        wo = w_out[l].astype(BF16)
        rw_t = router_w[l].T.astype(BF16)

        def channel_mix(h_in, attn_in, hy_in, lru_in, gate_in, cap, ctx_rows, final_g=None):
            hn, u2, lt = _merge_call(attn_in, hy_in, lru_in, gate_in, h_in, g1, wa, wh, wl, wo, n2, sh2, sc2, rw_t,
                                     ctx_rows)
            slot_row, slot_col, g_row, coff = _router_call(lt, cap)
            if ctx_rows:
                y = _expert_call_folded(u2, slot_row, g_row, coff, e1, e3, e2, l, cap)
            else:
                y = _expert_call(u2, slot_row, g_row, coff, e1, e3, e2, l, cap)
            return _scatter_call(hn, y, slot_col, coff, g2, cap, ctx_rows, final_g)

        h = channel_mix(h, attn, hyo, lruo, gate, cap_l, False, final_norm_g[None, :] if last else None)
        if not last:
            attn_c = _attn_call(qc, [(kc, vc)])
            hyo_c = hyena(hyc, feats_c, dft_c)
            hc = channel_mix(hc, attn_c, hyo_c, lruoc, gatec, cap_c, True)

    return h
```

```python
import functools
import math

import jax
import jax.numpy as jnp
from jax import lax
from jax.experimental import pallas as pl
from jax.experimental.pallas import tpu as pltpu

F32 = jnp.float32
BF16 = jnp.bfloat16
I32 = jnp.int32

D_MODEL = 1024
GRID_W = 64
NORM_EPS = 1e-6
N_HEADS = 8
N_KV_HEADS = 2
HEAD_DIM = 128
GROUP = N_HEADS // N_KV_HEADS
ROPE_PAIRS_AXIS = HEAD_DIM // 4
ROPE_THETA = 10000.0
ATTN_SCALE = HEAD_DIM ** -0.5
Q_PRESCALE = ATTN_SCALE * math.log2(math.e)
Q_WIDTH = N_HEADS * HEAD_DIM
KV_WIDTH = N_KV_HEADS * HEAD_DIM
HY_WIDTH = D_MODEL // 2
HY_BANDS = 16
HY_EMB = 1 + 2 * HY_BANDS
HY_FILTER_HIDDEN = 64
HY_FAST_DECAY_PCT = 0.3
HY_SLOW_DECAY_PCT = 1.5
HY_DECAY_TARGET = 1e-2
LRU_WIDTH = D_MODEL // 2
LRU_BLOCKS = 8
LRU_BLOCK = LRU_WIDTH // LRU_BLOCKS
LRU_C = 8.0
N_BRANCH = 3
N_EXPERTS = 16
EC_CAPACITY = 2
GATE_WIDTH = N_BRANCH * D_MODEL
OFF_Q = 0
OFF_K = OFF_Q + Q_WIDTH
OFF_V = OFF_K + KV_WIDTH
OFF_HY = OFF_V + KV_WIDTH
OFF_RX = OFF_HY + 3 * HY_WIDTH
OFF_RG = OFF_RX + LRU_WIDTH
OFF_GATE = OFF_RG + LRU_WIDTH
IN_TOTAL = OFF_GATE + GATE_WIDTH

LANES = 128
SUBLANES = 8
V7X_VMEM_LIMIT_BYTES = 56 * 1024 * 1024
MOD_ROWS = 24


def _params(n_axes, vmem=V7X_VMEM_LIMIT_BYTES):
    return pltpu.CompilerParams(dimension_semantics=("arbitrary",) * n_axes, vmem_limit_bytes=vmem)


def _dot(a, b):
    return jnp.dot(a, b, preferred_element_type=F32)


def _dot_nt(a, b):
    return lax.dot_general(a, b, (((1,), (1,)), ((), ())), preferred_element_type=F32)


def _split2(x):
    hi = x.astype(BF16)
    lo = (x - hi.astype(F32)).astype(BF16)
    return hi, lo


def _dot3(a, b):
    ah, al = _split2(a)
    bh, bl = _split2(b)
    return _dot(ah, bh) + (_dot(ah, bl) + _dot(al, bh))


def _rms(x):
    return x * lax.rsqrt(jnp.mean(x * x, axis=-1, keepdims=True) + NORM_EPS)


def _sigmoid(x):
    return 0.5 * jnp.tanh(0.5 * x) + 0.5


def _silu(x):
    return x * _sigmoid(x)


def _gelu_tanh(x):
    return 0.5 * x * (1.0 + jnp.tanh(math.sqrt(2.0 / math.pi) * (x + 0.044715 * (x * x * x))))


def _shift_down(x, k):
    row = lax.broadcasted_iota(I32, x.shape, 0)
    return jnp.where(row >= k, pltpu.roll(x, k, 0), 0.0)


def _shift_up(x, k):
    n = x.shape[0]
    row = lax.broadcasted_iota(I32, x.shape, 0)
    return jnp.where(row < n - k, pltpu.roll(x, n - k, 0), 0.0)


def _mod_kernel(c_ref, w_ref, b_ref, o_ref):
    o_ref[0] = _dot3(_silu(c_ref[...]), w_ref[0]) + b_ref[0]


def _mod_call(cvec, mod_w, mod_b):
    depth, d, six_d = mod_w.shape
    tn = 1536
    return pl.pallas_call(
        _mod_kernel,
        out_shape=jax.ShapeDtypeStruct((depth, MOD_ROWS, six_d), F32),
        grid=(depth, six_d // tn),
        in_specs=[
            pl.BlockSpec((MOD_ROWS, d), lambda l, j: (0, 0)),
            pl.BlockSpec((1, d, tn), lambda l, j: (l, 0, j)),
            pl.BlockSpec((1, 1, tn), lambda l, j: (l, 0, j)),
        ],
        out_specs=pl.BlockSpec((1, MOD_ROWS, tn), lambda l, j: (l, 0, j)),
        compiler_params=_params(2),
        name="mod",
    )(cvec, mod_w, mod_b.reshape(depth, 1, six_d))


def _inproj_kernel(h_ref, sh_ref, sc_ref, g_ref, w_ref, cs_ref, sn_ref, qg_ref, kg_ref,
                   q_ref, k_ref, v_ref, hy_ref, rx_ref, rg_ref, gate_ref):
    u = (_rms(h_ref[0]) * g_ref[...] * (1.0 + sc_ref[0]) + sh_ref[0]).astype(BF16)
    cs = cs_ref[...]
    sn = sn_ref[...]

    def normed_rope(z, g):
        r = _rms(z) * g
        return r * cs + pltpu.roll(r, HEAD_DIM // 2, 1) * sn

    zq = _dot(u, w_ref[0, :, OFF_Q:OFF_Q + Q_WIDTH])
    for hd in range(N_HEADS):
        sl = slice(hd * HEAD_DIM, (hd + 1) * HEAD_DIM)
        q_ref[0, hd] = (normed_rope(zq[:, sl], qg_ref[...]) * Q_PRESCALE).astype(BF16)
    zkv = _dot(u, w_ref[0, :, OFF_K:OFF_K + 2 * KV_WIDTH])
    for hd in range(N_KV_HEADS):
        sl = slice(hd * HEAD_DIM, (hd + 1) * HEAD_DIM)
        k_ref[0, :, sl] = normed_rope(zkv[:, sl], kg_ref[...]).astype(BF16)
    v_ref[0] = zkv[:, KV_WIDTH:].astype(BF16)
    for j in range(3):
        sl = slice(j * HY_WIDTH, (j + 1) * HY_WIDTH)
        hy_ref[0, :, sl] = _dot(u, w_ref[0, :, OFF_HY + j * HY_WIDTH:OFF_HY + (j + 1) * HY_WIDTH]).astype(BF16)
    rr = _dot(u, w_ref[0, :, OFF_RX:OFF_RX + 2 * LRU_WIDTH])
    rx_ref[0] = rr[:, :LRU_WIDTH]
    rg_ref[0] = _gelu_tanh(rr[:, LRU_WIDTH:]).astype(BF16)
    for j in range(N_BRANCH):
        sl = slice(j * D_MODEL, (j + 1) * D_MODEL)
        gate_ref[0, :, sl] = _dot(u, w_ref[0, :, OFF_GATE + j * D_MODEL:OFF_GATE + (j + 1) * D_MODEL]).astype(BF16)


def _inproj_call(h, sh, sc, g, w_bf, layer, cs, sn, qg, kg, ctx_rows):
    bsz, t, d = h.shape
    tm = min(512, t)
    row = (lambda b: MOD_ROWS - 8) if ctx_rows else (lambda b: b)
    tok = lambda w: pl.BlockSpec((1, tm, w), lambda b, i: (b, i, 0))
    modspec = pl.BlockSpec((1, 1, d), lambda b, i: (row(b), 0, 0))
    const = lambda shape: pl.BlockSpec(shape, lambda b, i: (0,) * len(shape))
    outs = [(KV_WIDTH, BF16), (KV_WIDTH, BF16), (3 * HY_WIDTH, BF16), (LRU_WIDTH, F32), (LRU_WIDTH, BF16),
            (GATE_WIDTH, BF16)]
    return pl.pallas_call(
        _inproj_kernel,
        out_shape=[jax.ShapeDtypeStruct((bsz, N_HEADS, t, HEAD_DIM), BF16)]
        + [jax.ShapeDtypeStruct((bsz, t, w), dt) for w, dt in outs],
        grid=(bsz, t // tm),
        in_specs=[
            tok(d), modspec, modspec, const((1, d)),
            pl.BlockSpec((1, d, IN_TOTAL), lambda b, i: (layer, 0, 0), pipeline_mode=pl.Buffered(1)),
            pl.BlockSpec((tm, HEAD_DIM), lambda b, i: (i, 0)),
            pl.BlockSpec((tm, HEAD_DIM), lambda b, i: (i, 0)),
            const((1, HEAD_DIM)), const((1, HEAD_DIM)),
        ],
        out_specs=[pl.BlockSpec((1, N_HEADS, tm, HEAD_DIM), lambda b, i: (b, 0, i, 0))] + [tok(w) for w, _ in outs],
        compiler_params=_params(2),
        name="inproj",
    )(h, sh, sc, g, w_bf, cs, sn, qg, kg)


ATTN_TQ = 1024
ATTN_TK = 512
ATTN_STREAMS = 4
NEG_BIG = -1e30


def _attn_kernel(*refs, n_src, tq):
    q_ref = refs[0]
    kv = refs[1:1 + 2 * n_src]
    o_ref, kcat, vext = refs[1 + 2 * n_src:]
    n_keys = kcat.shape[0]

    @pl.when(pl.program_id(2) == 0)
    def _():
        off = 0
        for i in range(n_src):
            n = kv[2 * i].shape[1]
            kcat[off:off + n, :] = kv[2 * i][0]
            vext[off:off + n, 0:HEAD_DIM] = kv[2 * i + 1][0]
            off += n
        vext[:, HEAD_DIM:] = jnp.ones((n_keys, HEAD_DIM), BF16)

    rows = GROUP * tq
    q = q_ref[0].reshape(rows, HEAD_DIM)
    per = rows // ATTN_STREAMS
    qs = [q[i * per:(i + 1) * per] for i in range(ATTN_STREAMS)]
    ms = [jnp.full((per, 1), NEG_BIG, F32) for _ in qs]
    accs = [jnp.zeros((per, 2 * HEAD_DIM), F32) for _ in qs]
    for off in range(0, n_keys, ATTN_TK):
        size = min(ATTN_TK, n_keys - off)
        kk = kcat[off:off + size, :]
        vv = vext[off:off + size, :]
        for i in range(ATTN_STREAMS):
            s = _dot_nt(qs[i], kk)
            m_new = jnp.maximum(ms[i], s.max(axis=-1, keepdims=True))
            p = jnp.exp2(s - m_new).astype(BF16)
            accs[i] = jnp.exp2(ms[i] - m_new) * accs[i] + _dot(p, vv)
            ms[i] = m_new
    out = jnp.concatenate([a[:, :HEAD_DIM] / a[:, HEAD_DIM:] for a in accs], axis=0).astype(BF16)
    for g in range(GROUP):
        o_ref[0, :, g * HEAD_DIM:(g + 1) * HEAD_DIM] = out[g * tq:(g + 1) * tq]


def _attn_call(q, kvs):
    bsz, _, tq_all, _ = q.shape
    tq = min(ATTN_TQ, tq_all)
    gw = GROUP * HEAD_DIM
    in_specs = [pl.BlockSpec((1, GROUP, tq, HEAD_DIM), lambda b, h, i: (b, h, i, 0))]
    args = [q]
    n_keys = 0
    for k, v in kvs:
        tk = k.shape[1]
        n_keys += tk
        spec = pl.BlockSpec((1, tk, HEAD_DIM), lambda b, h, i: (b, 0, h))
        in_specs += [spec, spec]
        args += [k, v]
    return pl.pallas_call(
        functools.partial(_attn_kernel, n_src=len(kvs), tq=tq),
        out_shape=jax.ShapeDtypeStruct((bsz, tq_all, Q_WIDTH), BF16),
        grid=(bsz, N_KV_HEADS, tq_all // tq),
        in_specs=in_specs,
        out_specs=pl.BlockSpec((1, tq, gw), lambda b, h, i: (b, i, h)),
        scratch_shapes=[pltpu.VMEM((n_keys, HEAD_DIM), BF16), pltpu.VMEM((n_keys, 2 * HEAD_DIM), BF16)],
        compiler_params=_params(3),
        name="attn",
    )(*args)


def _hy_pre_kernel(h0_ref, h1_ref, h2_ref, w0_ref, w1_ref, w2_ref, b0_ref, b1_ref, b2_ref,
                   x0_ref, we_ref, wo_ref, w_scr):
    def conv(x_ref, cw_ref, cb_ref):
        x = x_ref[0].astype(F32)
        cw = cw_ref[...]
        return cb_ref[...] + _shift_down(x, 1) * cw[0:1] + x * cw[1:2] + _shift_up(x, 1) * cw[2:3]

    x0_ref[0] = conv(h0_ref, w0_ref, b0_ref).astype(BF16)
    w_scr[...] = conv(h2_ref, w2_ref, b2_ref) * conv(h1_ref, w1_ref, b1_ref)
    half = w_scr.shape[0] // 2
    we_ref[0] = w_scr[pl.ds(0, half, stride=2), :].astype(BF16)
    wo_ref[0] = w_scr[pl.ds(1, half, stride=2), :].astype(BF16)


def _hy_pre_call(hy, conv_w, conv_b):
    bsz, t, _ = hy.shape
    nch = HY_WIDTH // LANES
    xs = lambda part: pl.BlockSpec((1, t, LANES), lambda b, j: (b, 0, part * nch + j))
    ws = lambda part: pl.BlockSpec((3, LANES), lambda b, j: (0, part * nch + j))
    bs = lambda part: pl.BlockSpec((1, LANES), lambda b, j: (0, part * nch + j))
    out = pl.BlockSpec((1, t, LANES), lambda b, j: (b, 0, j))
    outh = pl.BlockSpec((1, t // 2, LANES), lambda b, j: (b, 0, j))
    return pl.pallas_call(
        _hy_pre_kernel,
        out_shape=[jax.ShapeDtypeStruct((bsz, t, HY_WIDTH), BF16),
                   jax.ShapeDtypeStruct((bsz, t // 2, HY_WIDTH), BF16),
                   jax.ShapeDtypeStruct((bsz, t // 2, HY_WIDTH), BF16)],
        grid=(bsz, nch),
        in_specs=[xs(0), xs(1), xs(2), ws(0), ws(1), ws(2), bs(0), bs(1), bs(2)],
        out_specs=[out, outh, outh],
        scratch_shapes=[pltpu.VMEM((t, LANES), F32)],
        compiler_params=_params(2),
        name="hy_pre",
    )(hy, hy, hy, conv_w, conv_w, conv_w, conv_b, conv_b, conv_b)


def _hy_filter_kernel(feat_ref, w1_ref, b1_ref, w2_ref, b2_ref, w3_ref, fr_ref, dl_ref, o_ref):
    feats = feat_ref[...]
    hid = jnp.sin(fr_ref[0:1] * (_dot3(feats, w1_ref[...]) + b1_ref[...]))
    hid = jnp.sin(fr_ref[1:2] * (_dot3(hid, w2_ref[...]) + b2_ref[...]))
    filt = _dot3(hid, w3_ref[...])
    decay = jnp.exp(-feats[:, 0:1] * dl_ref[...])
    h_fwd = filt[:, :HY_WIDTH] * decay
    h_bwd = filt[:, HY_WIDTH:] * decay
    o_ref[:, :HY_WIDTH] = h_fwd + h_bwd
    o_ref[:, HY_WIDTH:] = h_bwd - h_fwd


def _hy_filter_call(feats, fw1, fb1, fw2, fb2, fw3, freq, deltas_abs):
    n = feats.shape[0]
    tn = min(512, n)
    const = lambda shape: pl.BlockSpec(shape, lambda i: (0,) * len(shape))
    return pl.pallas_call(
        _hy_filter_kernel,
        out_shape=jax.ShapeDtypeStruct((n, 2 * HY_WIDTH), F32),
        grid=(n // tn,),
        in_specs=[pl.BlockSpec((tn, LANES), lambda i: (i, 0)), const((LANES, LANES)), const((1, LANES)),
                  const((LANES, LANES)), const((1, LANES)), const((LANES, 2 * HY_WIDTH)), const((2, LANES)),
                  const((1, HY_WIDTH))],
        out_specs=pl.BlockSpec((tn, 2 * HY_WIDTH), lambda i: (i, 0)),
        compiler_params=_params(1),
        name="hy_filter",
    )(feats, fw1, fb1, fw2, fb2, fw3, freq, deltas_abs)


def _hy_kspec_kernel(ce_ref, co_ref, se_ref, so_ref, he_ref, ho_ref, krl_ref, kil_ref, krh_ref, kih_ref, *, scale):
    def branch(tab_e, tab_o, cols):
        he_hi, he_lo = _split2(he_ref[:, cols])
        ho_hi, ho_lo = _split2(ho_ref[:, cols])
        pe = _dot(tab_e[...], he_hi) + _dot(tab_e[...], he_lo)
        po = _dot(tab_o[...], ho_hi) + _dot(tab_o[...], ho_lo)
        return pe, po

    pc, qc = branch(ce_ref, co_ref, slice(0, HY_WIDTH))
    ps, qs = branch(se_ref, so_ref, slice(HY_WIDTH, 2 * HY_WIDTH))
    krl_ref[...] = (pc + qc) * scale
    krh_ref[...] = (pc - qc) * scale
    kil_ref[...] = (ps + qs) * scale
    kih_ref[...] = (qs - ps) * scale


def _hy_kspec_call(tabs, hsd):
    n = hsd.shape[0]
    half = n // 2
    tf = min(512, half)
    tab = pl.BlockSpec((tf, half), lambda i: (i, 0))
    out = pl.BlockSpec((tf, HY_WIDTH), lambda i: (i, 0))
    return pl.pallas_call(
        functools.partial(_hy_kspec_kernel, scale=1.0 / n),
        out_shape=[jax.ShapeDtypeStruct((half, HY_WIDTH), F32)] * 4,
        grid=(half // tf,),
        in_specs=[tab, tab, tab, tab, pl.BlockSpec((half, 2 * HY_WIDTH), lambda i: (0, 0)),
                  pl.BlockSpec((half, 2 * HY_WIDTH), lambda i: (1, 0))],
        out_specs=[out] * 4,
        compiler_params=_params(1),
        name="hy_kspec",
    )(*tabs, hsd, hsd)


def _hy_fwd_kernel(ce_ref, co_ref, se_ref, so_ref, we_ref, wo_ref, krl_ref, kil_ref, krh_ref, kih_ref,
                   urp_ref, urm_ref, uim_ref, uip_ref):
    we = we_ref[0]
    wo = wo_ref[0]
    pc = _dot(ce_ref[...], we)
    qc = _dot(co_ref[...], wo)
    ps = _dot(se_ref[...], we)
    qs = _dot(so_ref[...], wo)
    a_lo, a_hi = pc + qc, pc - qc
    b_lo, b_hi = ps + qs, qs - ps
    krl, kil, krh, kih = krl_ref[...], kil_ref[...], krh_ref[...], kih_ref[...]
    yre_lo = a_lo * krl + b_lo * kil
    yim_lo = a_lo * kil - b_lo * krl
    yre_hi = a_hi * krh + b_hi * kih
    yim_hi = a_hi * kih - b_hi * krh
    urp_ref[0] = (yre_lo + yre_hi).astype(BF16)
    urm_ref[0] = (yre_lo - yre_hi).astype(BF16)
    uim_ref[0] = (yim_lo - yim_hi).astype(BF16)
    uip_ref[0] = (yim_lo + yim_hi).astype(BF16)


def _hy_fwd_call(tabs, we, wo, kspec):
    bsz, half, _ = we.shape
    tf = min(512, half)
    tab = pl.BlockSpec((tf, half), lambda i, b: (i, 0))
    sig = pl.BlockSpec((1, half, HY_WIDTH), lambda i, b: (b, 0, 0))
    kk = pl.BlockSpec((tf, HY_WIDTH), lambda i, b: (i, 0))
    out = pl.BlockSpec((1, tf, HY_WIDTH), lambda i, b: (b, i, 0))
    return pl.pallas_call(
        _hy_fwd_kernel,
        out_shape=[jax.ShapeDtypeStruct((bsz, half, HY_WIDTH), BF16)] * 4,
        grid=(half // tf, bsz),
        in_specs=[tab, tab, tab, tab, sig, sig, kk, kk, kk, kk],
        out_specs=[out] * 4,
        compiler_params=_params(2),
        name="hy_fwd",
    )(*tabs, we, wo, *kspec)


def _hy_inv_kernel(cte_ref, cto_ref, ste_ref, sto_ref, urp_ref, urm_ref, uim_ref, uip_ref, x0_ref, we_ref,
                   wo_ref, bias_ref, o_ref, y_scr):
    tt = cte_ref.shape[0]
    bias = bias_ref[...]
    y_even = _dot(cte_ref[...], urp_ref[0]) + _dot(ste_ref[...], uim_ref[0]) + we_ref[0].astype(F32) * bias
    y_odd = _dot(cto_ref[...], urm_ref[0]) + _dot(sto_ref[...], uip_ref[0]) + wo_ref[0].astype(F32) * bias
    for j in range(HY_WIDTH // LANES):
        sl = slice(j * LANES, (j + 1) * LANES)
        y_scr[j, pl.ds(0, tt, stride=2), :] = y_even[:, sl]
        y_scr[j, pl.ds(1, tt, stride=2), :] = y_odd[:, sl]
        o_ref[0, :, sl] = (x0_ref[0, :, sl].astype(F32) * y_scr[j]).astype(BF16)


def _hy_inv_call(tabs_t, us, x0, we, wo, bias):
    bsz, n, _ = x0.shape
    half = n // 2
    tt = min(512, half)
    tab = pl.BlockSpec((tt, half), lambda i, b: (i, 0))
    full = pl.BlockSpec((1, half, HY_WIDTH), lambda i, b: (b, 0, 0))
    tile = pl.BlockSpec((1, 2 * tt, HY_WIDTH), lambda i, b: (b, i, 0))
    htile = pl.BlockSpec((1, tt, HY_WIDTH), lambda i, b: (b, i, 0))
    return pl.pallas_call(
        _hy_inv_kernel,
        out_shape=jax.ShapeDtypeStruct((bsz, n, HY_WIDTH), BF16),
        grid=(half // tt, bsz),
        in_specs=[tab, tab, tab, tab, full, full, full, full, tile, htile, htile,
                  pl.BlockSpec((1, HY_WIDTH), lambda i, b: (0, 0))],
        out_specs=tile,
        scratch_shapes=[pltpu.VMEM((HY_WIDTH // LANES, 2 * tt, LANES), F32)],
        compiler_params=_params(2),
        name="hy_inv",
    )(*tabs_t, *us, x0, we, wo, bias)


LRU_SCAN_UNROLL = 4


def _lru_kernel(rxl_ref, rxc_ref, rgl_ref, rgc_ref, cw_ref, cb_ref, wa_ref, wx_ref, ba_ref, bx_ref, lam_ref,
                ol_ref, oc_ref, a_scr, b_scr, h_scr):
    cw = cw_ref[...]
    cb = cb_ref[...]
    row8 = lax.broadcasted_iota(I32, (SUBLANES, LANES), 0)

    def coeffs(x_ref, n):
        x = x_ref[0]
        xc = (cb + _shift_down(x, 2) * cw[0:1] + _shift_down(x, 1) * cw[1:2] + x * cw[2:3]
              + _shift_up(x, 1) * cw[3:4])
        xb = xc.astype(BF16)
        for d in range(2):
            r = _sigmoid(_dot(xb, wa_ref[d, 0]) + ba_ref[d])
            i = _sigmoid(_dot(xb, wx_ref[d, 0]) + bx_ref[d])
            nl = -lam_ref[d]
            softplus = jnp.maximum(nl, 0.0) + jnp.log(1.0 + jnp.exp(-jnp.abs(nl)))
            log_a = (-LRU_C) * r * softplus
            a = jnp.exp(log_a)
            a_scr[d, 0:n, :] = a
            b_scr[d, 0:n, :] = jnp.sqrt(1.0 - a * a) * i * xc

    def scan(n, carry_f, carry_b):
        ng = n // SUBLANES

        def body(g, carry):
            cf, cbk = carry
            of = pl.multiple_of(g * SUBLANES, SUBLANES)
            ob = pl.multiple_of((ng - 1 - g) * SUBLANES, SUBLANES)
            a = a_scr[0, pl.ds(of, SUBLANES), :]
            b = b_scr[0, pl.ds(of, SUBLANES), :]
            a2 = a_scr[1, pl.ds(ob, SUBLANES), :]
            b2 = b_scr[1, pl.ds(ob, SUBLANES), :]
            for k in (1, 2, 4):
                keep = row8 >= k
                b = a * jnp.where(keep, pltpu.roll(b, k, 0), 0.0) + b
                a = a * jnp.where(keep, pltpu.roll(a, k, 0), 1.0)
                keep2 = row8 < SUBLANES - k
                b2 = a2 * jnp.where(keep2, pltpu.roll(b2, SUBLANES - k, 0), 0.0) + b2
                a2 = a2 * jnp.where(keep2, pltpu.roll(a2, SUBLANES - k, 0), 1.0)
            hf = a * cf + b
            hb = a2 * cbk + b2
            h_scr[0, pl.ds(of, SUBLANES), :] = hf
            h_scr[1, pl.ds(ob, SUBLANES), :] = hb
            return hf[SUBLANES - 1:SUBLANES, :], hb[0:1, :]

        return lax.fori_loop(0, ng, body, (carry_f, carry_b), unroll=LRU_SCAN_UNROLL)

    n_ctx = rxc_ref.shape[1]
    n_lat = rxl_ref.shape[1]
    zero = jnp.zeros((1, LANES), F32)
    coeffs(rxc_ref, n_ctx)
    cf, cbk = scan(n_ctx, zero, zero)
    oc_ref[0] = ((h_scr[0, 0:n_ctx, :] + h_scr[1, 0:n_ctx, :]) * rgc_ref[0].astype(F32)).astype(BF16)
    coeffs(rxl_ref, n_lat)
    scan(n_lat, cf, cbk)
    ol_ref[0] = ((h_scr[0] + h_scr[1]) * rgl_ref[0].astype(F32)).astype(BF16)


def _lru_call(rx, rx_c, rg, rg_c, conv_w, conv_b, wa_bd, wx_bd, ba, bx, lam):
    bsz, t, _ = rx.shape
    tc = rx_c.shape[1]
    nch = LRU_WIDTH // LANES
    seq = lambda n: pl.BlockSpec((1, n, LANES), lambda b, j: (b, 0, j))
    vec = lambda r: pl.BlockSpec((r, 1, LANES), lambda b, j: (0, 0, j))
    wsp = pl.BlockSpec((2, 1, LANES, LANES), lambda b, j: (0, j, 0, 0))
    return pl.pallas_call(
        _lru_kernel,
        out_shape=[jax.ShapeDtypeStruct((bsz, t, LRU_WIDTH), BF16), jax.ShapeDtypeStruct((bsz, tc, LRU_WIDTH), BF16)],
        grid=(bsz, nch),
        in_specs=[seq(t), seq(tc), seq(t), seq(tc),
                  pl.BlockSpec((4, LANES), lambda b, j: (0, j)), pl.BlockSpec((1, LANES), lambda b, j: (0, j)),
                  wsp, wsp, vec(2), vec(2), vec(2)],
        out_specs=[seq(t), seq(tc)],
        scratch_shapes=[pltpu.VMEM((2, t, LANES), F32)] * 3,
        compiler_params=_params(2),
        name="lru",
    )(rx, rx_c, rg, rg_c, conv_w, conv_b, wa_bd, wx_bd, ba, bx, lam)


def _merge_kernel(attn_ref, hy_ref, lru_ref, gate_ref, h_ref, g1_ref, wa_ref, wh_ref, wl_ref, wo_ref,
                  n2_ref, sh2_ref, sc2_ref, rw_ref, hn_ref, u2_ref, lt_ref):
    def gate(j):
        return _sigmoid(gate_ref[0, :, j * D_MODEL:(j + 1) * D_MODEL].astype(F32))

    y = gate(0) * _dot(attn_ref[0], wa_ref[...])
    y = y + gate(1) * _dot(hy_ref[0], wh_ref[...])
    y = y + gate(2) * _dot(lru_ref[0], wl_ref[...])
    hn = h_ref[0] + g1_ref[0] * _dot(y.astype(BF16), wo_ref[...])
    hn_ref[0] = hn
    u2 = (_rms(hn) * n2_ref[...] * (1.0 + sc2_ref[0]) + sh2_ref[0]).astype(BF16)
    u2_ref[0] = u2
    lt_ref[0] = _dot_nt(rw_ref[...], u2)


def _merge_call(attn, hyo, lruo, gate, h, g1, wa, wh, wl, wo, n2g, sh2, sc2, rw_t, ctx_rows):
    bsz, t, d = h.shape
    tm = min(512, t)
    row = (lambda b: MOD_ROWS - 8) if ctx_rows else (lambda b: b)
    tok = lambda w: pl.BlockSpec((1, tm, w), lambda b, i: (b, i, 0))
    modspec = pl.BlockSpec((1, 1, d), lambda b, i: (row(b), 0, 0))
    const = lambda shape: pl.BlockSpec(shape, lambda b, i: (0,) * len(shape))
    return pl.pallas_call(
        _merge_kernel,
        out_shape=[jax.ShapeDtypeStruct((bsz, t, d), F32), jax.ShapeDtypeStruct((bsz, t, d), BF16),
                   jax.ShapeDtypeStruct((bsz, N_EXPERTS, t), F32)],
        grid=(bsz, t // tm),
        in_specs=[tok(Q_WIDTH), tok(HY_WIDTH), tok(LRU_WIDTH), tok(GATE_WIDTH), tok(d), modspec,
                  const((Q_WIDTH, d)), const((HY_WIDTH, d)), const((LRU_WIDTH, d)), const((d, d)),
                  const((1, d)), modspec, modspec, const((N_EXPERTS, d))],
        out_specs=[tok(d), tok(d), pl.BlockSpec((1, N_EXPERTS, tm), lambda b, i: (b, 0, i))],
        compiler_params=_params(2),
        name="merge",
    )(attn, hyo, lruo, gate, h, g1, wa, wh, wl, wo, n2g, sh2, sc2, rw_t)


def _router_kernel(lt_ref, slot_ref, slotc_ref, g_ref, coff_ref, *, cap):
    lg = lt_ref[0]
    n_e, t = lg.shape
    ex = jnp.exp(lg - lg.max(axis=0, keepdims=True))
    aff = ex / ex.sum(axis=0, keepdims=True)
    key = pltpu.bitcast(aff, I32)
    capf = float(cap)

    def count(mask):
        return jnp.where(mask, 1.0, 0.0).sum(axis=1, keepdims=True)

    def vbody(i, thr):
        sh = 28 - 2 * i
        out = thr
        for j in (1, 2, 3):
            cand = thr | lax.shift_left(jnp.int32(j), sh)
            out = jnp.where(count(key >= cand) >= capf, cand, out)
        return out

    thr = lax.fori_loop(0, 15, vbody, jnp.zeros((n_e, 1), I32))
    gt = key > thr
    eq = key == thr
    need = capf - count(gt)
    idx = lax.broadcasted_iota(I32, (n_e, t), 1)
    nbits = t.bit_length() - 1

    def ibody(i, lo):
        sh = nbits - 2 - 2 * i
        out = lo
        for j in (1, 2, 3):
            cand = lo | lax.shift_left(jnp.int32(j), sh)
            out = jnp.where(count(eq & (idx < cand)) < need, cand, out)
        return out

    last = lax.fori_loop(0, nbits // 2, ibody, jnp.zeros((n_e, 1), I32))
    if nbits % 2:
        last = jnp.where(count(eq & (idx < (last | 1))) < need, last | 1, last)
    sel = gt | (eq & (idx <= last))
    self32 = jnp.where(sel, 1.0, 0.0)
    g_ref[0] = jnp.where(sel, aff, 0.0)

    r_i = lax.broadcasted_iota(I32, (LANES, LANES), 0)
    c_i = lax.broadcasted_iota(I32, (LANES, LANES), 1)
    tri = jnp.where(r_i <= c_i, 1.0, 0.0).astype(BF16)
    eye = jnp.where(r_i == c_i, 1.0, 0.0).astype(BF16)
    off = jnp.zeros((n_e, 1), F32)
    n_units = t // LANES
    coff_ref[0] = jnp.zeros((n_e, LANES), I32)
    units = [slice(c * LANES, (c + 1) * LANES) for c in range(n_units)]
    incs = [_dot(self32[:, sl].astype(BF16), tri) for sl in units]
    slot1s = []
    for c, sl in enumerate(units):
        coff_ref[0, :, c:c + 1] = off.astype(I32)
        slot1 = jnp.where(sel[:, sl], incs[c] - self32[:, sl] + off + 1.0, 0.0)
        off = off + incs[c][:, LANES - 1:LANES]
        slot_ref[0, :, sl] = slot1.astype(I32) - 1
        slot1s.append(slot1)
    coff_ref[0, :, n_units:n_units + 1] = off.astype(I32)
    his = [jnp.floor(s1 * (1.0 / 16.0)) for s1 in slot1s]
    cols_hi = [_dot_nt(eye, hi.astype(BF16)) for hi in his]
    cols_lo = [_dot_nt(eye, (s1 - 16.0 * hi).astype(BF16)) for s1, hi in zip(slot1s, his)]
    for c, sl in enumerate(units):
        slotc_ref[0, sl, :] = (16.0 * cols_hi[c] + cols_lo[c]).astype(I32) - 1


def _router_call(logits_t, cap):
    bsz, n_e, t = logits_t.shape
    assert t % LANES == 0 and t // LANES < LANES
    row = pl.BlockSpec((1, n_e, t), lambda b: (b, 0, 0))
    slot_row, slot_col, g_row, coff = pl.pallas_call(
        functools.partial(_router_kernel, cap=cap),
        out_shape=[jax.ShapeDtypeStruct((bsz, n_e, t), I32), jax.ShapeDtypeStruct((bsz, t, n_e), I32),
                   jax.ShapeDtypeStruct((bsz, n_e, t), F32), jax.ShapeDtypeStruct((bsz, n_e, LANES), I32)],
        grid=(bsz,),
        in_specs=[row],
        out_specs=[row, pl.BlockSpec((1, t, n_e), lambda b: (b, 0, 0)), row,
                   pl.BlockSpec((1, n_e, LANES), lambda b: (b, 0, 0))],
        compiler_params=_params(1),
        name="router",
    )(logits_t)
    return slot_row, slot_col, g_row, coff[:, :, :t // LANES + 1].reshape(-1)


MOE_GATHER_TOKENS = 256
MOE_SLOT_ROWS = 128
MOE_SCATTER_WINDOW = 128
MOE_SCATTER_SLOTS = 256


def _log2(n):
    assert n & (n - 1) == 0
    return n.bit_length() - 1


def _expert_kernel(coff_ref, u_ref, slot_ref, g_ref, w1_ref, w3_ref, w2_ref, y_ref, xg_scr, gs_scr, *, tkg, sb,
                   n_units):
    cap = xg_scr.shape[0]
    n_chunks = u_ref.shape[1] // tkg
    base = (pl.program_id(0) * pl.num_programs(1) + pl.program_id(1)) * (n_units + 1)
    upc = tkg // LANES
    rid = lax.broadcasted_iota(I32, (sb, tkg), 0)
    xg_scr[...] = jnp.zeros(xg_scr.shape, F32)
    gs_scr[...] = jnp.zeros(gs_scr.shape, F32)

    def gather(match, r0, c_tokens, g_row):
        xg_scr[pl.ds(r0, sb), :] += _dot(jnp.where(match, 1.0, 0.0).astype(BF16), u_ref[0, c_tokens, :])
        gs_scr[pl.ds(r0, sb), :] += jnp.where(match, g_row, 0.0).sum(axis=1, keepdims=True)

    def window_start(c):
        lo = coff_ref[base + c * upc]
        return jnp.minimum(lo & (-SUBLANES), cap - sb)

    for c in range(n_chunks):
        r0 = pl.multiple_of(window_start(c), SUBLANES)
        gather(slot_ref[0, 0, c:c + 1, :] - r0 == rid, r0, slice(c * tkg, (c + 1) * tkg), g_ref[0, 0, c:c + 1, :])

    def overflow(c, carry):
        covered = window_start(c) + sb
        hi = coff_ref[base + (c + 1) * upc]
        first = lax.shift_right_logical(covered, _log2(sb))
        stop = jnp.where(hi > covered, lax.shift_right_logical(hi + (sb - 1), _log2(sb)), first)

        def block(j, carry2):
            b0 = pl.multiple_of(j * sb, sb)
            srow = slot_ref[0, 0, pl.ds(c, 1), :]
            gather((srow - b0 == rid) & (srow >= covered), b0, pl.ds(pl.multiple_of(c * tkg, tkg), tkg),
                   g_ref[0, 0, pl.ds(c, 1), :])
            return carry2

        return lax.fori_loop(first, stop, block, carry)

    lax.fori_loop(0, n_chunks, overflow, 0)
    xb = xg_scr[...].astype(BF16)
    hid = (_silu(_dot(xb, w1_ref[0, 0])) * _dot(xb, w3_ref[0, 0])).astype(BF16)
    y_ref[0, 0] = (_dot(hid, w2_ref[0, 0]) * gs_scr[...]).astype(BF16)


def _expert_call(u2, slot_row, g_row, coff, w1, w3, w2, layer, cap):
    bsz, t, d = u2.shape
    n_e = w1.shape[1]
    tkg = min(MOE_GATHER_TOKENS, t)
    sb = min(MOE_SLOT_ROWS, cap)
    wspec = pl.BlockSpec((1, 1, d, d), lambda b, e, co: (layer, e, 0, 0))
    chunks = pl.BlockSpec((1, 1, t // tkg, tkg), lambda b, e, co: (b, e, 0, 0))
    grid_spec = pltpu.PrefetchScalarGridSpec(
        num_scalar_prefetch=1,
        grid=(bsz, n_e),
        in_specs=[pl.BlockSpec((1, t, d), lambda b, e, co: (b, 0, 0)),
                  chunks, chunks, wspec, wspec, wspec],
        out_specs=pl.BlockSpec((1, 1, cap, d), lambda b, e, co: (b, e, 0, 0)),
        scratch_shapes=[pltpu.VMEM((cap, d), F32), pltpu.VMEM((cap, 1), F32)],
    )
    return pl.pallas_call(
        functools.partial(_expert_kernel, tkg=tkg, sb=sb, n_units=t // LANES),
        out_shape=jax.ShapeDtypeStruct((bsz, n_e, cap, d), BF16),
        grid_spec=grid_spec,
        compiler_params=_params(2),
        name="expert",
    )(coff, u2, slot_row.reshape(bsz, n_e, t // tkg, tkg), g_row.reshape(bsz, n_e, t // tkg, tkg), w1, w3, w2)


def _expert_call_folded(u2, slot_row, g_row, coff, w1, w3, w2, layer, cap):
    bsz, t, d = u2.shape
    n_e = slot_row.shape[1]
    n_units = t // LANES
    first = (jnp.arange(bsz, dtype=I32) * cap)[:, None, None]
    slot_all = jnp.where(slot_row >= 0, slot_row + first, -1).transpose(1, 0, 2).reshape(1, n_e, bsz * t)
    coff_all = (coff.reshape(bsz, n_e, n_units + 1)[:, :, :n_units] + first).transpose(1, 0, 2)
    coff_all = jnp.concatenate([coff_all.reshape(n_e, bsz * n_units), jnp.full((n_e, 1), bsz * cap, I32)], axis=1)
    g_all = g_row.transpose(1, 0, 2).reshape(1, n_e, bsz * t)
    y = _expert_call(u2.reshape(1, bsz * t, d), slot_all, g_all, coff_all.reshape(-1), w1, w3, w2, layer, bsz * cap)
    return y.reshape(n_e, bsz, cap, d).transpose(1, 0, 2, 3)


def _scatter_kernel(coff_ref, h_ref, y_ref, slotc_ref, g2_ref, fg_ref, o_ref, acc_scr, ystack, *, win, group, kb,
                    n_units, final_norm):
    tk = h_ref.shape[1]
    cap = y_ref.shape[2]
    upt = tk // LANES
    i = pl.program_id(1)
    lane = lax.broadcasted_iota(I32, (tk, win), 1)
    bf16_rows = 2 * SUBLANES

    def span(e):
        base = (pl.program_id(0) * N_EXPERTS + e) * (n_units + 1)
        lo = coff_ref[base + i * upt]
        hi = coff_ref[base + (i + 1) * upt]
        return jnp.minimum(lo & (-bf16_rows), cap - win), hi

    acc = jnp.zeros((tk, D_MODEL), F32)
    for p in range(N_EXPERTS // group):
        hots = []
        for j in range(group):
            e = p * group + j
            r0 = pl.multiple_of(span(e)[0], bf16_rows)
            ystack[p, j * win:(j + 1) * win, :] = y_ref[0, e, pl.ds(r0, win), :]
            hots.append(jnp.where(slotc_ref[0, :, e:e + 1] - r0 == lane, 1.0, 0.0).astype(BF16))
        onehot = hots[0] if group == 1 else jnp.concatenate(hots, axis=1)
        acc = acc + _dot(onehot, ystack[p])
    acc_scr[...] = acc

    lane_kb = lax.broadcasted_iota(I32, (tk, kb), 1)
    for e in range(N_EXPERTS):
        r0, hi = span(e)
        covered = r0 + win
        first = lax.shift_right_logical(covered, _log2(kb))
        stop = jnp.where(hi > covered, lax.shift_right_logical(hi + (kb - 1), _log2(kb)), first)

        def block(j, carry, e=e, covered=covered):
            b0 = pl.multiple_of(j * kb, kb)
            scol = slotc_ref[0, :, e:e + 1]
            onehot = jnp.where((scol - b0 == lane_kb) & (scol >= covered), 1.0, 0.0).astype(BF16)
            acc_scr[...] += _dot(onehot, y_ref[0, e, pl.ds(b0, kb), :])
            return carry

        lax.fori_loop(first, stop, block, 0)
    out = h_ref[0] + g2_ref[0] * acc_scr[...]
    o_ref[0] = _rms(out) * fg_ref[...] if final_norm else out


def _scatter_call(h, y, slot_col, coff, g2, cap, ctx_rows, final_g=None):
    bsz, t, d = h.shape
    n_e = y.shape[1]
    tk = min(512, t)
    win = min(MOE_SCATTER_WINDOW, cap)
    group = 2 if win == LANES else 1
    kb = min(MOE_SCATTER_SLOTS, cap)
    row = (lambda b: MOD_ROWS - 8) if ctx_rows else (lambda b: b)
    tile = lambda w: pl.BlockSpec((1, tk, w), lambda b, i, co: (b, i, 0))
    grid_spec = pltpu.PrefetchScalarGridSpec(
        num_scalar_prefetch=1,
        grid=(bsz, t // tk),
        in_specs=[tile(d),
                  pl.BlockSpec((1, n_e, cap, d), lambda b, i, co: (b, 0, 0, 0)),
                  tile(n_e), pl.BlockSpec((1, 1, d), lambda b, i, co: (row(b), 0, 0)),
                  pl.BlockSpec((1, d), lambda b, i, co: (0, 0))],
        out_specs=tile(d),
        scratch_shapes=[pltpu.VMEM((tk, d), F32), pltpu.VMEM((n_e // group, group * win, d), BF16)],
    )
    return pl.pallas_call(
        functools.partial(_scatter_kernel, win=win, group=group, kb=kb, n_units=t // LANES,
                          final_norm=final_g is not None),
        out_shape=jax.ShapeDtypeStruct((bsz, t, d), F32),
        grid_spec=grid_spec,
        compiler_params=_params(2),
        name="moe_scatter",
    )(coff, h, y, slot_col, g2, jnp.ones((1, d), F32) if final_g is None else final_g)


def _rope_tables(n_lat, n_ctx):
    rows = n_lat // GRID_W
    row = jnp.repeat(jnp.arange(rows, dtype=F32), GRID_W)
    col = jnp.tile(jnp.arange(GRID_W, dtype=F32), rows)
    inv = jnp.power(ROPE_THETA, -jnp.arange(ROPE_PAIRS_AXIS, dtype=F32) / ROPE_PAIRS_AXIS)
    ang = jnp.concatenate([row[:, None] * inv, col[:, None] * inv], axis=-1)
    cos, sin = jnp.cos(ang), jnp.sin(ang)
    cs = jnp.concatenate([cos, cos], axis=-1)
    sn = jnp.concatenate([-sin, sin], axis=-1)
    return cs, sn, jnp.ones((n_ctx, HEAD_DIM), F32), jnp.zeros((n_ctx, HEAD_DIM), F32)


def _dft_tables(n):
    k = jnp.arange(n // 2, dtype=I32)

    def tables(first):
        m = ((2 * k[:, None] + 1) * (2 * k[None, :] + first)) % (4 * n)
        ang = m.astype(F32) * (2.0 * math.pi / (4 * n))
        return jnp.cos(ang), jnp.sin(ang)

    ce, se = tables(0)
    co, so = tables(1)
    fwd = tuple(x.astype(BF16) for x in (ce, co, se, so))
    inv = tuple(x.astype(BF16) for x in (ce.T, co, -se.T, -so))
    return fwd, inv


def _filter_features(n):
    t = jnp.linspace(0.0, 1.0, n, dtype=F32)[:, None]
    w = (2.0 * math.pi / n) * jnp.arange(n, dtype=F32)[:, None]
    f = jnp.linspace(1e-4, HY_BANDS - 1, HY_BANDS, dtype=F32)[None, :]
    feats = jnp.concatenate([t, jnp.cos(f * w), -jnp.sin(f * w)], axis=-1)
    feats = jnp.concatenate([feats[0::2], feats[1::2]], axis=0)
    return jnp.pad(feats, ((0, 0), (0, LANES - HY_EMB)))


def _pad_to(x, shape):
    return jnp.pad(x, [(0, s - d) for d, s in zip(x.shape, shape)])


def _block_diag_chunks(w):
    per = LANES // LRU_BLOCK
    w = w.reshape(2, LRU_BLOCKS // per, per, LRU_BLOCK, LRU_BLOCK)
    eye = jnp.eye(per, dtype=w.dtype)
    return jnp.einsum('dcpkj,pq->dcpkqj', w, eye).reshape(2, LRU_BLOCKS // per, LANES, LANES)


def kernel(x, c, ctx, c_ctx, mod_w, mod_b, norm1_g, norm2_g, w_in, q_norm_g, k_norm_g, hy_conv_w, hy_conv_b,
           hy_fw1, hy_fb1, hy_fw2, hy_fb2, hy_fw3, hy_freq, hy_bias, lru_conv_w, lru_conv_b, lru_wa, lru_ba,
           lru_wx, lru_bx, lru_lambda, w_attn_out, w_hy_out, w_lru_out, w_out, router_w, exp_w1, exp_w3, exp_w2,
           final_norm_g):
    bsz, n_lat, d = x.shape
    n_ctx = ctx.shape[1]
    depth = mod_w.shape[0]
    assert d == D_MODEL and bsz <= MOD_ROWS - 8 and n_lat % GRID_W == 0

    cs_l, sn_l, cs_c, sn_c = _rope_tables(n_lat, n_ctx)
    dft_l = _dft_tables(n_lat)
    dft_c = _dft_tables(n_ctx)
    feats_l = _filter_features(n_lat)
    feats_c = _filter_features(n_ctx)
    max_decay = math.log(HY_DECAY_TARGET) / HY_FAST_DECAY_PCT
    min_decay = math.log(HY_DECAY_TARGET) / HY_SLOW_DECAY_PCT
    deltas_abs = jnp.abs(jnp.linspace(min_decay, max_decay, HY_WIDTH, dtype=F32))[None, :]
    cap_l = max(1, EC_CAPACITY * n_lat // N_EXPERTS)
    cap_c = max(1, EC_CAPACITY * n_ctx // N_EXPERTS)

    cvec = jnp.zeros((MOD_ROWS, d), F32).at[:bsz].set(c).at[MOD_ROWS - 8].set(c_ctx)
    mod = _mod_call(cvec, mod_w, mod_b)
    mod = mod.reshape(depth, MOD_ROWS, 6, 1, d).transpose(0, 2, 1, 3, 4)

    w_in_b = w_in.astype(BF16)
    e1 = exp_w1.astype(BF16)
    e3 = exp_w3.astype(BF16)
    e2 = exp_w2.astype(BF16)

    h, hc = x, ctx
    for l in range(depth):
        last = l == depth - 1
        sh1, sc1, g1, sh2, sc2, g2 = (mod[l, j] for j in range(6))
        n1 = norm1_g[l][None, :]
        n2 = norm2_g[l][None, :]
        qg = q_norm_g[l][None, :]
        kg = k_norm_g[l][None, :]

        q, k, v, hy, rx, rg, gate = _inproj_call(h, sh1, sc1, n1, w_in_b, l, cs_l, sn_l, qg, kg, False)
        qc, kc, vc, hyc, rxc, rgc, gatec = _inproj_call(hc, sh1, sc1, n1, w_in_b, l, cs_c, sn_c, qg, kg, True)

        attn = _attn_call(q, [(k, v), (kc, vc)])

        fw1 = _pad_to(hy_fw1[l], (LANES, LANES))
        fb1 = _pad_to(hy_fb1[l][None, :], (1, LANES))
        fw2 = _pad_to(hy_fw2[l], (LANES, LANES))
        fb2 = _pad_to(hy_fb2[l][None, :], (1, LANES))
        fw3 = _pad_to(hy_fw3[l], (LANES, 2 * HY_WIDTH))
        freq = _pad_to(hy_freq[l], (2, LANES))
        hbias = hy_bias[l][None, :]

        def hyena(hy_in, feats, tabs):
            tabs_fwd, tabs_inv = tabs
            hsd = _hy_filter_call(feats, fw1, fb1, fw2, fb2, fw3, freq, deltas_abs)
            kspec = _hy_kspec_call(tabs_fwd, hsd)
            x0, we, wo = _hy_pre_call(hy_in, hy_conv_w[l], hy_conv_b[l][None, :])
            us = _hy_fwd_call(tabs_fwd, we, wo, kspec)
            return _hy_inv_call(tabs_inv, us, x0, we, wo, hbias)

        hyo = hyena(hy, feats_l, dft_l)

        lruo, lruoc = _lru_call(rx, rxc, rg, rgc, lru_conv_w[l], lru_conv_b[l][None, :],
                                _block_diag_chunks(lru_wa[l]).astype(BF16), _block_diag_chunks(lru_wx[l]).astype(BF16),
                                lru_ba[l][:, None, :], lru_bx[l][:, None, :], lru_lambda[l][:, None, :])

        wa = w_attn_out[l].astype(BF16)
        wh = w_hy_out[l].astype(BF16)
        wl = w_lru_out[l].astype(BF16)
        wo = w_out[l].astype(BF16)
        rw_t = router_w[l].T.astype(BF16)

        def channel_mix(h_in, attn_in, hy_in, lru_in, gate_in, cap, ctx_rows, final_g=None):
            hn, u2, lt = _merge_call(attn_in, hy_in, lru_in, gate_in, h_in, g1, wa, wh, wl, wo, n2, sh2, sc2, rw_t,
                                     ctx_rows)
            slot_row, slot_col, g_row, coff = _router_call(lt, cap)
            if ctx_rows:
                y = _expert_call_folded(u2, slot_row, g_row, coff, e1, e3, e2, l, cap)
            else:
                y = _expert_call(u2, slot_row, g_row, coff, e1, e3, e2, l, cap)
            return _scatter_call(hn, y, slot_col, coff, g2, cap, ctx_rows, final_g)

        h = channel_mix(h, attn, hyo, lruo, gate, cap_l, False, final_norm_g[None, :] if last else None)
        if not last:
            attn_c = _attn_call(qc, [(kc, vc)])
            hyo_c = hyena(hyc, feats_c, dft_c)
            hc = channel_mix(hc, attn_c, hyo_c, lruoc, gatec, cap_c, True)

    return h
```

```python
import functools
import math

import jax
import jax.numpy as jnp
from jax import lax
from jax.experimental import pallas as pl
from jax.experimental.pallas import tpu as pltpu

F32 = jnp.float32
BF16 = jnp.bfloat16
I32 = jnp.int32

D_MODEL = 1024
GRID_W = 64
NORM_EPS = 1e-6
N_HEADS = 8
N_KV_HEADS = 2
HEAD_DIM = 128
GROUP = N_HEADS // N_KV_HEADS
ROPE_PAIRS_AXIS = HEAD_DIM // 4
ROPE_THETA = 10000.0
ATTN_SCALE = HEAD_DIM ** -0.5
Q_PRESCALE = ATTN_SCALE * math.log2(math.e)
Q_WIDTH = N_HEADS * HEAD_DIM
KV_WIDTH = N_KV_HEADS * HEAD_DIM
HY_WIDTH = D_MODEL // 2
HY_BANDS = 16
HY_EMB = 1 + 2 * HY_BANDS
HY_FILTER_HIDDEN = 64
HY_FAST_DECAY_PCT = 0.3
HY_SLOW_DECAY_PCT = 1.5
HY_DECAY_TARGET = 1e-2
LRU_WIDTH = D_MODEL // 2
LRU_BLOCKS = 8
LRU_BLOCK = LRU_WIDTH // LRU_BLOCKS
LRU_C = 8.0
N_BRANCH = 3
N_EXPERTS = 16
EC_CAPACITY = 2
GATE_WIDTH = N_BRANCH * D_MODEL
OFF_Q = 0
OFF_K = OFF_Q + Q_WIDTH
OFF_V = OFF_K + KV_WIDTH
OFF_HY = OFF_V + KV_WIDTH
OFF_RX = OFF_HY + 3 * HY_WIDTH
OFF_RG = OFF_RX + LRU_WIDTH
OFF_GATE = OFF_RG + LRU_WIDTH
IN_TOTAL = OFF_GATE + GATE_WIDTH

LANES = 128
SUBLANES = 8
V7X_VMEM_LIMIT_BYTES = 56 * 1024 * 1024
MOD_ROWS = 24


def _params(n_axes, vmem=V7X_VMEM_LIMIT_BYTES):
    return pltpu.CompilerParams(dimension_semantics=("arbitrary",) * n_axes, vmem_limit_bytes=vmem)


def _dot(a, b):
    return jnp.dot(a, b, preferred_element_type=F32)


def _dot_nt(a, b):
    return lax.dot_general(a, b, (((1,), (1,)), ((), ())), preferred_element_type=F32)


def _split2(x):
    hi = x.astype(BF16)
    lo = (x - hi.astype(F32)).astype(BF16)
    return hi, lo


def _dot3(a, b):
    ah, al = _split2(a)
    bh, bl = _split2(b)
    return _dot(ah, bh) + (_dot(ah, bl) + _dot(al, bh))


def _rms(x):
    return x * lax.rsqrt(jnp.mean(x * x, axis=-1, keepdims=True) + NORM_EPS)


def _sigmoid(x):
    return 0.5 * jnp.tanh(0.5 * x) + 0.5


def _silu(x):
    return x * _sigmoid(x)


def _gelu_tanh(x):
    return 0.5 * x * (1.0 + jnp.tanh(math.sqrt(2.0 / math.pi) * (x + 0.044715 * (x * x * x))))


def _shift_down(x, k):
    row = lax.broadcasted_iota(I32, x.shape, 0)
    return jnp.where(row >= k, pltpu.roll(x, k, 0), 0.0)


def _shift_up(x, k):
    n = x.shape[0]
    row = lax.broadcasted_iota(I32, x.shape, 0)
    return jnp.where(row < n - k, pltpu.roll(x, n - k, 0), 0.0)


def _mod_kernel(c_ref, w_ref, b_ref, o_ref):
    o_ref[0] = _dot3(_silu(c_ref[...]), w_ref[0]) + b_ref[0]


def _mod_call(cvec, mod_w, mod_b):
    depth, d, six_d = mod_w.shape
    tn = 1536
    return pl.pallas_call(
        _mod_kernel,
        out_shape=jax.ShapeDtypeStruct((depth, MOD_ROWS, six_d), F32),
        grid=(depth, six_d // tn),
        in_specs=[
            pl.BlockSpec((MOD_ROWS, d), lambda l, j: (0, 0)),
            pl.BlockSpec((1, d, tn), lambda l, j: (l, 0, j)),
            pl.BlockSpec((1, 1, tn), lambda l, j: (l, 0, j)),
        ],
        out_specs=pl.BlockSpec((1, MOD_ROWS, tn), lambda l, j: (l, 0, j)),
        compiler_params=_params(2),
        name="mod",
    )(cvec, mod_w, mod_b.reshape(depth, 1, six_d))


def _inproj_kernel(h_ref, sh_ref, sc_ref, g_ref, w_ref, cs_ref, sn_ref, qg_ref, kg_ref,
                   q_ref, k_ref, v_ref, hy_ref, rx_ref, rg_ref, gate_ref):
    u = (_rms(h_ref[0]) * g_ref[...] * (1.0 + sc_ref[0]) + sh_ref[0]).astype(BF16)
    cs = cs_ref[...]
    sn = sn_ref[...]

    def normed_rope(z, g):
        r = _rms(z) * g
        return r * cs + pltpu.roll(r, HEAD_DIM // 2, 1) * sn

    zq = _dot(u, w_ref[0, :, OFF_Q:OFF_Q + Q_WIDTH])
    for hd in range(N_HEADS):
        sl = slice(hd * HEAD_DIM, (hd + 1) * HEAD_DIM)
        q_ref[0, hd] = (normed_rope(zq[:, sl], qg_ref[...]) * Q_PRESCALE).astype(BF16)
    zkv = _dot(u, w_ref[0, :, OFF_K:OFF_K + 2 * KV_WIDTH])
    for hd in range(N_KV_HEADS):
        sl = slice(hd * HEAD_DIM, (hd + 1) * HEAD_DIM)
        k_ref[0, :, sl] = normed_rope(zkv[:, sl], kg_ref[...]).astype(BF16)
    v_ref[0] = zkv[:, KV_WIDTH:].astype(BF16)
    for j in range(3):
        sl = slice(j * HY_WIDTH, (j + 1) * HY_WIDTH)
        hy_ref[0, :, sl] = _dot(u, w_ref[0, :, OFF_HY + j * HY_WIDTH:OFF_HY + (j + 1) * HY_WIDTH]).astype(BF16)
    rr = _dot(u, w_ref[0, :, OFF_RX:OFF_RX + 2 * LRU_WIDTH])
    rx_ref[0] = rr[:, :LRU_WIDTH]
    rg_ref[0] = _gelu_tanh(rr[:, LRU_WIDTH:]).astype(BF16)
    for j in range(N_BRANCH):
        sl = slice(j * D_MODEL, (j + 1) * D_MODEL)
        gate_ref[0, :, sl] = _dot(u, w_ref[0, :, OFF_GATE + j * D_MODEL:OFF_GATE + (j + 1) * D_MODEL]).astype(BF16)


def _inproj_call(h, sh, sc, g, w_bf, layer, cs, sn, qg, kg, ctx_rows):
    bsz, t, d = h.shape
    tm = min(512, t)
    row = (lambda b: MOD_ROWS - 8) if ctx_rows else (lambda b: b)
    tok = lambda w: pl.BlockSpec((1, tm, w), lambda b, i: (b, i, 0))
    modspec = pl.BlockSpec((1, 1, d), lambda b, i: (row(b), 0, 0))
    const = lambda shape: pl.BlockSpec(shape, lambda b, i: (0,) * len(shape))
    outs = [(KV_WIDTH, BF16), (KV_WIDTH, BF16), (3 * HY_WIDTH, BF16), (LRU_WIDTH, F32), (LRU_WIDTH, BF16),
            (GATE_WIDTH, BF16)]
    return pl.pallas_call(
        _inproj_kernel,
        out_shape=[jax.ShapeDtypeStruct((bsz, N_HEADS, t, HEAD_DIM), BF16)]
        + [jax.ShapeDtypeStruct((bsz, t, w), dt) for w, dt in outs],
        grid=(bsz, t // tm),
        in_specs=[
            tok(d), modspec, modspec, const((1, d)),
            pl.BlockSpec((1, d, IN_TOTAL), lambda b, i: (layer, 0, 0), pipeline_mode=pl.Buffered(1)),
            pl.BlockSpec((tm, HEAD_DIM), lambda b, i: (i, 0)),
            pl.BlockSpec((tm, HEAD_DIM), lambda b, i: (i, 0)),
            const((1, HEAD_DIM)), const((1, HEAD_DIM)),
        ],
        out_specs=[pl.BlockSpec((1, N_HEADS, tm, HEAD_DIM), lambda b, i: (b, 0, i, 0))] + [tok(w) for w, _ in outs],
        compiler_params=_params(2),
        name="inproj",
    )(h, sh, sc, g, w_bf, cs, sn, qg, kg)


ATTN_TQ = 1024
ATTN_TK = 512
ATTN_STREAMS = 4
NEG_BIG = -1e30


def _attn_kernel(*refs, n_src, tq):
    q_ref = refs[0]
    kv = refs[1:1 + 2 * n_src]
    o_ref, kcat, vext = refs[1 + 2 * n_src:]
    n_keys = kcat.shape[0]

    @pl.when(pl.program_id(2) == 0)
    def _():
        off = 0
        for i in range(n_src):
            n = kv[2 * i].shape[1]
            kcat[off:off + n, :] = kv[2 * i][0]
            vext[off:off + n, 0:HEAD_DIM] = kv[2 * i + 1][0]
            off += n
        vext[:, HEAD_DIM:] = jnp.ones((n_keys, HEAD_DIM), BF16)

    rows = GROUP * tq
    q = q_ref[0].reshape(rows, HEAD_DIM)
    per = rows // ATTN_STREAMS
    qs = [q[i * per:(i + 1) * per] for i in range(ATTN_STREAMS)]
    ms = [jnp.full((per, 1), NEG_BIG, F32) for _ in qs]
    accs = [jnp.zeros((per, 2 * HEAD_DIM), F32) for _ in qs]
    for off in range(0, n_keys, ATTN_TK):
        size = min(ATTN_TK, n_keys - off)
        kk = kcat[off:off + size, :]
        vv = vext[off:off + size, :]
        for i in range(ATTN_STREAMS):
            s = _dot_nt(qs[i], kk)
            m_new = jnp.maximum(ms[i], s.max(axis=-1, keepdims=True))
            p = jnp.exp2(s - m_new).astype(BF16)
            accs[i] = jnp.exp2(ms[i] - m_new) * accs[i] + _dot(p, vv)
            ms[i] = m_new
    out = jnp.concatenate([a[:, :HEAD_DIM] / a[:, HEAD_DIM:] for a in accs], axis=0).astype(BF16)
    for g in range(GROUP):
        o_ref[0, :, g * HEAD_DIM:(g + 1) * HEAD_DIM] = out[g * tq:(g + 1) * tq]


def _attn_call(q, kvs):
    bsz, _, tq_all, _ = q.shape
    tq = min(ATTN_TQ, tq_all)
    gw = GROUP * HEAD_DIM
    in_specs = [pl.BlockSpec((1, GROUP, tq, HEAD_DIM), lambda b, h, i: (b, h, i, 0))]
    args = [q]
    n_keys = 0
    for k, v in kvs:
        tk = k.shape[1]
        n_keys += tk
        spec = pl.BlockSpec((1, tk, HEAD_DIM), lambda b, h, i: (b, 0, h))
        in_specs += [spec, spec]
        args += [k, v]
    return pl.pallas_call(
        functools.partial(_attn_kernel, n_src=len(kvs), tq=tq),
        out_shape=jax.ShapeDtypeStruct((bsz, tq_all, Q_WIDTH), BF16),
        grid=(bsz, N_KV_HEADS, tq_all // tq),
        in_specs=in_specs,
        out_specs=pl.BlockSpec((1, tq, gw), lambda b, h, i: (b, i, h)),
        scratch_shapes=[pltpu.VMEM((n_keys, HEAD_DIM), BF16), pltpu.VMEM((n_keys, 2 * HEAD_DIM), BF16)],
        compiler_params=_params(3),
        name="attn",
    )(*args)


def _hy_pre_kernel(h0_ref, h1_ref, h2_ref, w0_ref, w1_ref, w2_ref, b0_ref, b1_ref, b2_ref,
                   x0_ref, we_ref, wo_ref, w_scr):
    def conv(x_ref, cw_ref, cb_ref):
        x = x_ref[0].astype(F32)
        cw = cw_ref[...]
        return cb_ref[...] + _shift_down(x, 1) * cw[0:1] + x * cw[1:2] + _shift_up(x, 1) * cw[2:3]

    x0_ref[0] = conv(h0_ref, w0_ref, b0_ref).astype(BF16)
    w_scr[...] = conv(h2_ref, w2_ref, b2_ref) * conv(h1_ref, w1_ref, b1_ref)
    half = w_scr.shape[0] // 2
    we_ref[0] = w_scr[pl.ds(0, half, stride=2), :].astype(BF16)
    wo_ref[0] = w_scr[pl.ds(1, half, stride=2), :].astype(BF16)


def _hy_pre_call(hy, conv_w, conv_b):
    bsz, t, _ = hy.shape
    nch = HY_WIDTH // LANES
    xs = lambda part: pl.BlockSpec((1, t, LANES), lambda b, j: (b, 0, part * nch + j))
    ws = lambda part: pl.BlockSpec((3, LANES), lambda b, j: (0, part * nch + j))
    bs = lambda part: pl.BlockSpec((1, LANES), lambda b, j: (0, part * nch + j))
    out = pl.BlockSpec((1, t, LANES), lambda b, j: (b, 0, j))
    outh = pl.BlockSpec((1, t // 2, LANES), lambda b, j: (b, 0, j))
    return pl.pallas_call(
        _hy_pre_kernel,
        out_shape=[jax.ShapeDtypeStruct((bsz, t, HY_WIDTH), BF16),
                   jax.ShapeDtypeStruct((bsz, t // 2, HY_WIDTH), BF16),
                   jax.ShapeDtypeStruct((bsz, t // 2, HY_WIDTH), BF16)],
        grid=(bsz, nch),
        in_specs=[xs(0), xs(1), xs(2), ws(0), ws(1), ws(2), bs(0), bs(1), bs(2)],
        out_specs=[out, outh, outh],
        scratch_shapes=[pltpu.VMEM((t, LANES), F32)],
        compiler_params=_params(2),
        name="hy_pre",
    )(hy, hy, hy, conv_w, conv_w, conv_w, conv_b, conv_b, conv_b)


def _hy_filter_kernel(feat_ref, w1_ref, b1_ref, w2_ref, b2_ref, w3_ref, fr_ref, dl_ref, o_ref):
    feats = feat_ref[...]
    hid = jnp.sin(fr_ref[0:1] * (_dot3(feats, w1_ref[...]) + b1_ref[...]))
    hid = jnp.sin(fr_ref[1:2] * (_dot3(hid, w2_ref[...]) + b2_ref[...]))
    filt = _dot3(hid, w3_ref[...])
    decay = jnp.exp(-feats[:, 0:1] * dl_ref[...])
    h_fwd = filt[:, :HY_WIDTH] * decay
    h_bwd = filt[:, HY_WIDTH:] * decay
    o_ref[:, :HY_WIDTH] = h_fwd + h_bwd
    o_ref[:, HY_WIDTH:] = h_bwd - h_fwd


def _hy_filter_call(feats, fw1, fb1, fw2, fb2, fw3, freq, deltas_abs):
    n = feats.shape[0]
    tn = min(512, n)
    const = lambda shape: pl.BlockSpec(shape, lambda i: (0,) * len(shape))
    return pl.pallas_call(
        _hy_filter_kernel,
        out_shape=jax.ShapeDtypeStruct((n, 2 * HY_WIDTH), F32),
        grid=(n // tn,),
        in_specs=[pl.BlockSpec((tn, LANES), lambda i: (i, 0)), const((LANES, LANES)), const((1, LANES)),
                  const((LANES, LANES)), const((1, LANES)), const((LANES, 2 * HY_WIDTH)), const((2, LANES)),
                  const((1, HY_WIDTH))],
        out_specs=pl.BlockSpec((tn, 2 * HY_WIDTH), lambda i: (i, 0)),
        compiler_params=_params(1),
        name="hy_filter",
    )(feats, fw1, fb1, fw2, fb2, fw3, freq, deltas_abs)


def _hy_kspec_kernel(ce_ref, co_ref, se_ref, so_ref, he_ref, ho_ref, krl_ref, kil_ref, krh_ref, kih_ref, *, scale):
    def branch(tab_e, tab_o, cols):
        he_hi, he_lo = _split2(he_ref[:, cols])
        ho_hi, ho_lo = _split2(ho_ref[:, cols])
        pe = _dot(tab_e[...], he_hi) + _dot(tab_e[...], he_lo)
        po = _dot(tab_o[...], ho_hi) + _dot(tab_o[...], ho_lo)
        return pe, po

    pc, qc = branch(ce_ref, co_ref, slice(0, HY_WIDTH))
    ps, qs = branch(se_ref, so_ref, slice(HY_WIDTH, 2 * HY_WIDTH))
    krl_ref[...] = (pc + qc) * scale
    krh_ref[...] = (pc - qc) * scale
    kil_ref[...] = (ps + qs) * scale
    kih_ref[...] = (qs - ps) * scale


def _hy_kspec_call(tabs, hsd):
    n = hsd.shape[0]
    half = n // 2
    tf = min(512, half)
    tab = pl.BlockSpec((tf, half), lambda i: (i, 0))
    out = pl.BlockSpec((tf, HY_WIDTH), lambda i: (i, 0))
    return pl.pallas_call(
        functools.partial(_hy_kspec_kernel, scale=1.0 / n),
        out_shape=[jax.ShapeDtypeStruct((half, HY_WIDTH), F32)] * 4,
        grid=(half // tf,),
        in_specs=[tab, tab, tab, tab, pl.BlockSpec((half, 2 * HY_WIDTH), lambda i: (0, 0)),
                  pl.BlockSpec((half, 2 * HY_WIDTH), lambda i: (1, 0))],
        out_specs=[out] * 4,
        compiler_params=_params(1),
        name="hy_kspec",
    )(*tabs, hsd, hsd)


def _hy_fwd_kernel(ce_ref, co_ref, se_ref, so_ref, we_ref, wo_ref, krl_ref, kil_ref, krh_ref, kih_ref,
                   urp_ref, urm_ref, uim_ref, uip_ref):
    we = we_ref[0]
    wo = wo_ref[0]
    pc = _dot(ce_ref[...], we)
    qc = _dot(co_ref[...], wo)
    ps = _dot(se_ref[...], we)
    qs = _dot(so_ref[...], wo)
    a_lo, a_hi = pc + qc, pc - qc
    b_lo, b_hi = ps + qs, qs - ps
    krl, kil, krh, kih = krl_ref[...], kil_ref[...], krh_ref[...], kih_ref[...]
    yre_lo = a_lo * krl + b_lo * kil
    yim_lo = a_lo * kil - b_lo * krl
    yre_hi = a_hi * krh + b_hi * kih
    yim_hi = a_hi * kih - b_hi * krh
    urp_ref[0] = (yre_lo + yre_hi).astype(BF16)
    urm_ref[0] = (yre_lo - yre_hi).astype(BF16)
    uim_ref[0] = (yim_lo - yim_hi).astype(BF16)
    uip_ref[0] = (yim_lo + yim_hi).astype(BF16)


def _hy_fwd_call(tabs, we, wo, kspec):
    bsz, half, _ = we.shape
    tf = min(512, half)
    tab = pl.BlockSpec((tf, half), lambda i, b: (i, 0))
    sig = pl.BlockSpec((1, half, HY_WIDTH), lambda i, b: (b, 0, 0))
    kk = pl.BlockSpec((tf, HY_WIDTH), lambda i, b: (i, 0))
    out = pl.BlockSpec((1, tf, HY_WIDTH), lambda i, b: (b, i, 0))
    return pl.pallas_call(
        _hy_fwd_kernel,
        out_shape=[jax.ShapeDtypeStruct((bsz, half, HY_WIDTH), BF16)] * 4,
        grid=(half // tf, bsz),
        in_specs=[tab, tab, tab, tab, sig, sig, kk, kk, kk, kk],
        out_specs=[out] * 4,
        compiler_params=_params(2),
        name="hy_fwd",
    )(*tabs, we, wo, *kspec)


def _hy_inv_kernel(cte_ref, cto_ref, ste_ref, sto_ref, urp_ref, urm_ref, uim_ref, uip_ref, x0_ref, we_ref,
                   wo_ref, bias_ref, o_ref, y_scr):
    tt = cte_ref.shape[0]
    bias = bias_ref[...]
    y_even = _dot(cte_ref[...], urp_ref[0]) + _dot(ste_ref[...], uim_ref[0]) + we_ref[0].astype(F32) * bias
    y_odd = _dot(cto_ref[...], urm_ref[0]) + _dot(sto_ref[...], uip_ref[0]) + wo_ref[0].astype(F32) * bias
    for j in range(HY_WIDTH // LANES):
        sl = slice(j * LANES, (j + 1) * LANES)
        y_scr[j, pl.ds(0, tt, stride=2), :] = y_even[:, sl]
        y_scr[j, pl.ds(1, tt, stride=2), :] = y_odd[:, sl]
        o_ref[0, :, sl] = (x0_ref[0, :, sl].astype(F32) * y_scr[j]).astype(BF16)


def _hy_inv_call(tabs_t, us, x0, we, wo, bias):
    bsz, n, _ = x0.shape
    half = n // 2
    tt = min(512, half)
    tab = pl.BlockSpec((tt, half), lambda i, b: (i, 0))
    full = pl.BlockSpec((1, half, HY_WIDTH), lambda i, b: (b, 0, 0))
    tile = pl.BlockSpec((1, 2 * tt, HY_WIDTH), lambda i, b: (b, i, 0))
    htile = pl.BlockSpec((1, tt, HY_WIDTH), lambda i, b: (b, i, 0))
    return pl.pallas_call(
        _hy_inv_kernel,
        out_shape=jax.ShapeDtypeStruct((bsz, n, HY_WIDTH), BF16),
        grid=(half // tt, bsz),
        in_specs=[tab, tab, tab, tab, full, full, full, full, tile, htile, htile,
                  pl.BlockSpec((1, HY_WIDTH), lambda i, b: (0, 0))],
        out_specs=tile,
        scratch_shapes=[pltpu.VMEM((HY_WIDTH // LANES, 2 * tt, LANES), F32)],
        compiler_params=_params(2),
        name="hy_inv",
    )(*tabs_t, *us, x0, we, wo, bias)


LRU_SCAN_UNROLL = 4


def _lru_kernel(rxl_ref, rxc_ref, rgl_ref, rgc_ref, cw_ref, cb_ref, wa_ref, wx_ref, ba_ref, bx_ref, lam_ref,
                ol_ref, oc_ref, a_scr, b_scr, h_scr):
    cw = cw_ref[...]
    cb = cb_ref[...]
    row8 = lax.broadcasted_iota(I32, (SUBLANES, LANES), 0)

    def coeffs(x_ref, n):
        x = x_ref[0]
        xc = (cb + _shift_down(x, 2) * cw[0:1] + _shift_down(x, 1) * cw[1:2] + x * cw[2:3]
              + _shift_up(x, 1) * cw[3:4])
        xb = xc.astype(BF16)
        for d in range(2):
            r = _sigmoid(_dot(xb, wa_ref[d, 0]) + ba_ref[d])
            i = _sigmoid(_dot(xb, wx_ref[d, 0]) + bx_ref[d])
            nl = -lam_ref[d]
            softplus = jnp.maximum(nl, 0.0) + jnp.log(1.0 + jnp.exp(-jnp.abs(nl)))
            log_a = (-LRU_C) * r * softplus
            a = jnp.exp(log_a)
            a_scr[d, 0:n, :] = a
            b_scr[d, 0:n, :] = jnp.sqrt(1.0 - a * a) * i * xc

    def scan(n, carry_f, carry_b):
        ng = n // SUBLANES

        def body(g, carry):
            cf, cbk = carry
            of = pl.multiple_of(g * SUBLANES, SUBLANES)
            ob = pl.multiple_of((ng - 1 - g) * SUBLANES, SUBLANES)
            a = a_scr[0, pl.ds(of, SUBLANES), :]
            b = b_scr[0, pl.ds(of, SUBLANES), :]
            a2 = a_scr[1, pl.ds(ob, SUBLANES), :]
            b2 = b_scr[1, pl.ds(ob, SUBLANES), :]
            for k in (1, 2, 4):
                keep = row8 >= k
                b = a * jnp.where(keep, pltpu.roll(b, k, 0), 0.0) + b
                a = a * jnp.where(keep, pltpu.roll(a, k, 0), 1.0)
                keep2 = row8 < SUBLANES - k
                b2 = a2 * jnp.where(keep2, pltpu.roll(b2, SUBLANES - k, 0), 0.0) + b2
                a2 = a2 * jnp.where(keep2, pltpu.roll(a2, SUBLANES - k, 0), 1.0)
            hf = a * cf + b
            hb = a2 * cbk + b2
            h_scr[0, pl.ds(of, SUBLANES), :] = hf
            h_scr[1, pl.ds(ob, SUBLANES), :] = hb
            return hf[SUBLANES - 1:SUBLANES, :], hb[0:1, :]

        return lax.fori_loop(0, ng, body, (carry_f, carry_b), unroll=LRU_SCAN_UNROLL)

    n_ctx = rxc_ref.shape[1]
    n_lat = rxl_ref.shape[1]
    zero = jnp.zeros((1, LANES), F32)
    coeffs(rxc_ref, n_ctx)
    cf, cbk = scan(n_ctx, zero, zero)
    oc_ref[0] = ((h_scr[0, 0:n_ctx, :] + h_scr[1, 0:n_ctx, :]) * rgc_ref[0].astype(F32)).astype(BF16)
    coeffs(rxl_ref, n_lat)
    scan(n_lat, cf, cbk)
    ol_ref[0] = ((h_scr[0] + h_scr[1]) * rgl_ref[0].astype(F32)).astype(BF16)


def _lru_call(rx, rx_c, rg, rg_c, conv_w, conv_b, wa_bd, wx_bd, ba, bx, lam):
    bsz, t, _ = rx.shape
    tc = rx_c.shape[1]
    nch = LRU_WIDTH // LANES
    seq = lambda n: pl.BlockSpec((1, n, LANES), lambda b, j: (b, 0, j))
    vec = lambda r: pl.BlockSpec((r, 1, LANES), lambda b, j: (0, 0, j))
    wsp = pl.BlockSpec((2, 1, LANES, LANES), lambda b, j: (0, j, 0, 0))
    return pl.pallas_call(
        _lru_kernel,
        out_shape=[jax.ShapeDtypeStruct((bsz, t, LRU_WIDTH), BF16), jax.ShapeDtypeStruct((bsz, tc, LRU_WIDTH), BF16)],
        grid=(bsz, nch),
        in_specs=[seq(t), seq(tc), seq(t), seq(tc),
                  pl.BlockSpec((4, LANES), lambda b, j: (0, j)), pl.BlockSpec((1, LANES), lambda b, j: (0, j)),
                  wsp, wsp, vec(2), vec(2), vec(2)],
        out_specs=[seq(t), seq(tc)],
        scratch_shapes=[pltpu.VMEM((2, t, LANES), F32)] * 3,
        compiler_params=_params(2),
        name="lru",
    )(rx, rx_c, rg, rg_c, conv_w, conv_b, wa_bd, wx_bd, ba, bx, lam)


def _merge_kernel(attn_ref, hy_ref, lru_ref, gate_ref, h_ref, g1_ref, wa_ref, wh_ref, wl_ref, wo_ref,
                  n2_ref, sh2_ref, sc2_ref, rw_ref, hn_ref, u2_ref, lt_ref):
    def gate(j):
        return _sigmoid(gate_ref[0, :, j * D_MODEL:(j + 1) * D_MODEL].astype(F32))

    y = gate(0) * _dot(attn_ref[0], wa_ref[...])
    y = y + gate(1) * _dot(hy_ref[0], wh_ref[...])
    y = y + gate(2) * _dot(lru_ref[0], wl_ref[...])
    hn = h_ref[0] + g1_ref[0] * _dot(y.astype(BF16), wo_ref[...])
    hn_ref[0] = hn
    u2 = (_rms(hn) * n2_ref[...] * (1.0 + sc2_ref[0]) + sh2_ref[0]).astype(BF16)
    u2_ref[0] = u2
    lt_ref[0] = _dot_nt(rw_ref[...], u2)


def _merge_call(attn, hyo, lruo, gate, h, g1, wa, wh, wl, wo, n2g, sh2, sc2, rw_t, ctx_rows):
    bsz, t, d = h.shape
    tm = min(512, t)
    row = (lambda b: MOD_ROWS - 8) if ctx_rows else (lambda b: b)
    tok = lambda w: pl.BlockSpec((1, tm, w), lambda b, i: (b, i, 0))
    modspec = pl.BlockSpec((1, 1, d), lambda b, i: (row(b), 0, 0))
    const = lambda shape: pl.BlockSpec(shape, lambda b, i: (0,) * len(shape))
    return pl.pallas_call(
        _merge_kernel,
        out_shape=[jax.ShapeDtypeStruct((bsz, t, d), F32), jax.ShapeDtypeStruct((bsz, t, d), BF16),
                   jax.ShapeDtypeStruct((bsz, N_EXPERTS, t), F32)],
        grid=(bsz, t // tm),
        in_specs=[tok(Q_WIDTH), tok(HY_WIDTH), tok(LRU_WIDTH), tok(GATE_WIDTH), tok(d), modspec,
                  const((Q_WIDTH, d)), const((HY_WIDTH, d)), const((LRU_WIDTH, d)), const((d, d)),
                  const((1, d)), modspec, modspec, const((N_EXPERTS, d))],
        out_specs=[tok(d), tok(d), pl.BlockSpec((1, N_EXPERTS, tm), lambda b, i: (b, 0, i))],
        compiler_params=_params(2),
        name="merge",
    )(attn, hyo, lruo, gate, h, g1, wa, wh, wl, wo, n2g, sh2, sc2, rw_t)


def _router_kernel(lt_ref, slot_ref, slotc_ref, g_ref, coff_ref, *, cap):
    lg = lt_ref[0]
    n_e, t = lg.shape
    ex = jnp.exp(lg - lg.max(axis=0, keepdims=True))
    aff = ex / ex.sum(axis=0, keepdims=True)
    key = pltpu.bitcast(aff, I32)
    capf = float(cap)

    def count(mask):
        return jnp.where(mask, 1.0, 0.0).sum(axis=1, keepdims=True)

    def vbody(i, thr):
        sh = 28 - 2 * i
        out = thr
        for j in (1, 2, 3):
            cand = thr | lax.shift_left(jnp.int32(j), sh)
            out = jnp.where(count(key >= cand) >= capf, cand, out)
        return out

    thr = lax.fori_loop(0, 15, vbody, jnp.zeros((n_e, 1), I32))
    gt = key > thr
    eq = key == thr
    need = capf - count(gt)
    idx = lax.broadcasted_iota(I32, (n_e, t), 1)
    nbits = t.bit_length() - 1

    def ibody(i, lo):
        sh = nbits - 2 - 2 * i
        out = lo
        for j in (1, 2, 3):
            cand = lo | lax.shift_left(jnp.int32(j), sh)
            out = jnp.where(count(eq & (idx < cand)) < need, cand, out)
        return out

    last = lax.fori_loop(0, nbits // 2, ibody, jnp.zeros((n_e, 1), I32))
    if nbits % 2:
        last = jnp.where(count(eq & (idx < (last | 1))) < need, last | 1, last)
    sel = gt | (eq & (idx <= last))
    self32 = jnp.where(sel, 1.0, 0.0)
    g_ref[0] = jnp.where(sel, aff, 0.0)

    r_i = lax.broadcasted_iota(I32, (LANES, LANES), 0)
    c_i = lax.broadcasted_iota(I32, (LANES, LANES), 1)
    tri = jnp.where(r_i <= c_i, 1.0, 0.0).astype(BF16)
    eye = jnp.where(r_i == c_i, 1.0, 0.0).astype(BF16)
    off = jnp.zeros((n_e, 1), F32)
    n_units = t // LANES
    coff_ref[0] = jnp.zeros((n_e, LANES), I32)
    units = [slice(c * LANES, (c + 1) * LANES) for c in range(n_units)]
    incs = [_dot(self32[:, sl].astype(BF16), tri) for sl in units]
    slot1s = []
    for c, sl in enumerate(units):
        coff_ref[0, :, c:c + 1] = off.astype(I32)
        slot1 = jnp.where(sel[:, sl], incs[c] - self32[:, sl] + off + 1.0, 0.0)
        off = off + incs[c][:, LANES - 1:LANES]
        slot_ref[0, :, sl] = slot1.astype(I32) - 1
        slot1s.append(slot1)
    coff_ref[0, :, n_units:n_units + 1] = off.astype(I32)
    his = [jnp.floor(s1 * (1.0 / 16.0)) for s1 in slot1s]
    cols_hi = [_dot_nt(eye, hi.astype(BF16)) for hi in his]
    cols_lo = [_dot_nt(eye, (s1 - 16.0 * hi).astype(BF16)) for s1, hi in zip(slot1s, his)]
    for c, sl in enumerate(units):
        slotc_ref[0, sl, :] = (16.0 * cols_hi[c] + cols_lo[c]).astype(I32) - 1


def _router_call(logits_t, cap):
    bsz, n_e, t = logits_t.shape
    assert t % LANES == 0 and t // LANES < LANES
    row = pl.BlockSpec((1, n_e, t), lambda b: (b, 0, 0))
    slot_row, slot_col, g_row, coff = pl.pallas_call(
        functools.partial(_router_kernel, cap=cap),
        out_shape=[jax.ShapeDtypeStruct((bsz, n_e, t), I32), jax.ShapeDtypeStruct((bsz, t, n_e), I32),
                   jax.ShapeDtypeStruct((bsz, n_e, t), F32), jax.ShapeDtypeStruct((bsz, n_e, LANES), I32)],
        grid=(bsz,),
        in_specs=[row],
        out_specs=[row, pl.BlockSpec((1, t, n_e), lambda b: (b, 0, 0)), row,
                   pl.BlockSpec((1, n_e, LANES), lambda b: (b, 0, 0))],
        compiler_params=_params(1),
        name="router",
    )(logits_t)
    return slot_row, slot_col, g_row, coff[:, :, :t // LANES + 1].reshape(-1)


MOE_GATHER_TOKENS = 256
MOE_SLOT_ROWS = 128
MOE_SCATTER_TOKENS = 256
MOE_SCATTER_WINDOW = 64
MOE_SCATTER_DEPTH = 256
MOE_SCATTER_SLOTS = 256


def _log2(n):
    assert n & (n - 1) == 0
    return n.bit_length() - 1


def _expert_kernel(coff_ref, u_ref, slot_ref, g_ref, w1_ref, w3_ref, w2_ref, y_ref, xg_scr, gs_scr, *, tkg, sb,
                   n_units):
    cap = xg_scr.shape[0]
    n_chunks = u_ref.shape[1] // tkg
    base = (pl.program_id(0) * pl.num_programs(1) + pl.program_id(1)) * (n_units + 1)
    upc = tkg // LANES
    rid = lax.broadcasted_iota(I32, (sb, tkg), 0)
    xg_scr[...] = jnp.zeros(xg_scr.shape, F32)
    gs_scr[...] = jnp.zeros(gs_scr.shape, F32)

    def gather(match, r0, c_tokens, g_row):
        xg_scr[pl.ds(r0, sb), :] += _dot(jnp.where(match, 1.0, 0.0).astype(BF16), u_ref[0, c_tokens, :])
        gs_scr[pl.ds(r0, sb), :] += jnp.where(match, g_row, 0.0).sum(axis=1, keepdims=True)

    def window_start(c):
        lo = coff_ref[base + c * upc]
        return jnp.minimum(lo & (-SUBLANES), cap - sb)

    for c in range(n_chunks):
        r0 = pl.multiple_of(window_start(c), SUBLANES)
        gather(slot_ref[0, 0, c:c + 1, :] - r0 == rid, r0, slice(c * tkg, (c + 1) * tkg), g_ref[0, 0, c:c + 1, :])

    def overflow(c, carry):
        covered = window_start(c) + sb
        hi = coff_ref[base + (c + 1) * upc]
        first = lax.shift_right_logical(covered, _log2(sb))
        stop = jnp.where(hi > covered, lax.shift_right_logical(hi + (sb - 1), _log2(sb)), first)

        def block(j, carry2):
            b0 = pl.multiple_of(j * sb, sb)
            srow = slot_ref[0, 0, pl.ds(c, 1), :]
            gather((srow - b0 == rid) & (srow >= covered), b0, pl.ds(pl.multiple_of(c * tkg, tkg), tkg),
                   g_ref[0, 0, pl.ds(c, 1), :])
            return carry2

        return lax.fori_loop(first, stop, block, carry)

    lax.fori_loop(0, n_chunks, overflow, 0)
    xb = xg_scr[...].astype(BF16)
    hid = (_silu(_dot(xb, w1_ref[0, 0])) * _dot(xb, w3_ref[0, 0])).astype(BF16)
    y_ref[0, 0] = (_dot(hid, w2_ref[0, 0]) * gs_scr[...]).astype(BF16)


def _expert_call(u2, slot_row, g_row, coff, w1, w3, w2, layer, cap):
    bsz, t, d = u2.shape
    n_e = w1.shape[1]
    tkg = min(MOE_GATHER_TOKENS, t)
    sb = min(MOE_SLOT_ROWS, cap)
    wspec = pl.BlockSpec((1, 1, d, d), lambda b, e, co: (layer, e, 0, 0))
    chunks = pl.BlockSpec((1, 1, t // tkg, tkg), lambda b, e, co: (b, e, 0, 0))
    grid_spec = pltpu.PrefetchScalarGridSpec(
        num_scalar_prefetch=1,
        grid=(bsz, n_e),
        in_specs=[pl.BlockSpec((1, t, d), lambda b, e, co: (b, 0, 0)),
                  chunks, chunks, wspec, wspec, wspec],
        out_specs=pl.BlockSpec((1, 1, cap, d), lambda b, e, co: (b, e, 0, 0)),
        scratch_shapes=[pltpu.VMEM((cap, d), F32), pltpu.VMEM((cap, 1), F32)],
    )
    return pl.pallas_call(
        functools.partial(_expert_kernel, tkg=tkg, sb=sb, n_units=t // LANES),
        out_shape=jax.ShapeDtypeStruct((bsz, n_e, cap, d), BF16),
        grid_spec=grid_spec,
        compiler_params=_params(2),
        name="expert",
    )(coff, u2, slot_row.reshape(bsz, n_e, t // tkg, tkg), g_row.reshape(bsz, n_e, t // tkg, tkg), w1, w3, w2)


def _expert_call_folded(u2, slot_row, g_row, coff, w1, w3, w2, layer, cap):
    bsz, t, d = u2.shape
    n_e = slot_row.shape[1]
    n_units = t // LANES
    first = (jnp.arange(bsz, dtype=I32) * cap)[:, None, None]
    slot_all = jnp.where(slot_row >= 0, slot_row + first, -1).transpose(1, 0, 2).reshape(1, n_e, bsz * t)
    coff_all = (coff.reshape(bsz, n_e, n_units + 1)[:, :, :n_units] + first).transpose(1, 0, 2)
    coff_all = jnp.concatenate([coff_all.reshape(n_e, bsz * n_units), jnp.full((n_e, 1), bsz * cap, I32)], axis=1)
    g_all = g_row.transpose(1, 0, 2).reshape(1, n_e, bsz * t)
    y = _expert_call(u2.reshape(1, bsz * t, d), slot_all, g_all, coff_all.reshape(-1), w1, w3, w2, layer, bsz * cap)
    return y.reshape(n_e, bsz, cap, d).transpose(1, 0, 2, 3)


def _scatter_kernel(coff_ref, h_ref, y_ref, slotc_ref, g2_ref, fg_ref, o_ref, acc_scr, ystack, *, win, group, kb,
                    n_units, final_norm):
    tk = h_ref.shape[1]
    cap = y_ref.shape[2]
    upt = tk // LANES
    i = pl.program_id(1)
    bf16_rows = 2 * SUBLANES

    def span(e):
        base = (pl.program_id(0) * N_EXPERTS + e) * (n_units + 1)
        lo = coff_ref[base + i * upt]
        hi = coff_ref[base + (i + 1) * upt]
        return jnp.minimum(lo & (-bf16_rows), cap - win), hi

    acc = jnp.zeros((tk, D_MODEL), F32)
    lane_g = lax.broadcasted_iota(I32, (tk, group * win), 1)
    for p in range(N_EXPERTS // group):
        onehot = None
        for j in range(group):
            e = p * group + j
            r0 = pl.multiple_of(span(e)[0], bf16_rows)
            ystack[p, j * win:(j + 1) * win, :] = y_ref[0, e, pl.ds(r0, win), :]
            rel = slotc_ref[0, :, e:e + 1] - r0
            hit = (rel >= 0) & (rel < win) & (rel + j * win == lane_g)
            onehot = hit if onehot is None else onehot | hit
        acc = acc + _dot(jnp.where(onehot, 1.0, 0.0).astype(BF16), ystack[p])
    acc_scr[...] = acc

    lane_kb = lax.broadcasted_iota(I32, (tk, kb), 1)
    for e in range(N_EXPERTS):
        r0, hi = span(e)
        covered = r0 + win
        first = lax.shift_right_logical(covered, _log2(kb))
        stop = jnp.where(hi > covered, lax.shift_right_logical(hi + (kb - 1), _log2(kb)), first)

        def block(j, carry, e=e, covered=covered):
            b0 = pl.multiple_of(j * kb, kb)
            scol = slotc_ref[0, :, e:e + 1]
            onehot = jnp.where((scol - b0 == lane_kb) & (scol >= covered), 1.0, 0.0).astype(BF16)
            acc_scr[...] += _dot(onehot, y_ref[0, e, pl.ds(b0, kb), :])
            return carry

        lax.fori_loop(first, stop, block, 0)
    out = h_ref[0] + g2_ref[0] * acc_scr[...]
    o_ref[0] = _rms(out) * fg_ref[...] if final_norm else out


def _scatter_call(h, y, slot_col, coff, g2, cap, ctx_rows, final_g=None):
    bsz, t, d = h.shape
    n_e = y.shape[1]
    tk = min(MOE_SCATTER_TOKENS, t)
    win = min(MOE_SCATTER_WINDOW, cap)
    group = max(1, min(N_EXPERTS, MOE_SCATTER_DEPTH // win))
    kb = min(MOE_SCATTER_SLOTS, cap)
    row = (lambda b: MOD_ROWS - 8) if ctx_rows else (lambda b: b)
    tile = lambda w: pl.BlockSpec((1, tk, w), lambda b, i, co: (b, i, 0))
    grid_spec = pltpu.PrefetchScalarGridSpec(
        num_scalar_prefetch=1,
        grid=(bsz, t // tk),
        in_specs=[tile(d),
                  pl.BlockSpec((1, n_e, cap, d), lambda b, i, co: (b, 0, 0, 0)),
                  tile(n_e), pl.BlockSpec((1, 1, d), lambda b, i, co: (row(b), 0, 0)),
                  pl.BlockSpec((1, d), lambda b, i, co: (0, 0))],
        out_specs=tile(d),
        scratch_shapes=[pltpu.VMEM((tk, d), F32), pltpu.VMEM((n_e // group, group * win, d), BF16)],
    )
    return pl.pallas_call(
        functools.partial(_scatter_kernel, win=win, group=group, kb=kb, n_units=t // LANES,
                          final_norm=final_g is not None),
        out_shape=jax.ShapeDtypeStruct((bsz, t, d), F32),
        grid_spec=grid_spec,
        compiler_params=_params(2),
        name="moe_scatter",
    )(coff, h, y, slot_col, g2, jnp.ones((1, d), F32) if final_g is None else final_g)


def _rope_tables(n_lat, n_ctx):
    rows = n_lat // GRID_W
    row = jnp.repeat(jnp.arange(rows, dtype=F32), GRID_W)
    col = jnp.tile(jnp.arange(GRID_W, dtype=F32), rows)
    inv = jnp.power(ROPE_THETA, -jnp.arange(ROPE_PAIRS_AXIS, dtype=F32) / ROPE_PAIRS_AXIS)
    ang = jnp.concatenate([row[:, None] * inv, col[:, None] * inv], axis=-1)
    cos, sin = jnp.cos(ang), jnp.sin(ang)
    cs = jnp.concatenate([cos, cos], axis=-1)
    sn = jnp.concatenate([-sin, sin], axis=-1)
    return cs, sn, jnp.ones((n_ctx, HEAD_DIM), F32), jnp.zeros((n_ctx, HEAD_DIM), F32)


def _dft_tables(n):
    k = jnp.arange(n // 2, dtype=I32)

    def tables(first):
        m = ((2 * k[:, None] + 1) * (2 * k[None, :] + first)) % (4 * n)
        ang = m.astype(F32) * (2.0 * math.pi / (4 * n))
        return jnp.cos(ang), jnp.sin(ang)

    ce, se = tables(0)
    co, so = tables(1)
    fwd = tuple(x.astype(BF16) for x in (ce, co, se, so))
    inv = tuple(x.astype(BF16) for x in (ce.T, co, -se.T, -so))
    return fwd, inv


def _filter_features(n):
    t = jnp.linspace(0.0, 1.0, n, dtype=F32)[:, None]
    w = (2.0 * math.pi / n) * jnp.arange(n, dtype=F32)[:, None]
    f = jnp.linspace(1e-4, HY_BANDS - 1, HY_BANDS, dtype=F32)[None, :]
    feats = jnp.concatenate([t, jnp.cos(f * w), -jnp.sin(f * w)], axis=-1)
    feats = jnp.concatenate([feats[0::2], feats[1::2]], axis=0)
    return jnp.pad(feats, ((0, 0), (0, LANES - HY_EMB)))


def _pad_to(x, shape):
    return jnp.pad(x, [(0, s - d) for d, s in zip(x.shape, shape)])


def _block_diag_chunks(w):
    per = LANES // LRU_BLOCK
    w = w.reshape(2, LRU_BLOCKS // per, per, LRU_BLOCK, LRU_BLOCK)
    eye = jnp.eye(per, dtype=w.dtype)
    return jnp.einsum('dcpkj,pq->dcpkqj', w, eye).reshape(2, LRU_BLOCKS // per, LANES, LANES)


def kernel(x, c, ctx, c_ctx, mod_w, mod_b, norm1_g, norm2_g, w_in, q_norm_g, k_norm_g, hy_conv_w, hy_conv_b,
           hy_fw1, hy_fb1, hy_fw2, hy_fb2, hy_fw3, hy_freq, hy_bias, lru_conv_w, lru_conv_b, lru_wa, lru_ba,
           lru_wx, lru_bx, lru_lambda, w_attn_out, w_hy_out, w_lru_out, w_out, router_w, exp_w1, exp_w3, exp_w2,
           final_norm_g):
    bsz, n_lat, d = x.shape
    n_ctx = ctx.shape[1]
    depth = mod_w.shape[0]
    assert d == D_MODEL and bsz <= MOD_ROWS - 8 and n_lat % GRID_W == 0

    cs_l, sn_l, cs_c, sn_c = _rope_tables(n_lat, n_ctx)
    dft_l = _dft_tables(n_lat)
    dft_c = _dft_tables(n_ctx)
    feats_l = _filter_features(n_lat)
    feats_c = _filter_features(n_ctx)
    max_decay = math.log(HY_DECAY_TARGET) / HY_FAST_DECAY_PCT
    min_decay = math.log(HY_DECAY_TARGET) / HY_SLOW_DECAY_PCT
    deltas_abs = jnp.abs(jnp.linspace(min_decay, max_decay, HY_WIDTH, dtype=F32))[None, :]
    cap_l = max(1, EC_CAPACITY * n_lat // N_EXPERTS)
    cap_c = max(1, EC_CAPACITY * n_ctx // N_EXPERTS)

    cvec = jnp.zeros((MOD_ROWS, d), F32).at[:bsz].set(c).at[MOD_ROWS - 8].set(c_ctx)
    mod = _mod_call(cvec, mod_w, mod_b)
    mod = mod.reshape(depth, MOD_ROWS, 6, 1, d).transpose(0, 2, 1, 3, 4)

    w_in_b = w_in.astype(BF16)
    e1 = exp_w1.astype(BF16)
    e3 = exp_w3.astype(BF16)
    e2 = exp_w2.astype(BF16)

    h, hc = x, ctx
    for l in range(depth):
        last = l == depth - 1
        sh1, sc1, g1, sh2, sc2, g2 = (mod[l, j] for j in range(6))
        n1 = norm1_g[l][None, :]
        n2 = norm2_g[l][None, :]
        qg = q_norm_g[l][None, :]
        kg = k_norm_g[l][None, :]

        q, k, v, hy, rx, rg, gate = _inproj_call(h, sh1, sc1, n1, w_in_b, l, cs_l, sn_l, qg, kg, False)
        qc, kc, vc, hyc, rxc, rgc, gatec = _inproj_call(hc, sh1, sc1, n1, w_in_b, l, cs_c, sn_c, qg, kg, True)

        attn = _attn_call(q, [(k, v), (kc, vc)])

        fw1 = _pad_to(hy_fw1[l], (LANES, LANES))
        fb1 = _pad_to(hy_fb1[l][None, :], (1, LANES))
        fw2 = _pad_to(hy_fw2[l], (LANES, LANES))
        fb2 = _pad_to(hy_fb2[l][None, :], (1, LANES))
        fw3 = _pad_to(hy_fw3[l], (LANES, 2 * HY_WIDTH))
        freq = _pad_to(hy_freq[l], (2, LANES))
        hbias = hy_bias[l][None, :]

        def hyena(hy_in, feats, tabs):
            tabs_fwd, tabs_inv = tabs
            hsd = _hy_filter_call(feats, fw1, fb1, fw2, fb2, fw3, freq, deltas_abs)
            kspec = _hy_kspec_call(tabs_fwd, hsd)
            x0, we, wo = _hy_pre_call(hy_in, hy_conv_w[l], hy_conv_b[l][None, :])
            us = _hy_fwd_call(tabs_fwd, we, wo, kspec)
            return _hy_inv_call(tabs_inv, us, x0, we, wo, hbias)

        hyo = hyena(hy, feats_l, dft_l)

        lruo, lruoc = _lru_call(rx, rxc, rg, rgc, lru_conv_w[l], lru_conv_b[l][None, :],
                                _block_diag_chunks(lru_wa[l]).astype(BF16), _block_diag_chunks(lru_wx[l]).astype(BF16),
                                lru_ba[l][:, None, :], lru_bx[l][:, None, :], lru_lambda[l][:, None, :])

        wa = w_attn_out[l].astype(BF16)
        wh = w_hy_out[l].astype(BF16)
        wl = w_lru_out[l].astype(BF16)
        wo = w_out[l].astype(BF16)
        rw_t = router_w[l].T.astype(BF16)

        def channel_mix(h_in, attn_in, hy_in, lru_in, gate_in, cap, ctx_rows, final_g=None):
            hn, u2, lt = _merge_call(attn_in, hy_in, lru_in, gate_in, h_in, g1, wa, wh, wl, wo, n2, sh2, sc2, rw_t,
                                     ctx_rows)
            slot_row, slot_col, g_row, coff = _router_call(lt, cap)
            if ctx_rows:
                y = _expert_call_folded(u2, slot_row, g_row, coff, e1, e3, e2, l, cap)
            else:
                y = _expert_call(u2, slot_row, g_row, coff, e1, e3, e2, l, cap)
            return _scatter_call(hn, y, slot_col, coff, g2, cap, ctx_rows, final_g)

        h = channel_mix(h, attn, hyo, lruo, gate, cap_l, False, final_norm_g[None, :] if last else None)
        if not last:
            attn_c = _attn_call(qc, [(kc, vc)])
            hyo_c = hyena(hyc, feats_c, dft_c)
            hc = channel_mix(hc, attn_c, hyo_c, lruoc, gatec, cap_c, True)

    return h
```
